```python
import math
import jax
import jax.numpy as jnp
from jax import lax
import numpy as np


D_MODEL = 2048
BATCH = 2
SEQ = 4096
DEPTH = 4
DEC_BATCH = 4
DEC_SEQ = 2048
PAST_LEN = 128

HEAD_DIM = 128
BLOCK = 128
A_HEADS = 4
A_V = 2 * HEAD_DIM
B_HEADS = 8
B_KV_HEADS = 2
WINDOW = 128
C_HEADS = 8
C_KV_HEADS = 2
GRID_W = 64
ROPE_THETA = 10000.0
NUM_BUCKETS = 32
MAX_DISTANCE = 128
BIAS_HEADS = A_HEADS + B_HEADS
MEM_TOKENS = 256
MEM_HEADS = 4
MEM_INNER = MEM_HEADS * HEAD_DIM
D_FF = 5632
N_BRANCH = 3
BRANCH_WIDTH = 1024
IN_SIZES = (A_HEADS * 2 * HEAD_DIM, A_HEADS * 2 * HEAD_DIM, A_HEADS * A_V,
            B_HEADS * HEAD_DIM, B_KV_HEADS * HEAD_DIM, B_KV_HEADS * HEAD_DIM,
            C_HEADS * HEAD_DIM, C_KV_HEADS * HEAD_DIM, C_KV_HEADS * HEAD_DIM,
            N_BRANCH * D_MODEL)
IN_WIDTH = 12288
ALPHA = (2 * DEPTH) ** 0.25
BETA = (8 * DEPTH) ** -0.25
LN_EPS = 1e-5
RMS_EPS = 1e-6
NEG_INF = -1e30

kernel_name = 'hybrid_gated_encoder_trunk'


def _split_points(sizes):
    pts, acc = [], 0
    for s in sizes[:-1]:
        acc += s
        pts.append(acc)
    return pts


def layer_norm(x, g, b):
    xf = x.astype(jnp.float32)
    mu = jnp.mean(xf, axis=-1, keepdims=True)
    var = jnp.mean(jnp.square(xf - mu), axis=-1, keepdims=True)
    return ((xf - mu) * lax.rsqrt(var + LN_EPS) * g.astype(jnp.float32) + b.astype(jnp.float32)).astype(x.dtype)


def rms_norm(x, g):
    xf = x.astype(jnp.float32)
    return (xf * lax.rsqrt(jnp.mean(xf * xf, axis=-1, keepdims=True) + RMS_EPS) * g.astype(jnp.float32)).astype(x.dtype)


def t5_bucket(rel):
    half = NUM_BUCKETS // 2
    max_exact = half // 2
    ret = jnp.where(rel > 0, half, 0)
    n = jnp.abs(rel)
    large = max_exact + (jnp.log(jnp.maximum(n, 1).astype(jnp.float32) / max_exact)
                         / math.log(MAX_DISTANCE / max_exact) * (half - max_exact)).astype(jnp.int32)
    large = jnp.minimum(large, half - 1)
    return ret + jnp.where(n < max_exact, n, large)


def axial_rope_tables(S):
    n_rows = S // GRID_W
    row = jnp.broadcast_to(jnp.arange(n_rows)[:, None], (n_rows, GRID_W)).reshape(-1).astype(jnp.float32)
    col = jnp.broadcast_to(jnp.arange(GRID_W)[None, :], (n_rows, GRID_W)).reshape(-1).astype(jnp.float32)
    axis_dims = HEAD_DIM // 2
    inv = ROPE_THETA ** (-jnp.arange(0, axis_dims, 2, dtype=jnp.float32) / axis_dims)
    ang = jnp.concatenate([row[:, None] * inv[None, :], col[:, None] * inv[None, :]], axis=-1)
    return jnp.cos(ang), jnp.sin(ang)


def apply_rope(x, cos, sin):
    half = HEAD_DIM // 2
    x1, x2 = x[..., :half], x[..., half:]
    c = cos[None, :, None, :]
    s = sin[None, :, None, :]
    return jnp.concatenate([x1 * c - x2 * s, x1 * s + x2 * c], axis=-1).astype(x.dtype)


def swiglu(x, w_gu, w_d):
    g, u = jnp.split(x @ w_gu, 2, axis=-1)
    return (jax.nn.silu(g) * u) @ w_d


def diff_attention(q, k, v, lam, subln_g, table_a, lambda_init):
    B, S = q.shape[0], q.shape[1]
    nb = S // BLOCK
    scale = HEAD_DIM ** -0.5
    lamf = lam.astype(jnp.float32)
    lam_full = jnp.exp(jnp.sum(lamf[0] * lamf[1])) - jnp.exp(jnp.sum(lamf[2] * lamf[3])) + lambda_init
    k_pos = jnp.arange(S)
    qb = jnp.moveaxis(q.reshape(B, nb, BLOCK, A_HEADS, 2, HEAD_DIM), 1, 0)

    def block(args):
        qi, i = args
        q_pos = i * BLOCK + jnp.arange(BLOCK)
        bias = jnp.moveaxis(table_a[t5_bucket(k_pos[None, :] - q_pos[:, None])], -1, 0)
        s = jnp.einsum('bqhmd,bshmd->bhmqs', qi, k).astype(jnp.float32) * scale
        p = jax.nn.softmax(s + bias[None, :, None].astype(jnp.float32), axis=-1)
        attn = p[:, :, 0] - lam_full * p[:, :, 1]
        return jnp.einsum('bhqs,bshe->bqhe', attn.astype(v.dtype), v)

    o = lax.map(block, (qb, jnp.arange(nb)))
    o = jnp.moveaxis(o, 0, 1).reshape(B, S, A_HEADS, A_V)
    o = rms_norm(o, subln_g) * (1.0 - lambda_init)
    return o.reshape(B, S, A_HEADS * A_V)


def window_attention(q, k, v, sink, table_b):
    B, S = q.shape[0], q.shape[1]
    nb = S // BLOCK
    G = B_HEADS // B_KV_HEADS
    scale = HEAD_DIM ** -0.5
    pad = ((0, 0), (BLOCK, BLOCK), (0, 0), (0, 0))
    kp = jnp.pad(k, pad).reshape(B, nb + 2, BLOCK, B_KV_HEADS, HEAD_DIM)
    vp = jnp.pad(v, pad).reshape(B, nb + 2, BLOCK, B_KV_HEADS, HEAD_DIM)
    kw = jnp.concatenate([kp[:, :-2], kp[:, 1:-1], kp[:, 2:]], axis=2)
    vw = jnp.concatenate([vp[:, :-2], vp[:, 1:-1], vp[:, 2:]], axis=2)
    qb = q.reshape(B, nb, BLOCK, B_KV_HEADS, G, HEAD_DIM)
    s = jnp.einsum('bnqhgd,bnkhd->bnhgqk', qb, kw).astype(jnp.float32) * scale
    q_off = jnp.arange(BLOCK)
    k_off = jnp.arange(3 * BLOCK) - BLOCK
    rel = k_off[None, :] - q_off[:, None]
    k_abs = jnp.arange(nb)[:, None] * BLOCK + k_off[None, :]
    valid = (jnp.abs(rel) <= WINDOW)[None] & ((k_abs >= 0) & (k_abs < S))[:, None, :]
    bias = jnp.moveaxis(table_b[t5_bucket(rel)], -1, 0).reshape(B_KV_HEADS, G, BLOCK, 3 * BLOCK)
    s = jnp.where(valid[None, :, None, None], s + bias.astype(jnp.float32), NEG_INF)
    sink_l = jnp.broadcast_to(sink.reshape(B_KV_HEADS, G)[None, None, :, :, None, None].astype(jnp.float32),
                              s.shape[:-1] + (1,))
    p = jax.nn.softmax(jnp.concatenate([s, sink_l], axis=-1), axis=-1)[..., :-1]
    o = jnp.einsum('bnhgqk,bnkhd->bnqhgd', p.astype(v.dtype), vw)
    return o.reshape(B, S, B_HEADS * HEAD_DIM)


def axial_attention(q, k, v, q_g, k_g, cos, sin):
    B, S = q.shape[0], q.shape[1]
    nb = S // BLOCK
    G = C_HEADS // C_KV_HEADS
    scale = HEAD_DIM ** -0.5
    q = apply_rope(rms_norm(q, q_g), cos, sin)
    k = apply_rope(rms_norm(k, k_g), cos, sin)
    qb = jnp.moveaxis(q.reshape(B, nb, BLOCK, C_KV_HEADS, G, HEAD_DIM), 1, 0)

    def block(qi):
        s = jnp.einsum('bqhgd,bshd->bhgqs', qi, k).astype(jnp.float32) * scale
        p = jax.nn.softmax(s, axis=-1)
        return jnp.einsum('bhgqs,bshd->bqhgd', p.astype(v.dtype), v)

    o = lax.map(block, qb)
    return jnp.moveaxis(o, 0, 1).reshape(B, S, C_HEADS * HEAD_DIM)


def parallel_mixer(x, w_in, w_branch, w_out, lam, subln_g, sink, qk_g, rel_bias_table, lambda_init, cos, sin):
    B, S, _ = x.shape
    parts = jnp.split(x @ w_in, _split_points(IN_SIZES), axis=-1)
    qa = parts[0].reshape(B, S, A_HEADS, 2, HEAD_DIM)
    ka = parts[1].reshape(B, S, A_HEADS, 2, HEAD_DIM)
    va = parts[2].reshape(B, S, A_HEADS, A_V)
    qb = parts[3].reshape(B, S, B_HEADS, HEAD_DIM)
    kb = parts[4].reshape(B, S, B_KV_HEADS, HEAD_DIM)
    vb = parts[5].reshape(B, S, B_KV_HEADS, HEAD_DIM)
    qc = parts[6].reshape(B, S, C_HEADS, HEAD_DIM)
    kc = parts[7].reshape(B, S, C_KV_HEADS, HEAD_DIM)
    vc = parts[8].reshape(B, S, C_KV_HEADS, HEAD_DIM)
    gates = jax.nn.sigmoid(parts[9].reshape(B, S, N_BRANCH, D_MODEL))
    ya = diff_attention(qa, ka, va, lam, subln_g, rel_bias_table[:, :A_HEADS], lambda_init)
    yb = window_attention(qb, kb, vb, sink, rel_bias_table[:, A_HEADS:])
    yc = axial_attention(qc, kc, vc, qk_g[0], qk_g[1], cos, sin)
    ys = jnp.stack([ya, yb, yc], axis=2)
    branches = jnp.einsum('bsnc,ncd->bsnd', ys, w_branch)
    merged = jnp.sum(gates * branches, axis=2)
    return merged @ w_out


def memory_attention(x, mem, w_q, w_kv, w_o):
    B, S, _ = x.shape
    M = mem.shape[1]
    scale = HEAD_DIM ** -0.5
    q = (x @ w_q).reshape(B, S, MEM_HEADS, HEAD_DIM)
    kv = (mem @ w_kv).reshape(B, M, 2, MEM_HEADS, HEAD_DIM)
    s = jnp.einsum('bqhd,bmhd->bhqm', q, kv[:, :, 0]).astype(jnp.float32) * scale
    p = jax.nn.softmax(s, axis=-1)
    o = jnp.einsum('bhqm,bmhd->bqhd', p.astype(x.dtype), kv[:, :, 1]).reshape(B, S, MEM_INNER)
    return o @ w_o


def run_trunk(x, mem, rel_bias_table, w_in, w_branch, w_out, lambda_qk, subln_g, sink_logits, qk_norm_g,
              w_mem_q, w_mem_kv, w_mem_o, w_ffn_gu, w_ffn_d, ln_g, ln_b):
    cos, sin = axial_rope_tables(x.shape[1])
    for i in range(DEPTH):
        lambda_init = 0.8 - 0.6 * math.exp(-0.3 * i)
        x = layer_norm(ALPHA * x + 0.5 * swiglu(x, w_ffn_gu[i, 0], w_ffn_d[i, 0]), ln_g[i, 0], ln_b[i, 0])
        x = layer_norm(ALPHA * x + parallel_mixer(x, w_in[i], w_branch[i], w_out[i], lambda_qk[i], subln_g[i],
                                                  sink_logits[i], qk_norm_g[i], rel_bias_table, lambda_init, cos, sin),
                       ln_g[i, 1], ln_b[i, 1])
        x = layer_norm(ALPHA * x + memory_attention(x, mem, w_mem_q[i], w_mem_kv[i], w_mem_o[i]), ln_g[i, 2], ln_b[i, 2])
        x = layer_norm(ALPHA * x + 0.5 * swiglu(x, w_ffn_gu[i, 1], w_ffn_d[i, 1]), ln_g[i, 3], ln_b[i, 3])
    return x


def setup_inputs(seed: int = 0) -> dict:
    key = jax.random.key(seed)
    ks = jax.random.split(key, 19)
    f32 = jnp.float32
    n = lambda k, shape: jax.random.normal(k, shape, dtype=f32)
    return {
        'x_prompt': n(ks[0], (BATCH, SEQ, D_MODEL)),
        'x_sample': n(ks[1], (DEC_BATCH, DEC_SEQ, D_MODEL)),
        'mem_prompt': n(ks[2], (BATCH, MEM_TOKENS, D_MODEL)),
        'mem_sample': n(ks[3], (DEC_BATCH, MEM_TOKENS, D_MODEL)),
        'rel_bias_table': 0.5 * n(ks[4], (NUM_BUCKETS, BIAS_HEADS)),
        'w_in': n(ks[5], (DEPTH, D_MODEL, IN_WIDTH)) * D_MODEL ** -0.5,
        'w_branch': n(ks[6], (DEPTH, N_BRANCH, BRANCH_WIDTH, D_MODEL)) * BRANCH_WIDTH ** -0.5,
        'w_out': n(ks[7], (DEPTH, D_MODEL, D_MODEL)) * (D_MODEL ** -0.5 * BETA),
        'lambda_qk': 0.1 * n(ks[8], (DEPTH, 4, HEAD_DIM)),
        'subln_g': 1.0 + 0.05 * n(ks[9], (DEPTH, A_V)),
        'sink_logits': n(ks[10], (DEPTH, B_HEADS)),
        'qk_norm_g': 1.0 + 0.05 * n(ks[11], (DEPTH, 2, HEAD_DIM)),
        'w_mem_q': n(ks[12], (DEPTH, D_MODEL, MEM_INNER)) * D_MODEL ** -0.5,
        'w_mem_kv': n(ks[13], (DEPTH, D_MODEL, 2 * MEM_INNER)) * D_MODEL ** -0.5,
        'w_mem_o': n(ks[14], (DEPTH, MEM_INNER, D_MODEL)) * (MEM_INNER ** -0.5 * BETA),
        'w_ffn_gu': n(ks[15], (DEPTH, 2, D_MODEL, 2 * D_FF)) * D_MODEL ** -0.5,
        'w_ffn_d': n(ks[16], (DEPTH, 2, D_FF, D_MODEL)) * (D_FF ** -0.5 * BETA),
        'ln_g': 1.0 + 0.05 * n(ks[17], (DEPTH, 4, D_MODEL)),
        'ln_b': 0.02 * n(ks[18], (DEPTH, 4, D_MODEL)),
    }


def reference(x_prompt, x_sample, mem_prompt, mem_sample, rel_bias_table, w_in, w_branch, w_out, lambda_qk,
              subln_g, sink_logits, qk_norm_g, w_mem_q, w_mem_kv, w_mem_o, w_ffn_gu, w_ffn_d, ln_g, ln_b):
    y_prompt = run_trunk(x_prompt, mem_prompt, rel_bias_table, w_in, w_branch, w_out, lambda_qk, subln_g,
                         sink_logits, qk_norm_g, w_mem_q, w_mem_kv, w_mem_o, w_ffn_gu, w_ffn_d, ln_g, ln_b)
    y_sample = run_trunk(x_sample, mem_sample, rel_bias_table, w_in, w_branch, w_out, lambda_qk, subln_g,
                         sink_logits, qk_norm_g, w_mem_q, w_mem_kv, w_mem_o, w_ffn_gu, w_ffn_d, ln_g, ln_b)
    return (y_prompt, y_sample)
```

```python
import functools
import math

import numpy as np
import jax
import jax.numpy as jnp
from jax import lax
from jax.experimental import pallas as pl
from jax.experimental.pallas import tpu as pltpu

D_MODEL = 2048
DEPTH = 4
HEAD_DIM = 128
A_HEADS = 4
A_V = 2 * HEAD_DIM
B_HEADS = 8
B_KV_HEADS = 2
WINDOW = 128
C_HEADS = 8
C_KV_HEADS = 2
GRID_W = 64
ROPE_THETA = 10000.0
NUM_BUCKETS = 32
MAX_DISTANCE = 128
MEM_TOKENS = 256
MEM_HEADS = 4
MEM_INNER = MEM_HEADS * HEAD_DIM
D_FF = 5632
N_BRANCH = 3
BRANCH_WIDTH = 1024
QKV_WIDTH = 6144
ALPHA = (2 * DEPTH) ** 0.25
LN_EPS = 1e-5
RMS_EPS = 1e-6
NEG_INF = -1e30
SCALE = HEAD_DIM ** -0.5

QA_OFF, KA_OFF, VA_OFF = 0, 1024, 2048
QB_OFF, KB_OFF, VB_OFF = 3072, 4096, 4352
QC_OFF, KC_OFF, VC_OFF = 4608, 5632, 5888

V7X_VMEM_BYTES = 64 * 1024 * 1024
VMEM_LIMIT = 56 * 1024 * 1024

F32 = jnp.float32
BF16 = jnp.bfloat16


def _params(semantics):
    return pltpu.CompilerParams(dimension_semantics=semantics, vmem_limit_bytes=VMEM_LIMIT)


def _layer_norm(y, g, b):
    mu = jnp.mean(y, axis=-1, keepdims=True)
    d = y - mu
    var = jnp.mean(d * d, axis=-1, keepdims=True)
    return d * lax.rsqrt(var + LN_EPS) * g + b


def _dot(a, b):
    return jnp.dot(a, b, preferred_element_type=F32)


def _dot_nt(a, b):
    return lax.dot_general(a, b, (((1,), (1,)), ((), ())), preferred_element_type=F32)


def _lane_fold(x, op):
    parts = [x[:, j * 128:(j + 1) * 128] for j in range(x.shape[1] // 128)]
    return functools.reduce(op, parts)


def _ffn_body(x_ref, wg_ref, wu_ref, wd_ref, g_ref, b_ref, o_ref, xb_ref, acc_ref, *, n_f):
    f = pl.program_id(1)

    @pl.when(f == 0)
    def _():
        xb_ref[...] = x_ref[...].astype(BF16)

    xb = xb_ref[...]
    h = _dot(xb, wg_ref[...])
    u = _dot(xb, wu_ref[...])
    a = (h * jax.nn.sigmoid(h) * u).astype(BF16)
    d = _dot(a, wd_ref[...])

    @pl.when(f == 0)
    def _():
        acc_ref[...] = d

    @pl.when(f > 0)
    def _():
        acc_ref[...] += d

    @pl.when(f == n_f - 1)
    def _():
        y = ALPHA * x_ref[...] + 0.5 * acc_ref[...]
        o_ref[...] = _layer_norm(y, g_ref[...], b_ref[...])


def _ffn(x, w_gu, w_d, ln_g, ln_b, layer, which, ln_idx, *, tm=512, tf=512):
    m = x.shape[0]
    n_f = D_FF // tf
    return pl.pallas_call(
        functools.partial(_ffn_body, n_f=n_f),
        out_shape=jax.ShapeDtypeStruct((m, D_MODEL), F32),
        grid=(m // tm, n_f),
        in_specs=[
            pl.BlockSpec((tm, D_MODEL), lambda i, f: (i, 0)),
            pl.BlockSpec((None, None, D_MODEL, tf), lambda i, f: (layer, which, 0, f)),
            pl.BlockSpec((None, None, D_MODEL, tf), lambda i, f: (layer, which, 0, f + n_f)),
            pl.BlockSpec((None, None, tf, D_MODEL), lambda i, f: (layer, which, f, 0)),
            pl.BlockSpec((None, None, 1, D_MODEL), lambda i, f: (layer, ln_idx, 0, 0)),
            pl.BlockSpec((None, None, 1, D_MODEL), lambda i, f: (layer, ln_idx, 0, 0)),
        ],
        out_specs=pl.BlockSpec((tm, D_MODEL), lambda i, f: (i, 0)),
        scratch_shapes=[pltpu.VMEM((tm, D_MODEL), BF16), pltpu.VMEM((tm, D_MODEL), F32)],
        compiler_params=_params(("parallel", "arbitrary")),
        name="ffn",
    )(x, w_gu, w_gu, w_d, ln_g, ln_b)


def _qkv_body(x_ref, w_ref, gn_ref, cos_ref, sin_ref, o_ref, *, tn):
    xb = x_ref[...].astype(BF16)
    cos = cos_ref[...]
    sin = sin_ref[...]
    for j in range(QKV_WIDTH // tn):
        c0 = j * tn
        acc = _dot(xb, w_ref[:, c0:c0 + tn])
        is_q = (QA_OFF <= c0 < KA_OFF) or (QB_OFF <= c0 < KB_OFF) or (QC_OFF <= c0 < KC_OFF)
        if QC_OFF <= c0 < VC_OFF:
            g = gn_ref[0:1, :] if c0 < KC_OFF else gn_ref[1:2, :]
            for hh in range(tn // HEAD_DIM):
                a = acc[:, hh * HEAD_DIM:(hh + 1) * HEAD_DIM]
                r = lax.rsqrt(jnp.mean(a * a, axis=-1, keepdims=True) + RMS_EPS)
                y = a * r * g
                y = y * cos + pltpu.roll(y, HEAD_DIM // 2, 1) * sin
                if is_q:
                    y = y * SCALE
                o_ref[:, c0 + hh * HEAD_DIM:c0 + (hh + 1) * HEAD_DIM] = y.astype(BF16)
        else:
            if is_q:
                acc = acc * SCALE
            o_ref[:, c0:c0 + tn] = acc.astype(BF16)


def _qkv(x, w_qkv, qk_norm_g, cos2, sin2, layer, *, tm=512, tn=256):
    m = x.shape[0]
    s = cos2.shape[0]
    n_pos = s // tm
    return pl.pallas_call(
        functools.partial(_qkv_body, tn=tn),
        out_shape=jax.ShapeDtypeStruct((m, QKV_WIDTH), BF16),
        grid=(m // tm,),
        in_specs=[
            pl.BlockSpec((tm, D_MODEL), lambda i: (i, 0)),
            pl.BlockSpec((None, D_MODEL, QKV_WIDTH), lambda i: (layer, 0, 0), pipeline_mode=pl.Buffered(1)),
            pl.BlockSpec((None, 2, HEAD_DIM), lambda i: (layer, 0, 0)),
            pl.BlockSpec((tm, HEAD_DIM), lambda i: (i % n_pos, 0)),
            pl.BlockSpec((tm, HEAD_DIM), lambda i: (i % n_pos, 0)),
        ],
        out_specs=pl.BlockSpec((tm, QKV_WIDTH), lambda i: (i, 0)),
        compiler_params=_params(("parallel",)),
        name="qkv",
    )(x, w_qkv, qk_norm_g, cos2, sin2)


def _diff_body(q_ref, k_ref, v_ref, bias_ref, lam_ref, sg_ref, o_ref, s1_ref, s2_ref, acc_ref, *, t, nc, lambda_init):
    i = pl.program_id(2)
    q = q_ref[...]
    q1 = q[:, :HEAD_DIM]
    q2 = q[:, HEAD_DIM:]

    def scores(c, carry):
        m1, m2 = carry
        kc = k_ref[pl.ds(pl.multiple_of(c * t, t), t), :]
        b = bias_ref[jnp.clip(c - i, -2, 2) + 2]
        s1 = _dot_nt(q1, kc[:, :HEAD_DIM]) + b
        s2 = _dot_nt(q2, kc[:, HEAD_DIM:]) + b
        s1_ref[c] = s1
        s2_ref[c] = s2
        return jnp.maximum(m1, _lane_fold(s1, jnp.maximum)), jnp.maximum(m2, _lane_fold(s2, jnp.maximum))

    neg = jnp.full((t, 128), -jnp.inf, F32)
    m1, m2 = lax.fori_loop(0, nc, scores, (neg, neg))
    m1 = jnp.max(m1, axis=-1, keepdims=True)
    m2 = jnp.max(m2, axis=-1, keepdims=True)

    acc_ref[...] = jnp.zeros_like(acc_ref)

    def weighted(c, carry):
        l1, l2 = carry
        e1 = jnp.exp(s1_ref[c] - m1)
        e2 = jnp.exp(s2_ref[c] - m2)
        e = jnp.concatenate([e1.astype(BF16), e2.astype(BF16)], axis=0)
        vc = v_ref[pl.ds(pl.multiple_of(c * t, t), t), :]
        acc_ref[...] += _dot(e, vc)
        return l1 + _lane_fold(e1, jnp.add), l2 + _lane_fold(e2, jnp.add)

    zero = jnp.zeros((t, 128), F32)
    l1, l2 = lax.fori_loop(0, nc, weighted, (zero, zero))
    l1 = jnp.sum(l1, axis=-1, keepdims=True)
    l2 = jnp.sum(l2, axis=-1, keepdims=True)

    lam = lam_ref[...]
    lam_full = (jnp.exp(jnp.sum(lam[0:1] * lam[1:2], axis=-1, keepdims=True))
                - jnp.exp(jnp.sum(lam[2:3] * lam[3:4], axis=-1, keepdims=True)) + lambda_init)
    o = acc_ref[0:t, :] / l1 - lam_full * (acc_ref[t:2 * t, :] / l2)
    o = o * lax.rsqrt(jnp.mean(o * o, axis=-1, keepdims=True) + RMS_EPS) * sg_ref[...]
    o_ref[...] = (o * (1.0 - lambda_init)).astype(BF16)


def _diff_attention(qkv, bias_a, lambda_qk, subln_g, layer, n_seq, s, lambda_init, *, t=256):
    m = qkv.shape[0]
    nc = s // t
    return pl.pallas_call(
        functools.partial(_diff_body, t=t, nc=nc, lambda_init=lambda_init),
        out_shape=jax.ShapeDtypeStruct((m, BRANCH_WIDTH), BF16),
        grid=(n_seq, A_HEADS, nc),
        in_specs=[
            pl.BlockSpec((t, A_V), lambda b, h, i: (b * nc + i, QA_OFF // A_V + h)),
            pl.BlockSpec((s, A_V), lambda b, h, i: (b, KA_OFF // A_V + h)),
            pl.BlockSpec((s, A_V), lambda b, h, i: (b, VA_OFF // A_V + h)),
            pl.BlockSpec((None, 5, t, t), lambda b, h, i: (h, 0, 0, 0)),
            pl.BlockSpec((None, 4, HEAD_DIM), lambda b, h, i: (layer, 0, 0)),
            pl.BlockSpec((None, 1, A_V), lambda b, h, i: (layer, 0, 0)),
        ],
        out_specs=pl.BlockSpec((t, A_V), lambda b, h, i: (b * nc + i, h)),
        scratch_shapes=[pltpu.VMEM((nc, t, t), F32), pltpu.VMEM((nc, t, t), F32), pltpu.VMEM((2 * t, A_V), F32)],
        compiler_params=_params(("parallel", "parallel", "arbitrary")),
        name="diff_attn",
    )(qkv, qkv, qkv, bias_a, lambda_qk, subln_g)


def _axial_body(q_ref, k_ref, v_ref, o_ref, s_ref, acc_ref, *, t, tk, nc):
    g = C_HEADS // C_KV_HEADS
    q = q_ref[...]
    qs = jnp.concatenate([q[:, h * HEAD_DIM:(h + 1) * HEAD_DIM] for h in range(g)], axis=0)

    def scores(c, m):
        kc = k_ref[pl.ds(pl.multiple_of(c * tk, tk), tk), :]
        sc = _dot_nt(qs, kc)
        s_ref[c] = sc
        return jnp.maximum(m, _lane_fold(sc, jnp.maximum))

    m = lax.fori_loop(0, nc, scores, jnp.full((g * t, 128), -jnp.inf, F32))
    m = jnp.max(m, axis=-1, keepdims=True)
    acc_ref[...] = jnp.zeros_like(acc_ref)

    def weighted(c, l):
        e = jnp.exp(s_ref[c] - m)
        vc = v_ref[pl.ds(pl.multiple_of(c * tk, tk), tk), :]
        acc_ref[...] += _dot(e.astype(BF16), vc)
        return l + _lane_fold(e, jnp.add)

    l = lax.fori_loop(0, nc, weighted, jnp.zeros((g * t, 128), F32))
    l = jnp.sum(l, axis=-1, keepdims=True)
    o = acc_ref[...] / l
    for h in range(g):
        o_ref[:, h * HEAD_DIM:(h + 1) * HEAD_DIM] = o[h * t:(h + 1) * t, :].astype(BF16)


def _axial_attention(qkv, n_seq, s, *, t=256, tk=512):
    m = qkv.shape[0]
    g = C_HEADS // C_KV_HEADS
    gw = g * HEAD_DIM
    nq = s // t
    nc = s // tk
    return pl.pallas_call(
        functools.partial(_axial_body, t=t, tk=tk, nc=nc),
        out_shape=jax.ShapeDtypeStruct((m, BRANCH_WIDTH), BF16),
        grid=(n_seq, C_KV_HEADS, nq),
        in_specs=[
            pl.BlockSpec((t, gw), lambda b, kv, i: (b * nq + i, QC_OFF // gw + kv)),
            pl.BlockSpec((s, HEAD_DIM), lambda b, kv, i: (b, KC_OFF // HEAD_DIM + kv)),
            pl.BlockSpec((s, HEAD_DIM), lambda b, kv, i: (b, VC_OFF // HEAD_DIM + kv)),
        ],
        out_specs=pl.BlockSpec((t, gw), lambda b, kv, i: (b * nq + i, kv)),
        scratch_shapes=[pltpu.VMEM((nc, g * t, tk), F32), pltpu.VMEM((g * t, HEAD_DIM), F32)],
        compiler_params=_params(("parallel", "parallel", "arbitrary")),
        name="axial_attn",
    )(qkv, qkv, qkv)


def _window_body(sink_ref, q_ref, kp_ref, kc_ref, kn_ref, vp_ref, vc_ref, vn_ref, bias_ref, o_ref, *, nb):
    g = B_HEADS // B_KV_HEADS
    t = WINDOW
    kv = pl.program_id(1)
    i = pl.program_id(2)
    q = q_ref[...]
    qs = jnp.concatenate([q[:, h * HEAD_DIM:(h + 1) * HEAD_DIM] for h in range(g)], axis=0)
    kw = jnp.concatenate([kp_ref[...], kc_ref[...], kn_ref[...]], axis=0)
    vw = jnp.concatenate([vp_ref[...], vc_ref[...], vn_ref[...]], axis=0)
    sc = _dot_nt(qs, kw)
    col = lax.broadcasted_iota(jnp.int32, (1, 3 * t), 1)
    edge = jnp.where(((col < t) & (i == 0)) | ((col >= 2 * t) & (i == nb - 1)), NEG_INF, 0.0).astype(F32)
    outs = []
    for h in range(g):
        sink = sink_ref[kv * g + h]
        sh = sc[h * t:(h + 1) * t, :] + bias_ref[h] + edge
        mx = jnp.maximum(jnp.max(sh, axis=-1, keepdims=True), sink)
        e = jnp.exp(sh - mx)
        l = jnp.sum(e, axis=-1, keepdims=True) + jnp.exp(sink - mx)
        outs.append(_dot(e.astype(BF16), vw) / l)
    for h in range(g):
        o_ref[:, h * HEAD_DIM:(h + 1) * HEAD_DIM] = outs[h].astype(BF16)


def _window_attention(qkv, bias_b, sink_logits, layer, n_seq, s):
    m = qkv.shape[0]
    g = B_HEADS // B_KV_HEADS
    gw = g * HEAD_DIM
    t = WINDOW
    nb = s // t
    kcol = KB_OFF // HEAD_DIM
    vcol = VB_OFF // HEAD_DIM

    def prev(b, kv, i, sink, col):
        return (b * nb + jnp.maximum(i - 1, 0), col + kv)

    def cur(b, kv, i, sink, col):
        return (b * nb + i, col + kv)

    def nxt(b, kv, i, sink, col):
        return (b * nb + jnp.minimum(i + 1, nb - 1), col + kv)

    blk = (t, HEAD_DIM)
    grid_spec = pltpu.PrefetchScalarGridSpec(
        num_scalar_prefetch=1,
        grid=(n_seq, B_KV_HEADS, nb),
        in_specs=[
            pl.BlockSpec((t, gw), lambda b, kv, i, sink: (b * nb + i, QB_OFF // gw + kv)),
            pl.BlockSpec(blk, functools.partial(prev, col=kcol)),
            pl.BlockSpec(blk, functools.partial(cur, col=kcol)),
            pl.BlockSpec(blk, functools.partial(nxt, col=kcol)),
            pl.BlockSpec(blk, functools.partial(prev, col=vcol)),
            pl.BlockSpec(blk, functools.partial(cur, col=vcol)),
            pl.BlockSpec(blk, functools.partial(nxt, col=vcol)),
            pl.BlockSpec((g, t, 3 * t), lambda b, kv, i, sink: (kv, 0, 0)),
        ],
        out_specs=pl.BlockSpec((t, gw), lambda b, kv, i, sink: (b * nb + i, kv)),
    )
    return pl.pallas_call(
        functools.partial(_window_body, nb=nb),
        out_shape=jax.ShapeDtypeStruct((m, BRANCH_WIDTH), BF16),
        grid_spec=grid_spec,
        compiler_params=_params(("parallel", "parallel", "arbitrary")),
        name="window_attn",
    )(sink_logits[layer], qkv, qkv, qkv, qkv, qkv, qkv, qkv, bias_b)


def _merge_body(x_ref, ya_ref, yb_ref, yc_ref, wg_ref, wb_ref, wo_ref, g_ref, b_ref, o_ref, xb_ref, acc_ref, *, n_c):
    j = pl.program_id(1)

    @pl.when(j == 0)
    def _():
        xb_ref[...] = x_ref[...].astype(BF16)

    xb = xb_ref[...]
    merged = None
    for n, y_ref in enumerate((ya_ref, yb_ref, yc_ref)):
        gate = jax.nn.sigmoid(_dot(xb, wg_ref[n]))
        term = gate * _dot(y_ref[...], wb_ref[n])
        merged = term if merged is None else merged + term
    d = _dot(merged.astype(BF16), wo_ref[...])

    @pl.when(j == 0)
    def _():
        acc_ref[...] = d

    @pl.when(j > 0)
    def _():
        acc_ref[...] += d

    @pl.when(j == n_c - 1)
    def _():
        y = ALPHA * x_ref[...] + acc_ref[...]
        o_ref[...] = _layer_norm(y, g_ref[...], b_ref[...])


def _merge(x, ya, yb, yc, w_gate, w_branch, w_out, ln_g, ln_b, layer, *, tm=512, tc=256):
    m = x.shape[0]
    n_c = D_MODEL // tc
    yspec = pl.BlockSpec((tm, BRANCH_WIDTH), lambda i, j: (i, 0))
    return pl.pallas_call(
        functools.partial(_merge_body, n_c=n_c),
        out_shape=jax.ShapeDtypeStruct((m, D_MODEL), F32),
        grid=(m // tm, n_c),
        in_specs=[
            pl.BlockSpec((tm, D_MODEL), lambda i, j: (i, 0)),
            yspec, yspec, yspec,
            pl.BlockSpec((None, N_BRANCH, D_MODEL, tc), lambda i, j: (layer, 0, 0, j)),
            pl.BlockSpec((None, N_BRANCH, BRANCH_WIDTH, tc), lambda i, j: (layer, 0, 0, j)),
            pl.BlockSpec((None, tc, D_MODEL), lambda i, j: (layer, j, 0)),
            pl.BlockSpec((None, None, 1, D_MODEL), lambda i, j: (layer, 1, 0, 0)),
            pl.BlockSpec((None, None, 1, D_MODEL), lambda i, j: (layer, 1, 0, 0)),
        ],
        out_specs=pl.BlockSpec((tm, D_MODEL), lambda i, j: (i, 0)),
        scratch_shapes=[pltpu.VMEM((tm, D_MODEL), BF16), pltpu.VMEM((tm, D_MODEL), F32)],
        compiler_params=_params(("parallel", "arbitrary")),
        name="merge",
    )(x, ya, yb, yc, w_gate, w_branch, w_out, ln_g, ln_b)


def _memkv_body(m_ref, w_ref, o_ref):
    o_ref[...] = _dot(m_ref[...].astype(BF16), w_ref[...]).astype(BF16)


def _memkv(mem, w_kv, layer):
    rows = mem.shape[0]
    return pl.pallas_call(
        _memkv_body,
        out_shape=jax.ShapeDtypeStruct((rows, 2 * MEM_INNER), BF16),
        grid=(rows // MEM_TOKENS,),
        in_specs=[
            pl.BlockSpec((MEM_TOKENS, D_MODEL), lambda i: (i, 0)),
            pl.BlockSpec((None, D_MODEL, 2 * MEM_INNER), lambda i: (layer, 0, 0)),
        ],
        out_specs=pl.BlockSpec((MEM_TOKENS, 2 * MEM_INNER), lambda i: (i, 0)),
        compiler_params=_params(("parallel",)),
        name="memkv",
    )(mem, w_kv)


def _mem_body(x_ref, kv_ref, wq_ref, wo_ref, g_ref, b_ref, o_ref):
    x = x_ref[...]
    q = (_dot(x.astype(BF16), wq_ref[...]) * SCALE).astype(BF16)
    kv = kv_ref[...]
    heads = []
    for h in range(MEM_HEADS):
        sc = _dot_nt(q[:, h * HEAD_DIM:(h + 1) * HEAD_DIM], kv[:, h * HEAD_DIM:(h + 1) * HEAD_DIM])
        e = jnp.exp(sc - jnp.max(sc, axis=-1, keepdims=True))
        l = jnp.sum(e, axis=-1, keepdims=True)
        v = kv[:, MEM_INNER + h * HEAD_DIM:MEM_INNER + (h + 1) * HEAD_DIM]
        heads.append((_dot(e.astype(BF16), v) / l).astype(BF16))
    o = jnp.concatenate(heads, axis=1)
    y = ALPHA * x + _dot(o, wo_ref[...])
    o_ref[...] = _layer_norm(y, g_ref[...], b_ref[...])


def _mem_attention(x, kv, w_q, w_o, ln_g, ln_b, layer, s, *, tm=512):
    m = x.shape[0]
    per_seq = s // tm
    return pl.pallas_call(
        _mem_body,
        out_shape=jax.ShapeDtypeStruct((m, D_MODEL), F32),
        grid=(m // tm,),
        in_specs=[
            pl.BlockSpec((tm, D_MODEL), lambda i: (i, 0)),
            pl.BlockSpec((MEM_TOKENS, 2 * MEM_INNER), lambda i: (i // per_seq, 0)),
            pl.BlockSpec((None, D_MODEL, MEM_INNER), lambda i: (layer, 0, 0)),
            pl.BlockSpec((None, MEM_INNER, D_MODEL), lambda i: (layer, 0, 0)),
            pl.BlockSpec((None, None, 1, D_MODEL), lambda i: (layer, 2, 0, 0)),
            pl.BlockSpec((None, None, 1, D_MODEL), lambda i: (layer, 2, 0, 0)),
        ],
        out_specs=pl.BlockSpec((tm, D_MODEL), lambda i: (i, 0)),
        compiler_params=_params(("parallel",)),
        name="mem_attn",
    )(x, kv, w_q, w_o, ln_g, ln_b)


def _t5_bucket_np(rel):
    half = NUM_BUCKETS // 2
    max_exact = half // 2
    rel = np.asarray(rel, np.int32)
    ret = np.where(rel > 0, half, 0)
    n = np.abs(rel)
    ratio = np.log(np.maximum(n, 1).astype(np.float32) / np.float32(max_exact)) / np.float32(math.log(MAX_DISTANCE / max_exact))
    large = max_exact + (ratio * np.float32(half - max_exact)).astype(np.int32)
    large = np.minimum(large, half - 1)
    return (ret + np.where(n < max_exact, n, large)).astype(np.int32)


def _diff_bias_tiles(table_a, t):
    ii = np.arange(t)[:, None]
    jj = np.arange(t)[None, :]
    buckets = np.stack([_t5_bucket_np(d * t + jj - ii) for d in (-2, -1, 0, 1, 2)])
    return jnp.transpose(table_a[buckets], (3, 0, 1, 2))


def _window_bias_tiles(table_b):
    rel = (np.arange(3 * WINDOW) - WINDOW)[None, :] - np.arange(WINDOW)[:, None]
    bias = jnp.transpose(table_b[_t5_bucket_np(rel)], (2, 0, 1))
    return jnp.where(jnp.asarray(np.abs(rel) <= WINDOW)[None], bias, NEG_INF)


def _rope_tables(s):
    n_rows = s // GRID_W
    row = np.repeat(np.arange(n_rows), GRID_W).astype(np.float32)
    col = np.tile(np.arange(GRID_W), n_rows).astype(np.float32)
    axis_dims = HEAD_DIM // 2
    inv = ROPE_THETA ** (-jnp.arange(0, axis_dims, 2, dtype=F32) / axis_dims)
    ang = jnp.concatenate([row[:, None] * inv[None, :], col[:, None] * inv[None, :]], axis=-1)
    cos, sin = jnp.cos(ang), jnp.sin(ang)
    return jnp.concatenate([cos, cos], axis=-1), jnp.concatenate([-sin, sin], axis=-1)


def _run_trunk(x, mem, wts, bias_a, bias_b):
    n_seq, s, _ = x.shape
    x = x.reshape(n_seq * s, D_MODEL)
    mem = mem.reshape(n_seq * MEM_TOKENS, D_MODEL)
    cos2, sin2 = _rope_tables(s)
    for i in range(DEPTH):
        lambda_init = 0.8 - 0.6 * math.exp(-0.3 * i)
        x = _ffn(x, wts["w_ffn_gu"], wts["w_ffn_d"], wts["ln_g"], wts["ln_b"], i, 0, 0)
        qkv = _qkv(x, wts["w_qkv"], wts["qk_norm_g"], cos2, sin2, i)
        ya = _diff_attention(qkv, bias_a, wts["lambda_qk"], wts["subln_g"], i, n_seq, s, lambda_init)
        yb = _window_attention(qkv, bias_b, wts["sink_logits"], i, n_seq, s)
        yc = _axial_attention(qkv, n_seq, s)
        x = _merge(x, ya, yb, yc, wts["w_gate"], wts["w_branch"], wts["w_out"], wts["ln_g"], wts["ln_b"], i)
        kv = _memkv(mem, wts["w_mem_kv"], i)
        x = _mem_attention(x, kv, wts["w_mem_q"], wts["w_mem_o"], wts["ln_g"], wts["ln_b"], i, s)
        x = _ffn(x, wts["w_ffn_gu"], wts["w_ffn_d"], wts["ln_g"], wts["ln_b"], i, 1, 3)
    return x.reshape(n_seq, s, D_MODEL)


def kernel(x_prompt, x_sample, mem_prompt, mem_sample, rel_bias_table, w_in, w_branch, w_out, lambda_qk, subln_g, sink_logits, qk_norm_g, w_mem_q, w_mem_kv, w_mem_o, w_ffn_gu, w_ffn_d, ln_g, ln_b):
    wts = {
        "w_qkv": w_in[:, :, :QKV_WIDTH].astype(BF16),
        "w_gate": jnp.transpose(w_in[:, :, QKV_WIDTH:].reshape(DEPTH, D_MODEL, N_BRANCH, D_MODEL), (0, 2, 1, 3)).astype(BF16),
        "w_branch": w_branch.astype(BF16),
        "w_out": w_out.astype(BF16),
        "w_mem_q": w_mem_q.astype(BF16),
        "w_mem_kv": w_mem_kv.astype(BF16),
        "w_mem_o": w_mem_o.astype(BF16),
        "w_ffn_gu": w_ffn_gu.astype(BF16),
        "w_ffn_d": w_ffn_d.astype(BF16),
        "lambda_qk": lambda_qk,
        "subln_g": subln_g.reshape(DEPTH, 1, A_V),
        "sink_logits": sink_logits,
        "qk_norm_g": qk_norm_g,
        "ln_g": ln_g.reshape(DEPTH, 4, 1, D_MODEL),
        "ln_b": ln_b.reshape(DEPTH, 4, 1, D_MODEL),
    }
    bias_a = _diff_bias_tiles(rel_bias_table[:, :A_HEADS], 256)
    bias_b = _window_bias_tiles(rel_bias_table[:, A_HEADS:])
    y_prompt = _run_trunk(x_prompt, mem_prompt, wts, bias_a, bias_b)
    y_sample = _run_trunk(x_sample, mem_sample, wts, bias_a, bias_b)
    return (y_prompt, y_sample)
```

```python
import functools
import math

import numpy as np
import jax
import jax.numpy as jnp
from jax import lax
from jax.experimental import pallas as pl
from jax.experimental.pallas import tpu as pltpu

D_MODEL = 2048
DEPTH = 4
HEAD_DIM = 128
A_HEADS = 4
A_V = 2 * HEAD_DIM
B_HEADS = 8
B_KV_HEADS = 2
WINDOW = 128
C_HEADS = 8
C_KV_HEADS = 2
GRID_W = 64
ROPE_THETA = 10000.0
NUM_BUCKETS = 32
MAX_DISTANCE = 128
MEM_TOKENS = 256
MEM_HEADS = 4
MEM_INNER = MEM_HEADS * HEAD_DIM
D_FF = 5632
N_BRANCH = 3
BRANCH_WIDTH = 1024
QKV_WIDTH = 6144
ALPHA = (2 * DEPTH) ** 0.25
LN_EPS = 1e-5
RMS_EPS = 1e-6
NEG_INF = -1e30
LOG2E = math.log2(math.e)
QSCALE = HEAD_DIM ** -0.5 * LOG2E

QA_OFF, KA_OFF, VA_OFF = 0, 1024, 2048
QB_OFF, KB_OFF, VB_OFF = 3072, 4096, 4352
QC_OFF, KC_OFF, VC_OFF = 4608, 5632, 5888

V7X_VMEM_BYTES = 64 * 1024 * 1024
VMEM_LIMIT = 56 * 1024 * 1024

F32 = jnp.float32
BF16 = jnp.bfloat16


def _params(semantics):
    return pltpu.CompilerParams(dimension_semantics=semantics, vmem_limit_bytes=VMEM_LIMIT)


def _layer_norm(y, g, b):
    mu = jnp.mean(y, axis=-1, keepdims=True)
    d = y - mu
    var = jnp.mean(d * d, axis=-1, keepdims=True)
    return d * lax.rsqrt(var + LN_EPS) * g + b


def _dot(a, b):
    return jnp.dot(a, b, preferred_element_type=F32)


def _dot_nt(a, b):
    return lax.dot_general(a, b, (((1,), (1,)), ((), ())), preferred_element_type=F32)


def _lane_fold(x, op):
    parts = [x[:, j * 128:(j + 1) * 128] for j in range(x.shape[1] // 128)]
    return functools.reduce(op, parts)


def _exp2_minus(x, col):
    parts = [jnp.exp2(x[:, j * 128:(j + 1) * 128] - col) for j in range(x.shape[1] // 128)]
    return parts[0] if len(parts) == 1 else jnp.concatenate(parts, axis=1)


def _lane_bcast(col):
    return jnp.broadcast_to(col, (col.shape[0], 128))


def _ffn_body(x_ref, wg_ref, wu_ref, wd_ref, g_ref, b_ref, o_ref, xb_ref, acc_ref, *, n_f):
    f = pl.program_id(1)

    @pl.when(f == 0)
    def _():
        xb_ref[...] = x_ref[...].astype(BF16)
        acc_ref[...] = jnp.zeros_like(acc_ref)

    xb = xb_ref[...]
    h = _dot(xb, wg_ref[...])
    u = _dot(xb, wu_ref[...])
    a = (h * jax.nn.sigmoid(h) * u).astype(BF16)
    acc_ref[...] += _dot(a, wd_ref[...])

    @pl.when(f == n_f - 1)
    def _():
        y = ALPHA * x_ref[...] + 0.5 * acc_ref[...]
        o_ref[...] = _layer_norm(y, g_ref[...], b_ref[...])


def _ffn(x, w_gu, w_d, ln_g, ln_b, layer, which, ln_idx, *, tm=512, tf=512):
    m = x.shape[0]
    n_f = D_FF // tf
    return pl.pallas_call(
        functools.partial(_ffn_body, n_f=n_f),
        out_shape=jax.ShapeDtypeStruct((m, D_MODEL), F32),
        grid=(m // tm, n_f),
        in_specs=[
            pl.BlockSpec((tm, D_MODEL), lambda i, f: (i, 0)),
            pl.BlockSpec((None, None, D_MODEL, tf), lambda i, f: (layer, which, 0, f)),
            pl.BlockSpec((None, None, D_MODEL, tf), lambda i, f: (layer, which, 0, f + n_f)),
            pl.BlockSpec((None, None, tf, D_MODEL), lambda i, f: (layer, which, f, 0)),
            pl.BlockSpec((None, None, 1, D_MODEL), lambda i, f: (layer, ln_idx, 0, 0)),
            pl.BlockSpec((None, None, 1, D_MODEL), lambda i, f: (layer, ln_idx, 0, 0)),
        ],
        out_specs=pl.BlockSpec((tm, D_MODEL), lambda i, f: (i, 0)),
        scratch_shapes=[pltpu.VMEM((tm, D_MODEL), BF16), pltpu.VMEM((tm, D_MODEL), F32)],
        compiler_params=_params(("parallel", "arbitrary")),
        name="ffn",
    )(x, w_gu, w_gu, w_d, ln_g, ln_b)


def _qkv_body(x_ref, w_ref, gn_ref, cos_ref, sin_ref, o_ref, *, tn):
    xb = x_ref[...].astype(BF16)
    cos = cos_ref[...]
    sin = sin_ref[...]
    for j in range(QKV_WIDTH // tn):
        c0 = j * tn
        acc = _dot(xb, w_ref[:, c0:c0 + tn])
        is_q = (QA_OFF <= c0 < KA_OFF) or (QB_OFF <= c0 < KB_OFF) or (QC_OFF <= c0 < KC_OFF)
        if QC_OFF <= c0 < VC_OFF:
            g = gn_ref[0:1, :] if c0 < KC_OFF else gn_ref[1:2, :]
            for hh in range(tn // HEAD_DIM):
                a = acc[:, hh * HEAD_DIM:(hh + 1) * HEAD_DIM]
                r = lax.rsqrt(jnp.mean(a * a, axis=-1, keepdims=True) + RMS_EPS)
                y = a * r * g
                y = y * cos + pltpu.roll(y, HEAD_DIM // 2, 1) * sin
                if is_q:
                    y = y * QSCALE
                o_ref[:, c0 + hh * HEAD_DIM:c0 + (hh + 1) * HEAD_DIM] = y.astype(BF16)
        else:
            if is_q:
                acc = acc * QSCALE
            o_ref[:, c0:c0 + tn] = acc.astype(BF16)


def _qkv(x, w_qkv, qk_norm_g, cos2, sin2, layer, *, tm=512, tn=256):
    m = x.shape[0]
    s = cos2.shape[0]
    n_pos = s // tm
    return pl.pallas_call(
        functools.partial(_qkv_body, tn=tn),
        out_shape=jax.ShapeDtypeStruct((m, QKV_WIDTH), BF16),
        grid=(m // tm,),
        in_specs=[
            pl.BlockSpec((tm, D_MODEL), lambda i: (i, 0)),
            pl.BlockSpec((None, D_MODEL, QKV_WIDTH), lambda i: (layer, 0, 0), pipeline_mode=pl.Buffered(1)),
            pl.BlockSpec((None, 2, HEAD_DIM), lambda i: (layer, 0, 0)),
            pl.BlockSpec((tm, HEAD_DIM), lambda i: (i % n_pos, 0)),
            pl.BlockSpec((tm, HEAD_DIM), lambda i: (i % n_pos, 0)),
        ],
        out_specs=pl.BlockSpec((tm, QKV_WIDTH), lambda i: (i, 0)),
        compiler_params=_params(("parallel",)),
        name="qkv",
    )(x, w_qkv, qk_norm_g, cos2, sin2)


def _diff_body(q_ref, k_ref, v_ref, bias_ref, lam_ref, sg_ref, o_ref, s1_ref, s2_ref, *, t, nc, lambda_init):
    i = pl.program_id(2)
    q = q_ref[...]
    q1 = q[:, :HEAD_DIM]
    q2 = q[:, HEAD_DIM:]

    m1 = m2 = None
    for c in range(nc):
        kc = k_ref[c * t:(c + 1) * t, :]
        b = bias_ref[jnp.clip(c - i, -2, 2) + 2]
        s1 = _dot_nt(q1, kc[:, :HEAD_DIM]) + b
        s2 = _dot_nt(q2, kc[:, HEAD_DIM:]) + b
        s1_ref[c] = s1
        s2_ref[c] = s2
        f1 = _lane_fold(s1, jnp.maximum)
        f2 = _lane_fold(s2, jnp.maximum)
        m1 = f1 if m1 is None else jnp.maximum(m1, f1)
        m2 = f2 if m2 is None else jnp.maximum(m2, f2)
    m1 = _lane_bcast(jnp.max(m1, axis=-1, keepdims=True))
    m2 = _lane_bcast(jnp.max(m2, axis=-1, keepdims=True))

    acc = l1 = l2 = None
    for c in range(nc):
        e1 = _exp2_minus(s1_ref[c], m1)
        e2 = _exp2_minus(s2_ref[c], m2)
        e = jnp.concatenate([e1.astype(BF16), e2.astype(BF16)], axis=0)
        d = _dot(e, v_ref[c * t:(c + 1) * t, :])
        f1 = _lane_fold(e1, jnp.add)
        f2 = _lane_fold(e2, jnp.add)
        acc = d if acc is None else acc + d
        l1 = f1 if l1 is None else l1 + f1
        l2 = f2 if l2 is None else l2 + f2
    l1 = jnp.sum(l1, axis=-1, keepdims=True)
    l2 = jnp.sum(l2, axis=-1, keepdims=True)

    lam = lam_ref[...]
    lam_full = (jnp.exp(jnp.sum(lam[0:1] * lam[1:2], axis=-1, keepdims=True))
                - jnp.exp(jnp.sum(lam[2:3] * lam[3:4], axis=-1, keepdims=True)) + lambda_init)
    o = acc[0:t, :] / l1 - lam_full * (acc[t:2 * t, :] / l2)
    o = o * lax.rsqrt(jnp.mean(o * o, axis=-1, keepdims=True) + RMS_EPS) * sg_ref[...]
    o_ref[...] = (o * (1.0 - lambda_init)).astype(BF16)


def _diff_attention(qkv, bias_a, lambda_qk, subln_g, layer, n_seq, s, lambda_init, *, t=256):
    m = qkv.shape[0]
    nc = s // t
    return pl.pallas_call(
        functools.partial(_diff_body, t=t, nc=nc, lambda_init=lambda_init),
        out_shape=jax.ShapeDtypeStruct((m, BRANCH_WIDTH), BF16),
        grid=(n_seq, A_HEADS, nc),
        in_specs=[
            pl.BlockSpec((t, A_V), lambda b, h, i: (b * nc + i, QA_OFF // A_V + h)),
            pl.BlockSpec((s, A_V), lambda b, h, i: (b, KA_OFF // A_V + h)),
            pl.BlockSpec((s, A_V), lambda b, h, i: (b, VA_OFF // A_V + h)),
            pl.BlockSpec((None, 5, t, t), lambda b, h, i: (h, 0, 0, 0)),
            pl.BlockSpec((None, 4, HEAD_DIM), lambda b, h, i: (layer, 0, 0)),
            pl.BlockSpec((None, 1, A_V), lambda b, h, i: (layer, 0, 0)),
        ],
        out_specs=pl.BlockSpec((t, A_V), lambda b, h, i: (b * nc + i, h)),
        scratch_shapes=[pltpu.VMEM((nc, t, t), F32), pltpu.VMEM((nc, t, t), F32)],
        compiler_params=_params(("parallel", "parallel", "arbitrary")),
        name="diff_attn",
    )(qkv, qkv, qkv, bias_a, lambda_qk, subln_g)


def _axial_body(q_ref, k_ref, v_ref, o_ref, s_ref, *, t, tk, nc):
    g = C_HEADS // C_KV_HEADS
    q = q_ref[...]
    half = g // 2
    qs = [jnp.concatenate([q[:, h * HEAD_DIM:(h + 1) * HEAD_DIM] for h in range(r * half, (r + 1) * half)], axis=0)
          for r in range(2)]

    m = [None, None]
    for c in range(nc):
        kc = k_ref[c * tk:(c + 1) * tk, :]
        for r in range(2):
            sc = _dot_nt(qs[r], kc)
            s_ref[r, c] = sc
            f = _lane_fold(sc, jnp.maximum)
            m[r] = f if m[r] is None else jnp.maximum(m[r], f)
    m = [_lane_bcast(jnp.max(mr, axis=-1, keepdims=True)) for mr in m]

    acc = [None, None]
    l = [None, None]
    for c in range(nc):
        vc = v_ref[c * tk:(c + 1) * tk, :]
        for r in range(2):
            e = _exp2_minus(s_ref[r, c], m[r])
            d = _dot(e.astype(BF16), vc)
            f = _lane_fold(e, jnp.add)
            acc[r] = d if acc[r] is None else acc[r] + d
            l[r] = f if l[r] is None else l[r] + f
    for r in range(2):
        o = acc[r] / jnp.sum(l[r], axis=-1, keepdims=True)
        for hh in range(half):
            h = r * half + hh
            o_ref[:, h * HEAD_DIM:(h + 1) * HEAD_DIM] = o[hh * t:(hh + 1) * t, :].astype(BF16)


def _axial_attention(qkv, n_seq, s, *, t=256, tk=256):
    m = qkv.shape[0]
    g = C_HEADS // C_KV_HEADS
    gw = g * HEAD_DIM
    nq = s // t
    nc = s // tk
    return pl.pallas_call(
        functools.partial(_axial_body, t=t, tk=tk, nc=nc),
        out_shape=jax.ShapeDtypeStruct((m, BRANCH_WIDTH), BF16),
        grid=(n_seq, C_KV_HEADS, nq),
        in_specs=[
            pl.BlockSpec((t, gw), lambda b, kv, i: (b * nq + i, QC_OFF // gw + kv)),
            pl.BlockSpec((s, HEAD_DIM), lambda b, kv, i: (b, KC_OFF // HEAD_DIM + kv)),
            pl.BlockSpec((s, HEAD_DIM), lambda b, kv, i: (b, VC_OFF // HEAD_DIM + kv)),
        ],
        out_specs=pl.BlockSpec((t, gw), lambda b, kv, i: (b * nq + i, kv)),
        scratch_shapes=[pltpu.VMEM((2, nc, g // 2 * t, tk), F32)],
        compiler_params=_params(("parallel", "parallel", "arbitrary")),
        name="axial_attn",
    )(qkv, qkv, qkv)


def _window_body(sink_ref, q_ref, kp_ref, kc_ref, kn_ref, vp_ref, vc_ref, vn_ref, bias_ref, o_ref, *, nb):
    g = B_HEADS // B_KV_HEADS
    t = WINDOW
    kv = pl.program_id(1)
    i = pl.program_id(2)
    q = q_ref[...]
    qs = jnp.concatenate([q[:, h * HEAD_DIM:(h + 1) * HEAD_DIM] for h in range(g)], axis=0)
    kw = jnp.concatenate([kp_ref[...], kc_ref[...], kn_ref[...]], axis=0)
    vw = jnp.concatenate([vp_ref[...], vc_ref[...], vn_ref[...]], axis=0)
    sc = _dot_nt(qs, kw)
    col = lax.broadcasted_iota(jnp.int32, (1, 3 * t), 1)
    edge = jnp.where(((col < t) & (i == 0)) | ((col >= 2 * t) & (i == nb - 1)), NEG_INF, 0.0).astype(F32)
    outs = []
    for h in range(g):
        sink = sink_ref[kv * g + h] * LOG2E
        sh = sc[h * t:(h + 1) * t, :] + bias_ref[h] + edge
        mx = jnp.maximum(jnp.max(sh, axis=-1, keepdims=True), sink)
        e = jnp.exp2(sh - mx)
        l = jnp.sum(e, axis=-1, keepdims=True) + jnp.exp2(sink - mx)
        outs.append(_dot(e.astype(BF16), vw) / l)
    for h in range(g):
        o_ref[:, h * HEAD_DIM:(h + 1) * HEAD_DIM] = outs[h].astype(BF16)


def _window_attention(qkv, bias_b, sink_logits, layer, n_seq, s):
    m = qkv.shape[0]
    g = B_HEADS // B_KV_HEADS
    gw = g * HEAD_DIM
    t = WINDOW
    nb = s // t
    kcol = KB_OFF // HEAD_DIM
    vcol = VB_OFF // HEAD_DIM

    def prev(b, kv, i, sink, col):
        return (b * nb + jnp.maximum(i - 1, 0), col + kv)

    def cur(b, kv, i, sink, col):
        return (b * nb + i, col + kv)

    def nxt(b, kv, i, sink, col):
        return (b * nb + jnp.minimum(i + 1, nb - 1), col + kv)

    blk = (t, HEAD_DIM)
    grid_spec = pltpu.PrefetchScalarGridSpec(
        num_scalar_prefetch=1,
        grid=(n_seq, B_KV_HEADS, nb),
        in_specs=[
            pl.BlockSpec((t, gw), lambda b, kv, i, sink: (b * nb + i, QB_OFF // gw + kv)),
            pl.BlockSpec(blk, functools.partial(prev, col=kcol)),
            pl.BlockSpec(blk, functools.partial(cur, col=kcol)),
            pl.BlockSpec(blk, functools.partial(nxt, col=kcol)),
            pl.BlockSpec(blk, functools.partial(prev, col=vcol)),
            pl.BlockSpec(blk, functools.partial(cur, col=vcol)),
            pl.BlockSpec(blk, functools.partial(nxt, col=vcol)),
            pl.BlockSpec((g, t, 3 * t), lambda b, kv, i, sink: (kv, 0, 0)),
        ],
        out_specs=pl.BlockSpec((t, gw), lambda b, kv, i, sink: (b * nb + i, kv)),
    )
    return pl.pallas_call(
        functools.partial(_window_body, nb=nb),
        out_shape=jax.ShapeDtypeStruct((m, BRANCH_WIDTH), BF16),
        grid_spec=grid_spec,
        compiler_params=_params(("parallel", "parallel", "arbitrary")),
        name="window_attn",
    )(sink_logits[layer], qkv, qkv, qkv, qkv, qkv, qkv, qkv, bias_b)


def _merge_body(x_ref, ya_ref, yb_ref, yc_ref, wg_ref, wb_ref, wo_ref, g_ref, b_ref, o_ref, xb_ref, acc_ref, *, n_c):
    j = pl.program_id(1)

    @pl.when(j == 0)
    def _():
        xb_ref[...] = x_ref[...].astype(BF16)
        acc_ref[...] = jnp.zeros_like(acc_ref)

    xb = xb_ref[...]
    merged = None
    for n, y_ref in enumerate((ya_ref, yb_ref, yc_ref)):
        gate = jax.nn.sigmoid(_dot(xb, wg_ref[n]))
        term = gate * _dot(y_ref[...], wb_ref[n])
        merged = term if merged is None else merged + term
    acc_ref[...] += _dot(merged.astype(BF16), wo_ref[...])

    @pl.when(j == n_c - 1)
    def _():
        y = ALPHA * x_ref[...] + acc_ref[...]
        o_ref[...] = _layer_norm(y, g_ref[...], b_ref[...])


def _merge(x, ya, yb, yc, w_gate, w_branch, w_out, ln_g, ln_b, layer, *, tm=512, tc=256):
    m = x.shape[0]
    n_c = D_MODEL // tc
    yspec = pl.BlockSpec((tm, BRANCH_WIDTH), lambda i, j: (i, 0))
    return pl.pallas_call(
        functools.partial(_merge_body, n_c=n_c),
        out_shape=jax.ShapeDtypeStruct((m, D_MODEL), F32),
        grid=(m // tm, n_c),
        in_specs=[
            pl.BlockSpec((tm, D_MODEL), lambda i, j: (i, 0)),
            yspec, yspec, yspec,
            pl.BlockSpec((None, N_BRANCH, D_MODEL, tc), lambda i, j: (layer, 0, 0, j)),
            pl.BlockSpec((None, N_BRANCH, BRANCH_WIDTH, tc), lambda i, j: (layer, 0, 0, j)),
            pl.BlockSpec((None, tc, D_MODEL), lambda i, j: (layer, j, 0)),
            pl.BlockSpec((None, None, 1, D_MODEL), lambda i, j: (layer, 1, 0, 0)),
            pl.BlockSpec((None, None, 1, D_MODEL), lambda i, j: (layer, 1, 0, 0)),
        ],
        out_specs=pl.BlockSpec((tm, D_MODEL), lambda i, j: (i, 0)),
        scratch_shapes=[pltpu.VMEM((tm, D_MODEL), BF16), pltpu.VMEM((tm, D_MODEL), F32)],
        compiler_params=_params(("parallel", "arbitrary")),
        name="merge",
    )(x, ya, yb, yc, w_gate, w_branch, w_out, ln_g, ln_b)


def _memkv_body(m_ref, w_ref, o_ref):
    o_ref[...] = _dot(m_ref[...].astype(BF16), w_ref[...]).astype(BF16)


def _memkv(mem, w_kv, layer):
    rows = mem.shape[0]
    return pl.pallas_call(
        _memkv_body,
        out_shape=jax.ShapeDtypeStruct((rows, 2 * MEM_INNER), BF16),
        grid=(rows // MEM_TOKENS,),
        in_specs=[
            pl.BlockSpec((MEM_TOKENS, D_MODEL), lambda i: (i, 0)),
            pl.BlockSpec((None, D_MODEL, 2 * MEM_INNER), lambda i: (layer, 0, 0)),
        ],
        out_specs=pl.BlockSpec((MEM_TOKENS, 2 * MEM_INNER), lambda i: (i, 0)),
        compiler_params=_params(("parallel",)),
        name="memkv",
    )(mem, w_kv)


def _mem_body(x_ref, kv_ref, wq_ref, wo_ref, g_ref, b_ref, o_ref):
    x = x_ref[...]
    q = (_dot(x.astype(BF16), wq_ref[...]) * QSCALE).astype(BF16)
    kv = kv_ref[...]
    heads = []
    for h in range(MEM_HEADS):
        sc = _dot_nt(q[:, h * HEAD_DIM:(h + 1) * HEAD_DIM], kv[:, h * HEAD_DIM:(h + 1) * HEAD_DIM])
        e = jnp.exp2(sc - jnp.max(sc, axis=-1, keepdims=True))
        l = jnp.sum(e, axis=-1, keepdims=True)
        v = kv[:, MEM_INNER + h * HEAD_DIM:MEM_INNER + (h + 1) * HEAD_DIM]
        heads.append((_dot(e.astype(BF16), v) / l).astype(BF16))
    o = jnp.concatenate(heads, axis=1)
    y = ALPHA * x + _dot(o, wo_ref[...])
    o_ref[...] = _layer_norm(y, g_ref[...], b_ref[...])


def _mem_attention(x, kv, w_q, w_o, ln_g, ln_b, layer, s, *, tm=512):
    m = x.shape[0]
    per_seq = s // tm
    return pl.pallas_call(
        _mem_body,
        out_shape=jax.ShapeDtypeStruct((m, D_MODEL), F32),
        grid=(m // tm,),
        in_specs=[
            pl.BlockSpec((tm, D_MODEL), lambda i: (i, 0)),
            pl.BlockSpec((MEM_TOKENS, 2 * MEM_INNER), lambda i: (i // per_seq, 0)),
            pl.BlockSpec((None, D_MODEL, MEM_INNER), lambda i: (layer, 0, 0)),
            pl.BlockSpec((None, MEM_INNER, D_MODEL), lambda i: (layer, 0, 0)),
            pl.BlockSpec((None, None, 1, D_MODEL), lambda i: (layer, 2, 0, 0)),
            pl.BlockSpec((None, None, 1, D_MODEL), lambda i: (layer, 2, 0, 0)),
        ],
        out_specs=pl.BlockSpec((tm, D_MODEL), lambda i: (i, 0)),
        compiler_params=_params(("parallel",)),
        name="mem_attn",
    )(x, kv, w_q, w_o, ln_g, ln_b)


def _t5_bucket_np(rel):
    half = NUM_BUCKETS // 2
    max_exact = half // 2
    rel = np.asarray(rel, np.int32)
    ret = np.where(rel > 0, half, 0)
    n = np.abs(rel)
    ratio = np.log(np.maximum(n, 1).astype(np.float32) / np.float32(max_exact)) / np.float32(math.log(MAX_DISTANCE / max_exact))
    large = max_exact + (ratio * np.float32(half - max_exact)).astype(np.int32)
    large = np.minimum(large, half - 1)
    return (ret + np.where(n < max_exact, n, large)).astype(np.int32)


def _lookup_bias(table, buckets):
    n_heads = table.shape[1]
    b = jnp.asarray(buckets)[None]
    tab = table * LOG2E
    out = jnp.zeros((n_heads,) + buckets.shape, F32)
    for k in range(NUM_BUCKETS):
        out = jnp.where(b == k, tab[k].reshape((n_heads,) + (1,) * buckets.ndim), out)
    return out


def _diff_bias_tiles(table_a, t):
    ii = np.arange(t)[:, None]
    jj = np.arange(t)[None, :]
    buckets = np.stack([_t5_bucket_np(d * t + jj - ii) for d in (-2, -1, 0, 1, 2)])
    return _lookup_bias(table_a, buckets)


def _window_bias_tiles(table_b):
    rel = (np.arange(3 * WINDOW) - WINDOW)[None, :] - np.arange(WINDOW)[:, None]
    bias = _lookup_bias(table_b, _t5_bucket_np(rel))
    return jnp.where(jnp.asarray(np.abs(rel) <= WINDOW)[None], bias, NEG_INF)


def _rope_tables(s):
    n_rows = s // GRID_W
    row = np.repeat(np.arange(n_rows), GRID_W).astype(np.float32)
    col = np.tile(np.arange(GRID_W), n_rows).astype(np.float32)
    axis_dims = HEAD_DIM // 2
    inv = ROPE_THETA ** (-jnp.arange(0, axis_dims, 2, dtype=F32) / axis_dims)
    ang = jnp.concatenate([row[:, None] * inv[None, :], col[:, None] * inv[None, :]], axis=-1)
    cos, sin = jnp.cos(ang), jnp.sin(ang)
    return jnp.concatenate([cos, cos], axis=-1), jnp.concatenate([-sin, sin], axis=-1)


def _run_trunk(x, mem, wts, bias_a, bias_b):
    n_seq, s, _ = x.shape
    x = x.reshape(n_seq * s, D_MODEL)
    mem = mem.reshape(n_seq * MEM_TOKENS, D_MODEL)
    cos2, sin2 = _rope_tables(s)
    for i in range(DEPTH):
        lambda_init = 0.8 - 0.6 * math.exp(-0.3 * i)
        x = _ffn(x, wts["w_ffn_gu"], wts["w_ffn_d"], wts["ln_g"], wts["ln_b"], i, 0, 0)
        qkv = _qkv(x, wts["w_qkv"], wts["qk_norm_g"], cos2, sin2, i)
        ya = _diff_attention(qkv, bias_a, wts["lambda_qk"], wts["subln_g"], i, n_seq, s, lambda_init)
        yb = _window_attention(qkv, bias_b, wts["sink_logits"], i, n_seq, s)
        yc = _axial_attention(qkv, n_seq, s)
        x = _merge(x, ya, yb, yc, wts["w_gate"], wts["w_branch"], wts["w_out"], wts["ln_g"], wts["ln_b"], i)
        kv = _memkv(mem, wts["w_mem_kv"], i)
        x = _mem_attention(x, kv, wts["w_mem_q"], wts["w_mem_o"], wts["ln_g"], wts["ln_b"], i, s)
        x = _ffn(x, wts["w_ffn_gu"], wts["w_ffn_d"], wts["ln_g"], wts["ln_b"], i, 1, 3)
    return x.reshape(n_seq, s, D_MODEL)


def kernel(x_prompt, x_sample, mem_prompt, mem_sample, rel_bias_table, w_in, w_branch, w_out, lambda_qk, subln_g, sink_logits, qk_norm_g, w_mem_q, w_mem_kv, w_mem_o, w_ffn_gu, w_ffn_d, ln_g, ln_b):
    wts = {
        "w_qkv": w_in[:, :, :QKV_WIDTH].astype(BF16),
        "w_gate": jnp.transpose(w_in[:, :, QKV_WIDTH:].reshape(DEPTH, D_MODEL, N_BRANCH, D_MODEL), (0, 2, 1, 3)).astype(BF16),
        "w_branch": w_branch.astype(BF16),
        "w_out": w_out.astype(BF16),
        "w_mem_q": w_mem_q.astype(BF16),
        "w_mem_kv": w_mem_kv.astype(BF16),
        "w_mem_o": w_mem_o.astype(BF16),
        "w_ffn_gu": w_ffn_gu.astype(BF16),
        "w_ffn_d": w_ffn_d.astype(BF16),
        "lambda_qk": lambda_qk,
        "subln_g": subln_g.reshape(DEPTH, 1, A_V),
        "sink_logits": sink_logits,
        "qk_norm_g": qk_norm_g,
        "ln_g": ln_g.reshape(DEPTH, 4, 1, D_MODEL),
        "ln_b": ln_b.reshape(DEPTH, 4, 1, D_MODEL),
    }
    bias_a = _diff_bias_tiles(rel_bias_table[:, :A_HEADS], 256)
    bias_b = _window_bias_tiles(rel_bias_table[:, A_HEADS:])
    y_prompt = _run_trunk(x_prompt, mem_prompt, wts, bias_a, bias_b)
    y_sample = _run_trunk(x_sample, mem_sample, wts, bias_a, bias_b)
    return (y_prompt, y_sample)
```

```python
import functools
import math

import numpy as np
import jax
import jax.numpy as jnp
from jax import lax
from jax.experimental import pallas as pl
from jax.experimental.pallas import tpu as pltpu

D_MODEL = 2048
DEPTH = 4
HEAD_DIM = 128
A_HEADS = 4
A_V = 2 * HEAD_DIM
B_HEADS = 8
B_KV_HEADS = 2
WINDOW = 128
C_HEADS = 8
C_KV_HEADS = 2
GRID_W = 64
ROPE_THETA = 10000.0
NUM_BUCKETS = 32
MAX_DISTANCE = 128
MEM_TOKENS = 256
MEM_HEADS = 4
MEM_INNER = MEM_HEADS * HEAD_DIM
D_FF = 5632
N_BRANCH = 3
BRANCH_WIDTH = 1024
QKV_WIDTH = 6144
ALPHA = (2 * DEPTH) ** 0.25
LN_EPS = 1e-5
RMS_EPS = 1e-6
NEG_INF = -1e30
LOG2E = math.log2(math.e)
QSCALE = HEAD_DIM ** -0.5 * LOG2E

QA_OFF, KA_OFF, VA_OFF = 0, 1024, 2048
QB_OFF, KB_OFF, VB_OFF = 3072, 4096, 4352
QC_OFF, KC_OFF, VC_OFF = 4608, 5632, 5888

V7X_VMEM_BYTES = 64 * 1024 * 1024
VMEM_LIMIT = 56 * 1024 * 1024

F32 = jnp.float32
BF16 = jnp.bfloat16


def _params(semantics):
    return pltpu.CompilerParams(dimension_semantics=semantics, vmem_limit_bytes=VMEM_LIMIT)


def _layer_norm(y, g, b):
    mu = jnp.mean(y, axis=-1, keepdims=True)
    d = y - mu
    var = jnp.mean(d * d, axis=-1, keepdims=True)
    return d * lax.rsqrt(var + LN_EPS) * g + b


def _dot(a, b):
    return jnp.dot(a, b, preferred_element_type=F32)


def _dot_nt(a, b):
    return lax.dot_general(a, b, (((1,), (1,)), ((), ())), preferred_element_type=F32)


def _lane_fold(x, op):
    parts = [x[:, j * 128:(j + 1) * 128] for j in range(x.shape[1] // 128)]
    return functools.reduce(op, parts)


def _exp2_minus(x, col):
    parts = [jnp.exp2(x[:, j * 128:(j + 1) * 128] - col) for j in range(x.shape[1] // 128)]
    return parts[0] if len(parts) == 1 else jnp.concatenate(parts, axis=1)


def _lane_bcast(col):
    return jnp.broadcast_to(col, (col.shape[0], 128))


def _ffn_body(x_ref, wg_ref, wu_ref, wd_ref, g_ref, b_ref, o_ref, xb_ref, acc_ref, *, n_f):
    f = pl.program_id(1)

    @pl.when(f == 0)
    def _():
        xb_ref[...] = x_ref[...].astype(BF16)
        acc_ref[...] = jnp.zeros_like(acc_ref)

    xb = xb_ref[...]
    h = _dot(xb, wg_ref[...])
    u = _dot(xb, wu_ref[...])
    a = (h * jax.nn.sigmoid(h) * u).astype(BF16)
    acc_ref[...] += _dot(a, wd_ref[...])

    @pl.when(f == n_f - 1)
    def _():
        y = ALPHA * x_ref[...] + 0.5 * acc_ref[...]
        o_ref[...] = _layer_norm(y, g_ref[...], b_ref[...])


def _ffn(x, w_gu, w_d, ln_g, ln_b, layer, which, ln_idx, *, tm=512, tf=512):
    m = x.shape[0]
    n_f = D_FF // tf
    return pl.pallas_call(
        functools.partial(_ffn_body, n_f=n_f),
        out_shape=jax.ShapeDtypeStruct((m, D_MODEL), F32),
        grid=(m // tm, n_f),
        in_specs=[
            pl.BlockSpec((tm, D_MODEL), lambda i, f: (i, 0)),
            pl.BlockSpec((None, None, D_MODEL, tf), lambda i, f: (layer, which, 0, f)),
            pl.BlockSpec((None, None, D_MODEL, tf), lambda i, f: (layer, which, 0, f + n_f)),
            pl.BlockSpec((None, None, tf, D_MODEL), lambda i, f: (layer, which, f, 0)),
            pl.BlockSpec((None, None, 1, D_MODEL), lambda i, f: (layer, ln_idx, 0, 0)),
            pl.BlockSpec((None, None, 1, D_MODEL), lambda i, f: (layer, ln_idx, 0, 0)),
        ],
        out_specs=pl.BlockSpec((tm, D_MODEL), lambda i, f: (i, 0)),
        scratch_shapes=[pltpu.VMEM((tm, D_MODEL), BF16), pltpu.VMEM((tm, D_MODEL), F32)],
        compiler_params=_params(("parallel", "arbitrary")),
        name="ffn",
    )(x, w_gu, w_gu, w_d, ln_g, ln_b)


def _qkv_body(x_ref, w_ref, gn_ref, cos_ref, sin_ref, o_ref, *, tn):
    xb = x_ref[...].astype(BF16)
    cos = cos_ref[...]
    sin = sin_ref[...]
    for j in range(QKV_WIDTH // tn):
        c0 = j * tn
        acc = _dot(xb, w_ref[:, c0:c0 + tn])
        is_q = (QA_OFF <= c0 < KA_OFF) or (QB_OFF <= c0 < KB_OFF) or (QC_OFF <= c0 < KC_OFF)
        if QC_OFF <= c0 < VC_OFF:
            g = gn_ref[0:1, :] if c0 < KC_OFF else gn_ref[1:2, :]
            for hh in range(tn // HEAD_DIM):
                a = acc[:, hh * HEAD_DIM:(hh + 1) * HEAD_DIM]
                r = lax.rsqrt(jnp.mean(a * a, axis=-1, keepdims=True) + RMS_EPS)
                y = a * r * g
                y = y * cos + pltpu.roll(y, HEAD_DIM // 2, 1) * sin
                if is_q:
                    y = y * QSCALE
                o_ref[:, c0 + hh * HEAD_DIM:c0 + (hh + 1) * HEAD_DIM] = y.astype(BF16)
        else:
            if is_q:
                acc = acc * QSCALE
            o_ref[:, c0:c0 + tn] = acc.astype(BF16)


def _qkv(x, w_qkv, qk_norm_g, cos2, sin2, layer, *, tm=512, tn=256):
    m = x.shape[0]
    s = cos2.shape[0]
    n_pos = s // tm
    return pl.pallas_call(
        functools.partial(_qkv_body, tn=tn),
        out_shape=jax.ShapeDtypeStruct((m, QKV_WIDTH), BF16),
        grid=(m // tm,),
        in_specs=[
            pl.BlockSpec((tm, D_MODEL), lambda i: (i, 0)),
            pl.BlockSpec((None, D_MODEL, QKV_WIDTH), lambda i: (layer, 0, 0), pipeline_mode=pl.Buffered(1)),
            pl.BlockSpec((None, 2, HEAD_DIM), lambda i: (layer, 0, 0)),
            pl.BlockSpec((tm, HEAD_DIM), lambda i: (i % n_pos, 0)),
            pl.BlockSpec((tm, HEAD_DIM), lambda i: (i % n_pos, 0)),
        ],
        out_specs=pl.BlockSpec((tm, QKV_WIDTH), lambda i: (i, 0)),
        compiler_params=_params(("parallel",)),
        name="qkv",
    )(x, w_qkv, qk_norm_g, cos2, sin2)


def _diff_body(q_ref, k_ref, v_ref, bias_ref, lam_ref, sg_ref, o_ref, s1_ref, s2_ref, *, t, nc, lambda_init):
    i = pl.program_id(2)
    q = q_ref[...]
    q1 = q[:, :HEAD_DIM]
    q2 = q[:, HEAD_DIM:]

    m1 = m2 = None
    for c in range(nc):
        kc = k_ref[c * t:(c + 1) * t, :]
        b = bias_ref[jnp.clip(c - i, -2, 2) + 2]
        s1 = _dot_nt(q1, kc[:, :HEAD_DIM]) + b
        s2 = _dot_nt(q2, kc[:, HEAD_DIM:]) + b
        s1_ref[c] = s1
        s2_ref[c] = s2
        f1 = _lane_fold(s1, jnp.maximum)
        f2 = _lane_fold(s2, jnp.maximum)
        m1 = f1 if m1 is None else jnp.maximum(m1, f1)
        m2 = f2 if m2 is None else jnp.maximum(m2, f2)
    m1 = _lane_bcast(jnp.max(m1, axis=-1, keepdims=True))
    m2 = _lane_bcast(jnp.max(m2, axis=-1, keepdims=True))

    acc = l1 = l2 = None
    for c in range(nc):
        e1 = _exp2_minus(s1_ref[c], m1)
        e2 = _exp2_minus(s2_ref[c], m2)
        e = jnp.concatenate([e1.astype(BF16), e2.astype(BF16)], axis=0)
        d = _dot(e, v_ref[c * t:(c + 1) * t, :])
        f1 = _lane_fold(e1, jnp.add)
        f2 = _lane_fold(e2, jnp.add)
        acc = d if acc is None else acc + d
        l1 = f1 if l1 is None else l1 + f1
        l2 = f2 if l2 is None else l2 + f2
    l1 = jnp.sum(l1, axis=-1, keepdims=True)
    l2 = jnp.sum(l2, axis=-1, keepdims=True)

    lam = lam_ref[...]
    lam_full = (jnp.exp(jnp.sum(lam[0:1] * lam[1:2], axis=-1, keepdims=True))
                - jnp.exp(jnp.sum(lam[2:3] * lam[3:4], axis=-1, keepdims=True)) + lambda_init)
    o = acc[0:t, :] / l1 - lam_full * (acc[t:2 * t, :] / l2)
    o = o * lax.rsqrt(jnp.mean(o * o, axis=-1, keepdims=True) + RMS_EPS) * sg_ref[...]
    o_ref[...] = (o * (1.0 - lambda_init)).astype(BF16)


def _diff_attention(qkv, bias_a, lambda_qk, subln_g, layer, n_seq, s, lambda_init, *, t=256):
    m = qkv.shape[0]
    nc = s // t
    return pl.pallas_call(
        functools.partial(_diff_body, t=t, nc=nc, lambda_init=lambda_init),
        out_shape=jax.ShapeDtypeStruct((m, BRANCH_WIDTH), BF16),
        grid=(n_seq, A_HEADS, nc),
        in_specs=[
            pl.BlockSpec((t, A_V), lambda b, h, i: (b * nc + i, QA_OFF // A_V + h)),
            pl.BlockSpec((s, A_V), lambda b, h, i: (b, KA_OFF // A_V + h)),
            pl.BlockSpec((s, A_V), lambda b, h, i: (b, VA_OFF // A_V + h)),
            pl.BlockSpec((None, 5, t, t), lambda b, h, i: (h, 0, 0, 0)),
            pl.BlockSpec((None, 4, HEAD_DIM), lambda b, h, i: (layer, 0, 0)),
            pl.BlockSpec((None, 1, A_V), lambda b, h, i: (layer, 0, 0)),
        ],
        out_specs=pl.BlockSpec((t, A_V), lambda b, h, i: (b * nc + i, h)),
        scratch_shapes=[pltpu.VMEM((nc, t, t), F32), pltpu.VMEM((nc, t, t), F32)],
        compiler_params=_params(("parallel", "parallel", "arbitrary")),
        name="diff_attn",
    )(qkv, qkv, qkv, bias_a, lambda_qk, subln_g)


def _axial_body(q_ref, k_ref, v_ref, o_ref, s_ref, *, t, tk, nc):
    g = C_HEADS // C_KV_HEADS
    q = q_ref[...]
    half = g // 2
    qs = [jnp.concatenate([q[:, h * HEAD_DIM:(h + 1) * HEAD_DIM] for h in range(r * half, (r + 1) * half)], axis=0)
          for r in range(2)]

    m = [None, None]
    for c in range(nc):
        kc = k_ref[c * tk:(c + 1) * tk, :]
        for r in range(2):
            sc = _dot_nt(qs[r], kc)
            s_ref[r, c] = sc
            f = _lane_fold(sc, jnp.maximum)
            m[r] = f if m[r] is None else jnp.maximum(m[r], f)
    m = [_lane_bcast(jnp.max(mr, axis=-1, keepdims=True)) for mr in m]

    acc = [None, None]
    l = [None, None]
    for c in range(nc):
        vc = v_ref[c * tk:(c + 1) * tk, :]
        for r in range(2):
            e = _exp2_minus(s_ref[r, c], m[r])
            d = _dot(e.astype(BF16), vc)
            f = _lane_fold(e, jnp.add)
            acc[r] = d if acc[r] is None else acc[r] + d
            l[r] = f if l[r] is None else l[r] + f
    for r in range(2):
        o = acc[r] / jnp.sum(l[r], axis=-1, keepdims=True)
        for hh in range(half):
            h = r * half + hh
            o_ref[:, h * HEAD_DIM:(h + 1) * HEAD_DIM] = o[hh * t:(hh + 1) * t, :].astype(BF16)


def _axial_attention(qkv, n_seq, s, *, t=256, tk=256):
    m = qkv.shape[0]
    g = C_HEADS // C_KV_HEADS
    gw = g * HEAD_DIM
    nq = s // t
    nc = s // tk
    return pl.pallas_call(
        functools.partial(_axial_body, t=t, tk=tk, nc=nc),
        out_shape=jax.ShapeDtypeStruct((m, BRANCH_WIDTH), BF16),
        grid=(n_seq, C_KV_HEADS, nq),
        in_specs=[
            pl.BlockSpec((t, gw), lambda b, kv, i: (b * nq + i, QC_OFF // gw + kv)),
            pl.BlockSpec((s, HEAD_DIM), lambda b, kv, i: (b, KC_OFF // HEAD_DIM + kv)),
            pl.BlockSpec((s, HEAD_DIM), lambda b, kv, i: (b, VC_OFF // HEAD_DIM + kv)),
        ],
        out_specs=pl.BlockSpec((t, gw), lambda b, kv, i: (b * nq + i, kv)),
        scratch_shapes=[pltpu.VMEM((2, nc, g // 2 * t, tk), F32)],
        compiler_params=_params(("parallel", "parallel", "arbitrary")),
        name="axial_attn",
    )(qkv, qkv, qkv)


def _window_body(sink_ref, q_ref, k_ref, v_ref, bias_ref, o_ref, *, nb):
    g = B_HEADS // B_KV_HEADS
    t = WINDOW
    kv = pl.program_id(1)
    sink = jnp.concatenate([jnp.full((1, t), sink_ref[kv * g + h] * LOG2E, F32) for h in range(g)], axis=1)

    def block(i, carry):
        r0 = pl.multiple_of(i * t, t)
        w0 = pl.multiple_of(jnp.clip(i - 1, 0, nb - 3) * t, t)
        tile = jnp.where(i == 0, 0, jnp.where(i == nb - 1, 2, 1))
        q = q_ref[pl.ds(r0, t), :]
        qs = jnp.concatenate([q[:, h * HEAD_DIM:(h + 1) * HEAD_DIM] for h in range(g)], axis=0)
        st = _dot_nt(k_ref[pl.ds(w0, 3 * t), :], qs) + bias_ref[tile]
        mx = jnp.maximum(jnp.max(st, axis=0, keepdims=True), sink)
        e = jnp.exp2(st - mx)
        l = jnp.sum(e, axis=0, keepdims=True) + jnp.exp2(sink - mx)
        p = (e * (1.0 / l)).astype(BF16)
        ot = lax.dot_general(v_ref[pl.ds(w0, 3 * t), :], p, (((0,), (0,)), ((), ())),
                             preferred_element_type=F32)
        for h in range(g):
            o_ref[pl.ds(r0, t), h * HEAD_DIM:(h + 1) * HEAD_DIM] = ot[:, h * t:(h + 1) * t].T.astype(BF16)
        return carry

    lax.fori_loop(0, nb, block, 0, unroll=2)


def _window_attention(qkv, bias_b, sink_logits, layer, n_seq, s):
    m = qkv.shape[0]
    g = B_HEADS // B_KV_HEADS
    gw = g * HEAD_DIM
    t = WINDOW
    nb = s // t
    grid_spec = pltpu.PrefetchScalarGridSpec(
        num_scalar_prefetch=1,
        grid=(n_seq, B_KV_HEADS),
        in_specs=[
            pl.BlockSpec((s, gw), lambda b, kv, sink: (b, QB_OFF // gw + kv)),
            pl.BlockSpec((s, HEAD_DIM), lambda b, kv, sink: (b, KB_OFF // HEAD_DIM + kv)),
            pl.BlockSpec((s, HEAD_DIM), lambda b, kv, sink: (b, VB_OFF // HEAD_DIM + kv)),
            pl.BlockSpec((3, None, 3 * t, g * t), lambda b, kv, sink: (0, kv, 0, 0)),
        ],
        out_specs=pl.BlockSpec((s, gw), lambda b, kv, sink: (b, kv)),
    )
    return pl.pallas_call(
        functools.partial(_window_body, nb=nb),
        out_shape=jax.ShapeDtypeStruct((m, BRANCH_WIDTH), BF16),
        grid_spec=grid_spec,
        compiler_params=_params(("parallel", "parallel")),
        name="window_attn",
    )(sink_logits[layer], qkv, qkv, qkv, bias_b)


def _gate_body(x_ref, ya_ref, yb_ref, yc_ref, wg_ref, wb_ref, o_ref, xb_ref):
    @pl.when(pl.program_id(1) == 0)
    def _():
        xb_ref[...] = x_ref[...].astype(BF16)

    xb = xb_ref[...]
    merged = None
    for n, y_ref in enumerate((ya_ref, yb_ref, yc_ref)):
        gate = jax.nn.sigmoid(_dot(xb, wg_ref[n]))
        term = gate * _dot(y_ref[...], wb_ref[n])
        merged = term if merged is None else merged + term
    o_ref[...] = merged.astype(BF16)


def _outproj_body(x_ref, m_ref, wo_ref, g_ref, b_ref, o_ref, *, n_split):
    rows = x_ref.shape[0] // n_split
    for r in range(n_split):
        sl = slice(r * rows, (r + 1) * rows)
        y = ALPHA * x_ref[sl, :] + _dot(m_ref[sl, :], wo_ref[...])
        o_ref[sl, :] = _layer_norm(y, g_ref[...], b_ref[...])


def _merge(x, ya, yb, yc, w_gate, w_branch, w_out, ln_g, ln_b, layer, *, tm=512, tc=512):
    m = x.shape[0]
    yspec = pl.BlockSpec((tm, BRANCH_WIDTH), lambda i, j: (i, 0))
    merged = pl.pallas_call(
        _gate_body,
        out_shape=jax.ShapeDtypeStruct((m, D_MODEL), BF16),
        grid=(m // tm, D_MODEL // tc),
        in_specs=[
            pl.BlockSpec((tm, D_MODEL), lambda i, j: (i, 0)),
            yspec, yspec, yspec,
            pl.BlockSpec((None, N_BRANCH, D_MODEL, tc), lambda i, j: (layer, 0, 0, j)),
            pl.BlockSpec((None, N_BRANCH, BRANCH_WIDTH, tc), lambda i, j: (layer, 0, 0, j)),
        ],
        out_specs=pl.BlockSpec((tm, tc), lambda i, j: (i, j)),
        scratch_shapes=[pltpu.VMEM((tm, D_MODEL), BF16)],
        compiler_params=_params(("parallel", "arbitrary")),
        name="gate_merge",
    )(x, ya, yb, yc, w_gate, w_branch)
    return pl.pallas_call(
        functools.partial(_outproj_body, n_split=2),
        out_shape=jax.ShapeDtypeStruct((m, D_MODEL), F32),
        grid=(m // tm,),
        in_specs=[
            pl.BlockSpec((tm, D_MODEL), lambda i: (i, 0)),
            pl.BlockSpec((tm, D_MODEL), lambda i: (i, 0)),
            pl.BlockSpec((None, D_MODEL, D_MODEL), lambda i: (layer, 0, 0), pipeline_mode=pl.Buffered(1)),
            pl.BlockSpec((None, None, 1, D_MODEL), lambda i: (layer, 1, 0, 0)),
            pl.BlockSpec((None, None, 1, D_MODEL), lambda i: (layer, 1, 0, 0)),
        ],
        out_specs=pl.BlockSpec((tm, D_MODEL), lambda i: (i, 0)),
        compiler_params=_params(("parallel",)),
        name="out_proj",
    )(x, merged, w_out, ln_g, ln_b)


def _memkv_body(m_ref, w_ref, o_ref):
    o_ref[...] = _dot(m_ref[...].astype(BF16), w_ref[...]).astype(BF16)


def _memkv(mem, w_kv, layer):
    rows = mem.shape[0]
    return pl.pallas_call(
        _memkv_body,
        out_shape=jax.ShapeDtypeStruct((rows, 2 * MEM_INNER), BF16),
        grid=(rows // MEM_TOKENS,),
        in_specs=[
            pl.BlockSpec((MEM_TOKENS, D_MODEL), lambda i: (i, 0)),
            pl.BlockSpec((None, D_MODEL, 2 * MEM_INNER), lambda i: (layer, 0, 0)),
        ],
        out_specs=pl.BlockSpec((MEM_TOKENS, 2 * MEM_INNER), lambda i: (i, 0)),
        compiler_params=_params(("parallel",)),
        name="memkv",
    )(mem, w_kv)


def _mem_body(x_ref, kv_ref, wq_ref, wo_ref, g_ref, b_ref, o_ref):
    x = x_ref[...]
    q = (_dot(x.astype(BF16), wq_ref[...]) * QSCALE).astype(BF16)
    kv = kv_ref[...]
    heads = []
    for h in range(MEM_HEADS):
        sc = _dot_nt(q[:, h * HEAD_DIM:(h + 1) * HEAD_DIM], kv[:, h * HEAD_DIM:(h + 1) * HEAD_DIM])
        e = jnp.exp2(sc - jnp.max(sc, axis=-1, keepdims=True))
        l = jnp.sum(e, axis=-1, keepdims=True)
        v = kv[:, MEM_INNER + h * HEAD_DIM:MEM_INNER + (h + 1) * HEAD_DIM]
        heads.append((_dot(e.astype(BF16), v) / l).astype(BF16))
    o = jnp.concatenate(heads, axis=1)
    y = ALPHA * x + _dot(o, wo_ref[...])
    o_ref[...] = _layer_norm(y, g_ref[...], b_ref[...])


def _mem_attention(x, kv, w_q, w_o, ln_g, ln_b, layer, s, *, tm=512):
    m = x.shape[0]
    per_seq = s // tm
    return pl.pallas_call(
        _mem_body,
        out_shape=jax.ShapeDtypeStruct((m, D_MODEL), F32),
        grid=(m // tm,),
        in_specs=[
            pl.BlockSpec((tm, D_MODEL), lambda i: (i, 0)),
            pl.BlockSpec((MEM_TOKENS, 2 * MEM_INNER), lambda i: (i // per_seq, 0)),
            pl.BlockSpec((None, D_MODEL, MEM_INNER), lambda i: (layer, 0, 0)),
            pl.BlockSpec((None, MEM_INNER, D_MODEL), lambda i: (layer, 0, 0)),
            pl.BlockSpec((None, None, 1, D_MODEL), lambda i: (layer, 2, 0, 0)),
            pl.BlockSpec((None, None, 1, D_MODEL), lambda i: (layer, 2, 0, 0)),
        ],
        out_specs=pl.BlockSpec((tm, D_MODEL), lambda i: (i, 0)),
        compiler_params=_params(("parallel",)),
        name="mem_attn",
    )(x, kv, w_q, w_o, ln_g, ln_b)


def _t5_bucket_np(rel):
    half = NUM_BUCKETS // 2
    max_exact = half // 2
    rel = np.asarray(rel, np.int32)
    ret = np.where(rel > 0, half, 0)
    n = np.abs(rel)
    ratio = np.log(np.maximum(n, 1).astype(np.float32) / np.float32(max_exact)) / np.float32(math.log(MAX_DISTANCE / max_exact))
    large = max_exact + (ratio * np.float32(half - max_exact)).astype(np.int32)
    large = np.minimum(large, half - 1)
    return (ret + np.where(n < max_exact, n, large)).astype(np.int32)


def _lookup_bias(table, buckets):
    n_heads = table.shape[1]
    b = jnp.asarray(buckets)[None]
    tab = table * LOG2E
    out = jnp.zeros((n_heads,) + buckets.shape, F32)
    for k in range(NUM_BUCKETS):
        out = jnp.where(b == k, tab[k].reshape((n_heads,) + (1,) * buckets.ndim), out)
    return out


def _diff_bias_tiles(table_a, t):
    ii = np.arange(t)[:, None]
    jj = np.arange(t)[None, :]
    buckets = np.stack([_t5_bucket_np(d * t + jj - ii) for d in (-2, -1, 0, 1, 2)])
    return _lookup_bias(table_a, buckets)


def _window_bias_tiles(table_b):
    g = B_HEADS // B_KV_HEADS
    rel = np.stack([(np.arange(3 * WINDOW) - p * WINDOW)[None, :] - np.arange(WINDOW)[:, None] for p in range(3)])
    bias = _lookup_bias(table_b, _t5_bucket_np(rel))
    bias = jnp.where(jnp.asarray(np.abs(rel) <= WINDOW)[None], bias, NEG_INF)
    bias = bias.reshape(B_KV_HEADS, g, 3, WINDOW, 3 * WINDOW)
    return jnp.transpose(bias, (2, 0, 4, 1, 3)).reshape(3, B_KV_HEADS, 3 * WINDOW, g * WINDOW)


def _rope_tables(s):
    n_rows = s // GRID_W
    row = np.repeat(np.arange(n_rows), GRID_W).astype(np.float32)
    col = np.tile(np.arange(GRID_W), n_rows).astype(np.float32)
    axis_dims = HEAD_DIM // 2
    inv = ROPE_THETA ** (-jnp.arange(0, axis_dims, 2, dtype=F32) / axis_dims)
    ang = jnp.concatenate([row[:, None] * inv[None, :], col[:, None] * inv[None, :]], axis=-1)
    cos, sin = jnp.cos(ang), jnp.sin(ang)
    return jnp.concatenate([cos, cos], axis=-1), jnp.concatenate([-sin, sin], axis=-1)


def _run_trunk(x, mem, wts, bias_a, bias_b):
    n_seq, s, _ = x.shape
    x = x.reshape(n_seq * s, D_MODEL)
    mem = mem.reshape(n_seq * MEM_TOKENS, D_MODEL)
    cos2, sin2 = _rope_tables(s)
    for i in range(DEPTH):
        lambda_init = 0.8 - 0.6 * math.exp(-0.3 * i)
        x = _ffn(x, wts["w_ffn_gu"], wts["w_ffn_d"], wts["ln_g"], wts["ln_b"], i, 0, 0)
        qkv = _qkv(x, wts["w_qkv"], wts["qk_norm_g"], cos2, sin2, i)
        ya = _diff_attention(qkv, bias_a, wts["lambda_qk"], wts["subln_g"], i, n_seq, s, lambda_init)
        yb = _window_attention(qkv, bias_b, wts["sink_logits"], i, n_seq, s)
        yc = _axial_attention(qkv, n_seq, s)
        x = _merge(x, ya, yb, yc, wts["w_gate"], wts["w_branch"], wts["w_out"], wts["ln_g"], wts["ln_b"], i)
        kv = _memkv(mem, wts["w_mem_kv"], i)
        x = _mem_attention(x, kv, wts["w_mem_q"], wts["w_mem_o"], wts["ln_g"], wts["ln_b"], i, s)
        x = _ffn(x, wts["w_ffn_gu"], wts["w_ffn_d"], wts["ln_g"], wts["ln_b"], i, 1, 3)
    return x.reshape(n_seq, s, D_MODEL)


def kernel(x_prompt, x_sample, mem_prompt, mem_sample, rel_bias_table, w_in, w_branch, w_out, lambda_qk, subln_g, sink_logits, qk_norm_g, w_mem_q, w_mem_kv, w_mem_o, w_ffn_gu, w_ffn_d, ln_g, ln_b):
    wts = {
        "w_qkv": w_in[:, :, :QKV_WIDTH].astype(BF16),
        "w_gate": jnp.transpose(w_in[:, :, QKV_WIDTH:].reshape(DEPTH, D_MODEL, N_BRANCH, D_MODEL), (0, 2, 1, 3)).astype(BF16),
        "w_branch": w_branch.astype(BF16),
        "w_out": w_out.astype(BF16),
        "w_mem_q": w_mem_q.astype(BF16),
        "w_mem_kv": w_mem_kv.astype(BF16),
        "w_mem_o": w_mem_o.astype(BF16),
        "w_ffn_gu": w_ffn_gu.astype(BF16),
        "w_ffn_d": w_ffn_d.astype(BF16),
        "lambda_qk": lambda_qk,
        "subln_g": subln_g.reshape(DEPTH, 1, A_V),
        "sink_logits": sink_logits,
        "qk_norm_g": qk_norm_g,
        "ln_g": ln_g.reshape(DEPTH, 4, 1, D_MODEL),
        "ln_b": ln_b.reshape(DEPTH, 4, 1, D_MODEL),
    }
    bias_a = _diff_bias_tiles(rel_bias_table[:, :A_HEADS], 256)
    bias_b = _window_bias_tiles(rel_bias_table[:, A_HEADS:])
    y_prompt = _run_trunk(x_prompt, mem_prompt, wts, bias_a, bias_b)
    y_sample = _run_trunk(x_sample, mem_sample, wts, bias_a, bias_b)
    return (y_prompt, y_sample)
```

```python
import functools
import math

import numpy as np
import jax
import jax.numpy as jnp
from jax import lax
from jax.experimental import pallas as pl
from jax.experimental.pallas import tpu as pltpu

D_MODEL = 2048
DEPTH = 4
HEAD_DIM = 128
A_HEADS = 4
A_V = 2 * HEAD_DIM
B_HEADS = 8
B_KV_HEADS = 2
WINDOW = 128
C_HEADS = 8
C_KV_HEADS = 2
GRID_W = 64
ROPE_THETA = 10000.0
NUM_BUCKETS = 32
MAX_DISTANCE = 128
MEM_TOKENS = 256
MEM_HEADS = 4
MEM_INNER = MEM_HEADS * HEAD_DIM
D_FF = 5632
N_BRANCH = 3
BRANCH_WIDTH = 1024
QKV_WIDTH = 6144
ALPHA = (2 * DEPTH) ** 0.25
LN_EPS = 1e-5
RMS_EPS = 1e-6
NEG_INF = -1e30
LOG2E = math.log2(math.e)
QSCALE = HEAD_DIM ** -0.5 * LOG2E

QA_OFF, KA_OFF, VA_OFF = 0, 1024, 2048
QB_OFF, KB_OFF, VB_OFF = 3072, 4096, 4352
QC_OFF, KC_OFF, VC_OFF = 4608, 5632, 5888

V7X_VMEM_BYTES = 64 * 1024 * 1024
VMEM_LIMIT = 56 * 1024 * 1024

F32 = jnp.float32
BF16 = jnp.bfloat16


def _params(semantics):
    return pltpu.CompilerParams(dimension_semantics=semantics, vmem_limit_bytes=VMEM_LIMIT)


def _layer_norm(y, g, b):
    mu = jnp.mean(y, axis=-1, keepdims=True)
    d = y - mu
    var = jnp.mean(d * d, axis=-1, keepdims=True)
    return d * lax.rsqrt(var + LN_EPS) * g + b


def _dot(a, b):
    return jnp.dot(a, b, preferred_element_type=F32)


def _dot_nt(a, b):
    return lax.dot_general(a, b, (((1,), (1,)), ((), ())), preferred_element_type=F32)


def _lane_fold(x, op):
    parts = [x[:, j * 128:(j + 1) * 128] for j in range(x.shape[1] // 128)]
    return functools.reduce(op, parts)


def _exp2_minus(x, col):
    parts = [jnp.exp2(x[:, j * 128:(j + 1) * 128] - col) for j in range(x.shape[1] // 128)]
    return parts[0] if len(parts) == 1 else jnp.concatenate(parts, axis=1)


def _lane_bcast(col):
    return jnp.broadcast_to(col, (col.shape[0], 128))


def _ffn_body(x_ref, wg_ref, wu_ref, wd_ref, g_ref, b_ref, o_ref, xb_ref, acc_ref, *, n_f):
    f = pl.program_id(1)

    @pl.when(f == 0)
    def _():
        xb_ref[...] = x_ref[...].astype(BF16)
        acc_ref[...] = jnp.zeros_like(acc_ref)

    xb = xb_ref[...]
    h = _dot(xb, wg_ref[...])
    u = _dot(xb, wu_ref[...])
    a = (h * jax.nn.sigmoid(h) * u).astype(BF16)
    acc_ref[...] += _dot(a, wd_ref[...])

    @pl.when(f == n_f - 1)
    def _():
        y = ALPHA * x_ref[...] + 0.5 * acc_ref[...]
        o_ref[...] = _layer_norm(y, g_ref[...], b_ref[...])


def _ffn(x, w_gu, w_d, ln_g, ln_b, layer, which, ln_idx, *, tm=512, tf=512):
    m = x.shape[0]
    n_f = D_FF // tf
    return pl.pallas_call(
        functools.partial(_ffn_body, n_f=n_f),
        out_shape=jax.ShapeDtypeStruct((m, D_MODEL), F32),
        grid=(m // tm, n_f),
        in_specs=[
            pl.BlockSpec((tm, D_MODEL), lambda i, f: (i, 0)),
            pl.BlockSpec((None, None, D_MODEL, tf), lambda i, f: (layer, which, 0, f)),
            pl.BlockSpec((None, None, D_MODEL, tf), lambda i, f: (layer, which, 0, f + n_f)),
            pl.BlockSpec((None, None, tf, D_MODEL), lambda i, f: (layer, which, f, 0)),
            pl.BlockSpec((None, None, 1, D_MODEL), lambda i, f: (layer, ln_idx, 0, 0)),
            pl.BlockSpec((None, None, 1, D_MODEL), lambda i, f: (layer, ln_idx, 0, 0)),
        ],
        out_specs=pl.BlockSpec((tm, D_MODEL), lambda i, f: (i, 0)),
        scratch_shapes=[pltpu.VMEM((tm, D_MODEL), BF16), pltpu.VMEM((tm, D_MODEL), F32)],
        compiler_params=_params(("parallel", "arbitrary")),
        name="ffn",
    )(x, w_gu, w_gu, w_d, ln_g, ln_b)


def _qkv_body(x_ref, w_ref, gn_ref, cos_ref, sin_ref, o_ref, *, tn):
    xb = x_ref[...].astype(BF16)
    cos = cos_ref[...]
    sin = sin_ref[...]
    for j in range(QKV_WIDTH // tn):
        c0 = j * tn
        acc = _dot(xb, w_ref[:, c0:c0 + tn])
        is_q = (QA_OFF <= c0 < KA_OFF) or (QB_OFF <= c0 < KB_OFF) or (QC_OFF <= c0 < KC_OFF)
        if QC_OFF <= c0 < VC_OFF:
            g = gn_ref[0:1, :] if c0 < KC_OFF else gn_ref[1:2, :]
            for hh in range(tn // HEAD_DIM):
                a = acc[:, hh * HEAD_DIM:(hh + 1) * HEAD_DIM]
                r = lax.rsqrt(jnp.mean(a * a, axis=-1, keepdims=True) + RMS_EPS)
                y = a * r * g
                y = y * cos + pltpu.roll(y, HEAD_DIM // 2, 1) * sin
                if is_q:
                    y = y * QSCALE
                o_ref[:, c0 + hh * HEAD_DIM:c0 + (hh + 1) * HEAD_DIM] = y.astype(BF16)
        else:
            if is_q:
                acc = acc * QSCALE
            o_ref[:, c0:c0 + tn] = acc.astype(BF16)


def _qkv(x, w_qkv, qk_norm_g, cos2, sin2, layer, *, tm=512, tn=256):
    m = x.shape[0]
    s = cos2.shape[0]
    n_pos = s // tm
    return pl.pallas_call(
        functools.partial(_qkv_body, tn=tn),
        out_shape=jax.ShapeDtypeStruct((m, QKV_WIDTH), BF16),
        grid=(m // tm,),
        in_specs=[
            pl.BlockSpec((tm, D_MODEL), lambda i: (i, 0)),
            pl.BlockSpec((None, D_MODEL, QKV_WIDTH), lambda i: (layer, 0, 0), pipeline_mode=pl.Buffered(1)),
            pl.BlockSpec((None, 2, HEAD_DIM), lambda i: (layer, 0, 0)),
            pl.BlockSpec((tm, HEAD_DIM), lambda i: (i % n_pos, 0)),
            pl.BlockSpec((tm, HEAD_DIM), lambda i: (i % n_pos, 0)),
        ],
        out_specs=pl.BlockSpec((tm, QKV_WIDTH), lambda i: (i, 0)),
        compiler_params=_params(("parallel",)),
        name="qkv",
    )(x, w_qkv, qk_norm_g, cos2, sin2)


def _diff_body(q_ref, k_ref, v_ref, bias_ref, lam_ref, sg_ref, o_ref, s1_ref, s2_ref, m_ref, *, t, nc, nq, n_items, lambda_init):
    g = pl.program_id(0)
    i = jnp.minimum(g, n_items - 1) % nq

    @pl.when(g == 0)
    def _():
        s1_ref[1] = jnp.zeros(s1_ref.shape[1:], F32)
        s2_ref[1] = jnp.zeros(s2_ref.shape[1:], F32)
        m_ref[1] = jnp.zeros(m_ref.shape[1:], F32)

    def step(cur, prv):
        q = q_ref[...]
        q1 = q[:, :HEAD_DIM]
        q2 = q[:, HEAD_DIM:]
        m1p = m_ref[prv, 0]
        m2p = m_ref[prv, 1]
        m1 = m2 = acc = l1 = l2 = None
        for c in range(nc):
            kc = k_ref[c * t:(c + 1) * t, :]
            b = bias_ref[jnp.clip(c - i, -2, 2) + 2]
            s1 = _dot_nt(q1, kc[:, :HEAD_DIM]) + b
            s2 = _dot_nt(q2, kc[:, HEAD_DIM:]) + b
            s1_ref[cur, c] = s1
            s2_ref[cur, c] = s2
            f1 = _lane_fold(s1, jnp.maximum)
            f2 = _lane_fold(s2, jnp.maximum)
            m1 = f1 if m1 is None else jnp.maximum(m1, f1)
            m2 = f2 if m2 is None else jnp.maximum(m2, f2)

            e1 = _exp2_minus(s1_ref[prv, c], m1p)
            e2 = _exp2_minus(s2_ref[prv, c], m2p)
            e = jnp.concatenate([e1.astype(BF16), e2.astype(BF16)], axis=0)
            d = _dot(e, v_ref[c * t:(c + 1) * t, :])
            f1 = _lane_fold(e1, jnp.add)
            f2 = _lane_fold(e2, jnp.add)
            acc = d if acc is None else acc + d
            l1 = f1 if l1 is None else l1 + f1
            l2 = f2 if l2 is None else l2 + f2
        m_ref[cur, 0] = _lane_bcast(jnp.max(m1, axis=-1, keepdims=True))
        m_ref[cur, 1] = _lane_bcast(jnp.max(m2, axis=-1, keepdims=True))

        l1 = jnp.sum(l1, axis=-1, keepdims=True)
        l2 = jnp.sum(l2, axis=-1, keepdims=True)
        lam = lam_ref[...]
        lam_full = (jnp.exp(jnp.sum(lam[0:1] * lam[1:2], axis=-1, keepdims=True))
                    - jnp.exp(jnp.sum(lam[2:3] * lam[3:4], axis=-1, keepdims=True)) + lambda_init)
        o = acc[0:t, :] / l1 - lam_full * (acc[t:2 * t, :] / l2)
        o = o * lax.rsqrt(jnp.mean(o * o, axis=-1, keepdims=True) + RMS_EPS) * sg_ref[...]
        o_ref[...] = (o * (1.0 - lambda_init)).astype(BF16)

    @pl.when(g % 2 == 0)
    def _():
        step(0, 1)

    @pl.when(g % 2 == 1)
    def _():
        step(1, 0)


def _pipeline_items(n_items, n_heads, nq):
    def scored(g):
        it = jnp.minimum(g, n_items - 1)
        return it // (n_heads * nq), (it // nq) % n_heads, it % nq

    def finished(g):
        it = jnp.maximum(g - 1, 0)
        return it // (n_heads * nq), (it // nq) % n_heads, it % nq

    return scored, finished


def _diff_attention(qkv, bias_a, lambda_qk, subln_g, layer, n_seq, s, lambda_init, *, t=256):
    m = qkv.shape[0]
    nc = s // t
    nq = s // t
    n_items = n_seq * A_HEADS * nq
    scored, finished = _pipeline_items(n_items, A_HEADS, nq)

    def q_map(g):
        b, h, i = scored(g)
        return (b * nq + i, QA_OFF // A_V + h)

    def k_map(g):
        b, h, _ = scored(g)
        return (b, KA_OFF // A_V + h)

    def v_map(g):
        b, h, _ = finished(g)
        return (b, VA_OFF // A_V + h)

    def o_map(g):
        b, h, i = finished(g)
        return (b * nq + i, h)

    return pl.pallas_call(
        functools.partial(_diff_body, t=t, nc=nc, nq=nq, n_items=n_items, lambda_init=lambda_init),
        out_shape=jax.ShapeDtypeStruct((m, BRANCH_WIDTH), BF16),
        grid=(n_items + 1,),
        in_specs=[
            pl.BlockSpec((t, A_V), q_map),
            pl.BlockSpec((s, A_V), k_map),
            pl.BlockSpec((s, A_V), v_map),
            pl.BlockSpec((None, 5, t, t), lambda g: (scored(g)[1], 0, 0, 0)),
            pl.BlockSpec((None, 4, HEAD_DIM), lambda g: (layer, 0, 0)),
            pl.BlockSpec((None, 1, A_V), lambda g: (layer, 0, 0)),
        ],
        out_specs=pl.BlockSpec((t, A_V), o_map),
        scratch_shapes=[pltpu.VMEM((2, nc, t, t), F32), pltpu.VMEM((2, nc, t, t), F32), pltpu.VMEM((2, 2, t, 128), F32)],
        compiler_params=_params(("arbitrary",)),
        name="diff_attn",
    )(qkv, qkv, qkv, bias_a, lambda_qk, subln_g)


def _axial_body(q_ref, k_ref, v_ref, o_ref, s_ref, m_ref, *, t, tk, nc):
    gq = C_HEADS // C_KV_HEADS
    g = pl.program_id(0)
    half = gq // 2

    @pl.when(g == 0)
    def _():
        s_ref[1] = jnp.zeros(s_ref.shape[1:], F32)
        m_ref[1] = jnp.zeros(m_ref.shape[1:], F32)

    def step(cur, prv):
        q = q_ref[...]
        qs = [jnp.concatenate([q[:, h * HEAD_DIM:(h + 1) * HEAD_DIM] for h in range(r * half, (r + 1) * half)], axis=0)
              for r in range(2)]
        mp = [m_ref[prv, r] for r in range(2)]
        m = [None, None]
        acc = [None, None]
        l = [None, None]
        for c in range(nc):
            kc = k_ref[c * tk:(c + 1) * tk, :]
            vc = v_ref[c * tk:(c + 1) * tk, :]
            for r in range(2):
                sc = _dot_nt(qs[r], kc)
                s_ref[cur, r, c] = sc
                f = _lane_fold(sc, jnp.maximum)
                m[r] = f if m[r] is None else jnp.maximum(m[r], f)

                e = _exp2_minus(s_ref[prv, r, c], mp[r])
                d = _dot(e.astype(BF16), vc)
                f = _lane_fold(e, jnp.add)
                acc[r] = d if acc[r] is None else acc[r] + d
                l[r] = f if l[r] is None else l[r] + f
        for r in range(2):
            m_ref[cur, r] = _lane_bcast(jnp.max(m[r], axis=-1, keepdims=True))
            o = acc[r] / jnp.sum(l[r], axis=-1, keepdims=True)
            for hh in range(half):
                h = r * half + hh
                o_ref[:, h * HEAD_DIM:(h + 1) * HEAD_DIM] = o[hh * t:(hh + 1) * t, :].astype(BF16)

    @pl.when(g % 2 == 0)
    def _():
        step(0, 1)

    @pl.when(g % 2 == 1)
    def _():
        step(1, 0)


def _axial_attention(qkv, n_seq, s, *, t=256, tk=256):
    m = qkv.shape[0]
    gq = C_HEADS // C_KV_HEADS
    gw = gq * HEAD_DIM
    nq = s // t
    nc = s // tk
    n_items = n_seq * C_KV_HEADS * nq
    scored, finished = _pipeline_items(n_items, C_KV_HEADS, nq)

    def q_map(g):
        b, kv, i = scored(g)
        return (b * nq + i, QC_OFF // gw + kv)

    def k_map(g):
        b, kv, _ = scored(g)
        return (b, KC_OFF // HEAD_DIM + kv)

    def v_map(g):
        b, kv, _ = finished(g)
        return (b, VC_OFF // HEAD_DIM + kv)

    def o_map(g):
        b, kv, i = finished(g)
        return (b * nq + i, kv)

    return pl.pallas_call(
        functools.partial(_axial_body, t=t, tk=tk, nc=nc),
        out_shape=jax.ShapeDtypeStruct((m, BRANCH_WIDTH), BF16),
        grid=(n_items + 1,),
        in_specs=[
            pl.BlockSpec((t, gw), q_map),
            pl.BlockSpec((s, HEAD_DIM), k_map),
            pl.BlockSpec((s, HEAD_DIM), v_map),
        ],
        out_specs=pl.BlockSpec((t, gw), o_map),
        scratch_shapes=[pltpu.VMEM((2, 2, nc, gq // 2 * t, tk), F32), pltpu.VMEM((2, 2, gq // 2 * t, 128), F32)],
        compiler_params=_params(("arbitrary",)),
        name="axial_attn",
    )(qkv, qkv, qkv)


def _window_body(sink_ref, q_ref, k_ref, v_ref, bias_ref, o_ref, *, nb):
    g = B_HEADS // B_KV_HEADS
    t = WINDOW
    kv = pl.program_id(1)
    sink = jnp.concatenate([jnp.full((1, t), sink_ref[kv * g + h] * LOG2E, F32) for h in range(g)], axis=1)

    def block(i, carry):
        r0 = pl.multiple_of(i * t, t)
        w0 = pl.multiple_of(jnp.clip(i - 1, 0, nb - 3) * t, t)
        tile = jnp.where(i == 0, 0, jnp.where(i == nb - 1, 2, 1))
        q = q_ref[pl.ds(r0, t), :]
        qs = jnp.concatenate([q[:, h * HEAD_DIM:(h + 1) * HEAD_DIM] for h in range(g)], axis=0)
        st = _dot_nt(k_ref[pl.ds(w0, 3 * t), :], qs) + bias_ref[tile]
        mx = jnp.maximum(jnp.max(st, axis=0, keepdims=True), sink)
        e = jnp.exp2(st - mx)
        l = jnp.sum(e, axis=0, keepdims=True) + jnp.exp2(sink - mx)
        p = (e * (1.0 / l)).astype(BF16)
        ot = lax.dot_general(v_ref[pl.ds(w0, 3 * t), :], p, (((0,), (0,)), ((), ())),
                             preferred_element_type=F32)
        for h in range(g):
            o_ref[pl.ds(r0, t), h * HEAD_DIM:(h + 1) * HEAD_DIM] = ot[:, h * t:(h + 1) * t].T.astype(BF16)
        return carry

    lax.fori_loop(0, nb, block, 0, unroll=2)


def _window_attention(qkv, bias_b, sink_logits, layer, n_seq, s):
    m = qkv.shape[0]
    g = B_HEADS // B_KV_HEADS
    gw = g * HEAD_DIM
    t = WINDOW
    nb = s // t
    grid_spec = pltpu.PrefetchScalarGridSpec(
        num_scalar_prefetch=1,
        grid=(n_seq, B_KV_HEADS),
        in_specs=[
            pl.BlockSpec((s, gw), lambda b, kv, sink: (b, QB_OFF // gw + kv)),
            pl.BlockSpec((s, HEAD_DIM), lambda b, kv, sink: (b, KB_OFF // HEAD_DIM + kv)),
            pl.BlockSpec((s, HEAD_DIM), lambda b, kv, sink: (b, VB_OFF // HEAD_DIM + kv)),
            pl.BlockSpec((3, None, 3 * t, g * t), lambda b, kv, sink: (0, kv, 0, 0)),
        ],
        out_specs=pl.BlockSpec((s, gw), lambda b, kv, sink: (b, kv)),
    )
    return pl.pallas_call(
        functools.partial(_window_body, nb=nb),
        out_shape=jax.ShapeDtypeStruct((m, BRANCH_WIDTH), BF16),
        grid_spec=grid_spec,
        compiler_params=_params(("parallel", "parallel")),
        name="window_attn",
    )(sink_logits[layer], qkv, qkv, qkv, bias_b)


def _gate_body(x_ref, ya_ref, yb_ref, yc_ref, wg_ref, wb_ref, o_ref, xb_ref):
    @pl.when(pl.program_id(1) == 0)
    def _():
        xb_ref[...] = x_ref[...].astype(BF16)

    xb = xb_ref[...]
    merged = None
    for n, y_ref in enumerate((ya_ref, yb_ref, yc_ref)):
        gate = jax.nn.sigmoid(_dot(xb, wg_ref[n]))
        term = gate * _dot(y_ref[...], wb_ref[n])
        merged = term if merged is None else merged + term
    o_ref[...] = merged.astype(BF16)


def _outproj_body(x_ref, m_ref, wo_ref, g_ref, b_ref, o_ref, *, n_split):
    rows = x_ref.shape[0] // n_split
    for r in range(n_split):
        sl = slice(r * rows, (r + 1) * rows)
        y = ALPHA * x_ref[sl, :] + _dot(m_ref[sl, :], wo_ref[...])
        o_ref[sl, :] = _layer_norm(y, g_ref[...], b_ref[...])


def _merge(x, ya, yb, yc, w_gate, w_branch, w_out, ln_g, ln_b, layer, *, tm=512, tc=512):
    m = x.shape[0]
    yspec = pl.BlockSpec((tm, BRANCH_WIDTH), lambda i, j: (i, 0))
    merged = pl.pallas_call(
        _gate_body,
        out_shape=jax.ShapeDtypeStruct((m, D_MODEL), BF16),
        grid=(m // tm, D_MODEL // tc),
        in_specs=[
            pl.BlockSpec((tm, D_MODEL), lambda i, j: (i, 0)),
            yspec, yspec, yspec,
            pl.BlockSpec((None, N_BRANCH, D_MODEL, tc), lambda i, j: (layer, 0, 0, j)),
            pl.BlockSpec((None, N_BRANCH, BRANCH_WIDTH, tc), lambda i, j: (layer, 0, 0, j)),
        ],
        out_specs=pl.BlockSpec((tm, tc), lambda i, j: (i, j)),
        scratch_shapes=[pltpu.VMEM((tm, D_MODEL), BF16)],
        compiler_params=_params(("parallel", "arbitrary")),
        name="gate_merge",
    )(x, ya, yb, yc, w_gate, w_branch)
    return pl.pallas_call(
        functools.partial(_outproj_body, n_split=2),
        out_shape=jax.ShapeDtypeStruct((m, D_MODEL), F32),
        grid=(m // tm,),
        in_specs=[
            pl.BlockSpec((tm, D_MODEL), lambda i: (i, 0)),
            pl.BlockSpec((tm, D_MODEL), lambda i: (i, 0)),
            pl.BlockSpec((None, D_MODEL, D_MODEL), lambda i: (layer, 0, 0), pipeline_mode=pl.Buffered(1)),
            pl.BlockSpec((None, None, 1, D_MODEL), lambda i: (layer, 1, 0, 0)),
            pl.BlockSpec((None, None, 1, D_MODEL), lambda i: (layer, 1, 0, 0)),
        ],
        out_specs=pl.BlockSpec((tm, D_MODEL), lambda i: (i, 0)),
        compiler_params=_params(("parallel",)),
        name="out_proj",
    )(x, merged, w_out, ln_g, ln_b)


def _memkv_body(m_ref, w_ref, o_ref):
    o_ref[...] = _dot(m_ref[...].astype(BF16), w_ref[...]).astype(BF16)


def _memkv(mem, w_kv, layer):
    rows = mem.shape[0]
    return pl.pallas_call(
        _memkv_body,
        out_shape=jax.ShapeDtypeStruct((rows, 2 * MEM_INNER), BF16),
        grid=(rows // MEM_TOKENS,),
        in_specs=[
            pl.BlockSpec((MEM_TOKENS, D_MODEL), lambda i: (i, 0)),
            pl.BlockSpec((None, D_MODEL, 2 * MEM_INNER), lambda i: (layer, 0, 0)),
        ],
        out_specs=pl.BlockSpec((MEM_TOKENS, 2 * MEM_INNER), lambda i: (i, 0)),
        compiler_params=_params(("parallel",)),
        name="memkv",
    )(mem, w_kv)


def _mem_body(x_ref, kv_ref, wq_ref, wo_ref, g_ref, b_ref, o_ref):
    x = x_ref[...]
    q = (_dot(x.astype(BF16), wq_ref[...]) * QSCALE).astype(BF16)
    kv = kv_ref[...]
    heads = []
    for h in range(MEM_HEADS):
        sc = _dot_nt(q[:, h * HEAD_DIM:(h + 1) * HEAD_DIM], kv[:, h * HEAD_DIM:(h + 1) * HEAD_DIM])
        e = jnp.exp2(sc - jnp.max(sc, axis=-1, keepdims=True))
        l = jnp.sum(e, axis=-1, keepdims=True)
        v = kv[:, MEM_INNER + h * HEAD_DIM:MEM_INNER + (h + 1) * HEAD_DIM]
        heads.append((_dot(e.astype(BF16), v) / l).astype(BF16))
    o = jnp.concatenate(heads, axis=1)
    y = ALPHA * x + _dot(o, wo_ref[...])
    o_ref[...] = _layer_norm(y, g_ref[...], b_ref[...])


def _mem_attention(x, kv, w_q, w_o, ln_g, ln_b, layer, s, *, tm=512):
    m = x.shape[0]
    per_seq = s // tm
    return pl.pallas_call(
        _mem_body,
        out_shape=jax.ShapeDtypeStruct((m, D_MODEL), F32),
        grid=(m // tm,),
        in_specs=[
            pl.BlockSpec((tm, D_MODEL), lambda i: (i, 0)),
            pl.BlockSpec((MEM_TOKENS, 2 * MEM_INNER), lambda i: (i // per_seq, 0)),
            pl.BlockSpec((None, D_MODEL, MEM_INNER), lambda i: (layer, 0, 0)),
            pl.BlockSpec((None, MEM_INNER, D_MODEL), lambda i: (layer, 0, 0)),
            pl.BlockSpec((None, None, 1, D_MODEL), lambda i: (layer, 2, 0, 0)),
            pl.BlockSpec((None, None, 1, D_MODEL), lambda i: (layer, 2, 0, 0)),
        ],
        out_specs=pl.BlockSpec((tm, D_MODEL), lambda i: (i, 0)),
        compiler_params=_params(("parallel",)),
        name="mem_attn",
    )(x, kv, w_q, w_o, ln_g, ln_b)


def _t5_bucket_np(rel):
    half = NUM_BUCKETS // 2
    max_exact = half // 2
    rel = np.asarray(rel, np.int32)
    ret = np.where(rel > 0, half, 0)
    n = np.abs(rel)
    ratio = np.log(np.maximum(n, 1).astype(np.float32) / np.float32(max_exact)) / np.float32(math.log(MAX_DISTANCE / max_exact))
    large = max_exact + (ratio * np.float32(half - max_exact)).astype(np.int32)
    large = np.minimum(large, half - 1)
    return (ret + np.where(n < max_exact, n, large)).astype(np.int32)


def _lookup_bias(table, buckets):
    n_heads = table.shape[1]
    b = jnp.asarray(buckets)[None]
    tab = table * LOG2E
    out = jnp.zeros((n_heads,) + buckets.shape, F32)
    for k in range(NUM_BUCKETS):
        out = jnp.where(b == k, tab[k].reshape((n_heads,) + (1,) * buckets.ndim), out)
    return out


def _diff_bias_tiles(table_a, t):
    ii = np.arange(t)[:, None]
    jj = np.arange(t)[None, :]
    buckets = np.stack([_t5_bucket_np(d * t + jj - ii) for d in (-2, -1, 0, 1, 2)])
    return _lookup_bias(table_a, buckets)


def _window_bias_tiles(table_b):
    g = B_HEADS // B_KV_HEADS
    rel = np.stack([(np.arange(3 * WINDOW) - p * WINDOW)[None, :] - np.arange(WINDOW)[:, None] for p in range(3)])
    bias = _lookup_bias(table_b, _t5_bucket_np(rel))
    bias = jnp.where(jnp.asarray(np.abs(rel) <= WINDOW)[None], bias, NEG_INF)
    bias = bias.reshape(B_KV_HEADS, g, 3, WINDOW, 3 * WINDOW)
    return jnp.transpose(bias, (2, 0, 4, 1, 3)).reshape(3, B_KV_HEADS, 3 * WINDOW, g * WINDOW)


def _rope_tables(s):
    n_rows = s // GRID_W
    row = np.repeat(np.arange(n_rows), GRID_W).astype(np.float32)
    col = np.tile(np.arange(GRID_W), n_rows).astype(np.float32)
    axis_dims = HEAD_DIM // 2
    inv = ROPE_THETA ** (-jnp.arange(0, axis_dims, 2, dtype=F32) / axis_dims)
    ang = jnp.concatenate([row[:, None] * inv[None, :], col[:, None] * inv[None, :]], axis=-1)
    cos, sin = jnp.cos(ang), jnp.sin(ang)
    return jnp.concatenate([cos, cos], axis=-1), jnp.concatenate([-sin, sin], axis=-1)


def _run_trunk(x, mem, wts, bias_a, bias_b):
    n_seq, s, _ = x.shape
    x = x.reshape(n_seq * s, D_MODEL)
    mem = mem.reshape(n_seq * MEM_TOKENS, D_MODEL)
    cos2, sin2 = _rope_tables(s)
    for i in range(DEPTH):
        lambda_init = 0.8 - 0.6 * math.exp(-0.3 * i)
        x = _ffn(x, wts["w_ffn_gu"], wts["w_ffn_d"], wts["ln_g"], wts["ln_b"], i, 0, 0)
        qkv = _qkv(x, wts["w_qkv"], wts["qk_norm_g"], cos2, sin2, i)
        ya = _diff_attention(qkv, bias_a, wts["lambda_qk"], wts["subln_g"], i, n_seq, s, lambda_init)
        yb = _window_attention(qkv, bias_b, wts["sink_logits"], i, n_seq, s)
        yc = _axial_attention(qkv, n_seq, s)
        x = _merge(x, ya, yb, yc, wts["w_gate"], wts["w_branch"], wts["w_out"], wts["ln_g"], wts["ln_b"], i)
        kv = _memkv(mem, wts["w_mem_kv"], i)
        x = _mem_attention(x, kv, wts["w_mem_q"], wts["w_mem_o"], wts["ln_g"], wts["ln_b"], i, s)
        x = _ffn(x, wts["w_ffn_gu"], wts["w_ffn_d"], wts["ln_g"], wts["ln_b"], i, 1, 3)
    return x.reshape(n_seq, s, D_MODEL)


def kernel(x_prompt, x_sample, mem_prompt, mem_sample, rel_bias_table, w_in, w_branch, w_out, lambda_qk, subln_g, sink_logits, qk_norm_g, w_mem_q, w_mem_kv, w_mem_o, w_ffn_gu, w_ffn_d, ln_g, ln_b):
    wts = {
        "w_qkv": w_in[:, :, :QKV_WIDTH].astype(BF16),
        "w_gate": jnp.transpose(w_in[:, :, QKV_WIDTH:].reshape(DEPTH, D_MODEL, N_BRANCH, D_MODEL), (0, 2, 1, 3)).astype(BF16),
        "w_branch": w_branch.astype(BF16),
        "w_out": w_out.astype(BF16),
        "w_mem_q": w_mem_q.astype(BF16),
        "w_mem_kv": w_mem_kv.astype(BF16),
        "w_mem_o": w_mem_o.astype(BF16),
        "w_ffn_gu": w_ffn_gu.astype(BF16),
        "w_ffn_d": w_ffn_d.astype(BF16),
        "lambda_qk": lambda_qk,
        "subln_g": subln_g.reshape(DEPTH, 1, A_V),
        "sink_logits": sink_logits,
        "qk_norm_g": qk_norm_g,
        "ln_g": ln_g.reshape(DEPTH, 4, 1, D_MODEL),
        "ln_b": ln_b.reshape(DEPTH, 4, 1, D_MODEL),
    }
    bias_a = _diff_bias_tiles(rel_bias_table[:, :A_HEADS], 256)
    bias_b = _window_bias_tiles(rel_bias_table[:, A_HEADS:])
    y_prompt = _run_trunk(x_prompt, mem_prompt, wts, bias_a, bias_b)
    y_sample = _run_trunk(x_sample, mem_sample, wts, bias_a, bias_b)
    return (y_prompt, y_sample)
```

```python
import functools
import math

import numpy as np
import jax
import jax.numpy as jnp
from jax import lax
from jax.experimental import pallas as pl
from jax.experimental.pallas import tpu as pltpu

D_MODEL = 2048
DEPTH = 4
HEAD_DIM = 128
A_HEADS = 4
A_V = 2 * HEAD_DIM
B_HEADS = 8
B_KV_HEADS = 2
WINDOW = 128
C_HEADS = 8
C_KV_HEADS = 2
GRID_W = 64
ROPE_THETA = 10000.0
NUM_BUCKETS = 32
MAX_DISTANCE = 128
MEM_TOKENS = 256
MEM_HEADS = 4
MEM_INNER = MEM_HEADS * HEAD_DIM
D_FF = 5632
N_BRANCH = 3
BRANCH_WIDTH = 1024
QKV_WIDTH = 6144
ALPHA = (2 * DEPTH) ** 0.25
LN_EPS = 1e-5
RMS_EPS = 1e-6
NEG_INF = -1e30
LOG2E = math.log2(math.e)
QSCALE = HEAD_DIM ** -0.5 * LOG2E

QA_OFF, KA_OFF, VA_OFF = 0, 1024, 2048
QB_OFF, KB_OFF, VB_OFF = 3072, 4096, 4352
QC_OFF, KC_OFF, VC_OFF = 4608, 5632, 5888

V7X_VMEM_BYTES = 64 * 1024 * 1024
VMEM_LIMIT = 56 * 1024 * 1024
F32 = jnp.float32
BF16 = jnp.bfloat16


def _params(semantics, vmem_limit=VMEM_LIMIT):
    return pltpu.CompilerParams(dimension_semantics=semantics, vmem_limit_bytes=vmem_limit)


def _layer_norm(y, g, b):
    mu = jnp.mean(y, axis=-1, keepdims=True)
    d = y - mu
    var = jnp.mean(d * d, axis=-1, keepdims=True)
    return d * lax.rsqrt(var + LN_EPS) * g + b


def _dot(a, b):
    return jnp.dot(a, b, preferred_element_type=F32)


def _dot_nt(a, b):
    return lax.dot_general(a, b, (((1,), (1,)), ((), ())), preferred_element_type=F32)


def _lane_fold(x, op):
    parts = [x[:, j * 128:(j + 1) * 128] for j in range(x.shape[1] // 128)]
    return functools.reduce(op, parts)


def _exp2_minus(x, col):
    parts = [jnp.exp2(x[:, j * 128:(j + 1) * 128] - col) for j in range(x.shape[1] // 128)]
    return parts[0] if len(parts) == 1 else jnp.concatenate(parts, axis=1)


def _lane_bcast(col):
    return jnp.broadcast_to(col, (col.shape[0], 128))


def _ffn_body(x_ref, wg_ref, wu_ref, wd_ref, g_ref, b_ref, o_ref, xb_ref, acc_ref, *, n_f):
    f = pl.program_id(1)

    @pl.when(f == 0)
    def _():
        xb_ref[...] = x_ref[...].astype(BF16)
        acc_ref[...] = jnp.zeros_like(acc_ref)

    xb = xb_ref[...]
    h = _dot(xb, wg_ref[...])
    u = _dot(xb, wu_ref[...])
    a = (h * jax.nn.sigmoid(h) * u).astype(BF16)
    acc_ref[...] += _dot(a, wd_ref[...])

    @pl.when(f == n_f - 1)
    def _():
        y = ALPHA * x_ref[...] + 0.5 * acc_ref[...]
        o_ref[...] = _layer_norm(y, g_ref[...], b_ref[...])


def _ffn(x, w_gu, w_d, ln_g, ln_b, layer, which, ln_idx, *, tm=512, tf=512):
    m = x.shape[0]
    n_f = D_FF // tf
    return pl.pallas_call(
        functools.partial(_ffn_body, n_f=n_f),
        out_shape=jax.ShapeDtypeStruct((m, D_MODEL), F32),
        grid=(m // tm, n_f),
        in_specs=[
            pl.BlockSpec((tm, D_MODEL), lambda i, f: (i, 0)),
            pl.BlockSpec((None, None, D_MODEL, tf), lambda i, f: (layer, which, 0, f)),
            pl.BlockSpec((None, None, D_MODEL, tf), lambda i, f: (layer, which, 0, f + n_f)),
            pl.BlockSpec((None, None, tf, D_MODEL), lambda i, f: (layer, which, f, 0)),
            pl.BlockSpec((None, None, 1, D_MODEL), lambda i, f: (layer, ln_idx, 0, 0)),
            pl.BlockSpec((None, None, 1, D_MODEL), lambda i, f: (layer, ln_idx, 0, 0)),
        ],
        out_specs=pl.BlockSpec((tm, D_MODEL), lambda i, f: (i, 0)),
        scratch_shapes=[pltpu.VMEM((tm, D_MODEL), BF16), pltpu.VMEM((tm, D_MODEL), F32)],
        compiler_params=_params(("parallel", "arbitrary")),
        name="ffn",
    )(x, w_gu, w_gu, w_d, ln_g, ln_b)


def _qkv_body(x_ref, w_ref, gn_ref, cos_ref, sin_ref, o_ref, *, tn):
    xb = x_ref[...].astype(BF16)
    cos = cos_ref[...]
    sin = sin_ref[...]
    starts = [j * tn for j in range(QKV_WIDTH // tn)]
    for c0 in sorted(starts, key=lambda c: not (QC_OFF <= c < VC_OFF)):
        acc = _dot(xb, w_ref[:, c0:c0 + tn])
        is_q = (QA_OFF <= c0 < KA_OFF) or (QB_OFF <= c0 < KB_OFF) or (QC_OFF <= c0 < KC_OFF)
        if QC_OFF <= c0 < VC_OFF:
            g = gn_ref[0:1, :] if c0 < KC_OFF else gn_ref[1:2, :]
            for hh in range(tn // HEAD_DIM):
                a = acc[:, hh * HEAD_DIM:(hh + 1) * HEAD_DIM]
                r = lax.rsqrt(jnp.mean(a * a, axis=-1, keepdims=True) + RMS_EPS)
                y = a * r * g
                y = y * cos + pltpu.roll(y, HEAD_DIM // 2, 1) * sin
                if is_q:
                    y = y * QSCALE
                o_ref[:, c0 + hh * HEAD_DIM:c0 + (hh + 1) * HEAD_DIM] = y.astype(BF16)
        else:
            if is_q:
                acc = acc * QSCALE
            o_ref[:, c0:c0 + tn] = acc.astype(BF16)


def _qkv(x, w_qkv, qk_norm_g, cos2, sin2, layer, *, tm=512, tn=256):
    m = x.shape[0]
    s = cos2.shape[0]
    n_pos = s // tm
    return pl.pallas_call(
        functools.partial(_qkv_body, tn=tn),
        out_shape=jax.ShapeDtypeStruct((m, QKV_WIDTH), BF16),
        grid=(m // tm,),
        in_specs=[
            pl.BlockSpec((tm, D_MODEL), lambda i: (i, 0)),
            pl.BlockSpec((None, D_MODEL, QKV_WIDTH), lambda i: (layer, 0, 0), pipeline_mode=pl.Buffered(1)),
            pl.BlockSpec((None, 2, HEAD_DIM), lambda i: (layer, 0, 0)),
            pl.BlockSpec((tm, HEAD_DIM), lambda i: (i % n_pos, 0)),
            pl.BlockSpec((tm, HEAD_DIM), lambda i: (i % n_pos, 0)),
        ],
        out_specs=pl.BlockSpec((tm, QKV_WIDTH), lambda i: (i, 0)),
        compiler_params=_params(("parallel",)),
        name="qkv",
    )(x, w_qkv, qk_norm_g, cos2, sin2)


def _diff_body(q_ref, k_ref, v_ref, bias_ref, lam_ref, sg_ref, o_ref, s1_ref, s2_ref, m_ref, *, t, nc, nq, n_items, lambda_init):
    g = pl.program_id(0)
    i = jnp.minimum(g, n_items - 1) % nq

    @pl.when(g == 0)
    def _():
        s1_ref[1] = jnp.zeros(s1_ref.shape[1:], F32)
        s2_ref[1] = jnp.zeros(s2_ref.shape[1:], F32)
        m_ref[1] = jnp.zeros(m_ref.shape[1:], F32)

    def step(cur, prv):
        q = q_ref[...]
        q1 = q[:, :HEAD_DIM]
        q2 = q[:, HEAD_DIM:]
        m1p = m_ref[prv, 0]
        m2p = m_ref[prv, 1]
        m1 = m2 = acc = l1 = l2 = None
        for c in range(nc):
            kc = k_ref[c * t:(c + 1) * t, :]
            b = bias_ref[jnp.clip(c - i, -2, 2) + 2]
            s1 = _dot_nt(q1, kc[:, :HEAD_DIM]) + b
            s2 = _dot_nt(q2, kc[:, HEAD_DIM:]) + b
            s1_ref[cur, c] = s1
            s2_ref[cur, c] = s2
            f1 = _lane_fold(s1, jnp.maximum)
            f2 = _lane_fold(s2, jnp.maximum)
            m1 = f1 if m1 is None else jnp.maximum(m1, f1)
            m2 = f2 if m2 is None else jnp.maximum(m2, f2)

            e1 = _exp2_minus(s1_ref[prv, c], m1p)
            e2 = _exp2_minus(s2_ref[prv, c], m2p)
            e = jnp.concatenate([e1.astype(BF16), e2.astype(BF16)], axis=0)
            d = _dot(e, v_ref[c * t:(c + 1) * t, :])
            f1 = _lane_fold(e1, jnp.add)
            f2 = _lane_fold(e2, jnp.add)
            acc = d if acc is None else acc + d
            l1 = f1 if l1 is None else l1 + f1
            l2 = f2 if l2 is None else l2 + f2
        m_ref[cur, 0] = _lane_bcast(jnp.max(m1, axis=-1, keepdims=True))
        m_ref[cur, 1] = _lane_bcast(jnp.max(m2, axis=-1, keepdims=True))

        l1 = jnp.sum(l1, axis=-1, keepdims=True)
        l2 = jnp.sum(l2, axis=-1, keepdims=True)
        lam = lam_ref[...]
        lam_full = (jnp.exp(jnp.sum(lam[0:1] * lam[1:2], axis=-1, keepdims=True))
                    - jnp.exp(jnp.sum(lam[2:3] * lam[3:4], axis=-1, keepdims=True)) + lambda_init)
        o = acc[0:t, :] / l1 - lam_full * (acc[t:2 * t, :] / l2)
        o = o * lax.rsqrt(jnp.mean(o * o, axis=-1, keepdims=True) + RMS_EPS) * sg_ref[...]
        o_ref[...] = (o * (1.0 - lambda_init)).astype(BF16)

    @pl.when(g % 2 == 0)
    def _():
        step(0, 1)

    @pl.when(g % 2 == 1)
    def _():
        step(1, 0)


def _pipeline_items(n_items, n_heads, nq):
    def scored(g):
        it = jnp.minimum(g, n_items - 1)
        return it // (n_heads * nq), (it // nq) % n_heads, it % nq

    def finished(g):
        it = jnp.maximum(g - 1, 0)
        return it // (n_heads * nq), (it // nq) % n_heads, it % nq

    return scored, finished


def _diff_attention(qkv, bias_a, lambda_qk, subln_g, layer, n_seq, s, lambda_init, *, t=256):
    m = qkv.shape[0]
    nc = s // t
    nq = s // t
    n_items = n_seq * A_HEADS * nq
    scored, finished = _pipeline_items(n_items, A_HEADS, nq)

    def q_map(g):
        b, h, i = scored(g)
        return (b * nq + i, QA_OFF // A_V + h)

    def k_map(g):
        b, h, _ = scored(g)
        return (b, KA_OFF // A_V + h)

    def v_map(g):
        b, h, _ = finished(g)
        return (b, VA_OFF // A_V + h)

    def o_map(g):
        b, h, i = finished(g)
        return (b * nq + i, h)

    return pl.pallas_call(
        functools.partial(_diff_body, t=t, nc=nc, nq=nq, n_items=n_items, lambda_init=lambda_init),
        out_shape=jax.ShapeDtypeStruct((m, BRANCH_WIDTH), BF16),
        grid=(n_items + 1,),
        in_specs=[
            pl.BlockSpec((t, A_V), q_map),
            pl.BlockSpec((s, A_V), k_map),
            pl.BlockSpec((s, A_V), v_map),
            pl.BlockSpec((None, 5, t, t), lambda g: (scored(g)[1], 0, 0, 0)),
            pl.BlockSpec((None, 4, HEAD_DIM), lambda g: (layer, 0, 0)),
            pl.BlockSpec((None, 1, A_V), lambda g: (layer, 0, 0)),
        ],
        out_specs=pl.BlockSpec((t, A_V), o_map),
        scratch_shapes=[pltpu.VMEM((2, nc, t, t), F32), pltpu.VMEM((2, nc, t, t), F32), pltpu.VMEM((2, 2, t, 128), F32)],
        compiler_params=_params(("arbitrary",)),
        name="diff_attn",
    )(qkv, qkv, qkv, bias_a, lambda_qk, subln_g)


def _axial_body(q_ref, k_ref, v_ref, o_ref, s_ref, m_ref, *, t, tk, nc):
    gq = C_HEADS // C_KV_HEADS
    g = pl.program_id(0)
    half = gq // 2

    @pl.when(g == 0)
    def _():
        s_ref[1] = jnp.zeros(s_ref.shape[1:], F32)
        m_ref[1] = jnp.zeros(m_ref.shape[1:], F32)

    def step(cur, prv):
        q = q_ref[...]
        qs = [jnp.concatenate([q[:, h * HEAD_DIM:(h + 1) * HEAD_DIM] for h in range(r * half, (r + 1) * half)], axis=0)
              for r in range(2)]
        mp = [m_ref[prv, r] for r in range(2)]
        m = [None, None]
        acc = [None, None]
        l = [None, None]
        for c in range(nc):
            kc = k_ref[c * tk:(c + 1) * tk, :]
            vc = v_ref[c * tk:(c + 1) * tk, :]
            for r in range(2):
                sc = _dot_nt(qs[r], kc)
                s_ref[cur, r, c] = sc
                f = _lane_fold(sc, jnp.maximum)
                m[r] = f if m[r] is None else jnp.maximum(m[r], f)

                e = _exp2_minus(s_ref[prv, r, c], mp[r])
                d = _dot(e.astype(BF16), vc)
                f = _lane_fold(e, jnp.add)
                acc[r] = d if acc[r] is None else acc[r] + d
                l[r] = f if l[r] is None else l[r] + f
        for r in range(2):
            m_ref[cur, r] = _lane_bcast(jnp.max(m[r], axis=-1, keepdims=True))
            o = acc[r] / jnp.sum(l[r], axis=-1, keepdims=True)
            for hh in range(half):
                h = r * half + hh
                o_ref[:, h * HEAD_DIM:(h + 1) * HEAD_DIM] = o[hh * t:(hh + 1) * t, :].astype(BF16)

    @pl.when(g % 2 == 0)
    def _():
        step(0, 1)

    @pl.when(g % 2 == 1)
    def _():
        step(1, 0)


def _axial_attention(qkv, n_seq, s, *, t=256, tk=256):
    m = qkv.shape[0]
    gq = C_HEADS // C_KV_HEADS
    gw = gq * HEAD_DIM
    nq = s // t
    nc = s // tk
    n_items = n_seq * C_KV_HEADS * nq
    scored, finished = _pipeline_items(n_items, C_KV_HEADS, nq)

    def q_map(g):
        b, kv, i = scored(g)
        return (b * nq + i, QC_OFF // gw + kv)

    def k_map(g):
        b, kv, _ = scored(g)
        return (b, KC_OFF // HEAD_DIM + kv)

    def v_map(g):
        b, kv, _ = finished(g)
        return (b, VC_OFF // HEAD_DIM + kv)

    def o_map(g):
        b, kv, i = finished(g)
        return (b * nq + i, kv)

    return pl.pallas_call(
        functools.partial(_axial_body, t=t, tk=tk, nc=nc),
        out_shape=jax.ShapeDtypeStruct((m, BRANCH_WIDTH), BF16),
        grid=(n_items + 1,),
        in_specs=[
            pl.BlockSpec((t, gw), q_map),
            pl.BlockSpec((s, HEAD_DIM), k_map),
            pl.BlockSpec((s, HEAD_DIM), v_map),
        ],
        out_specs=pl.BlockSpec((t, gw), o_map),
        scratch_shapes=[pltpu.VMEM((2, 2, nc, gq // 2 * t, tk), F32), pltpu.VMEM((2, 2, gq // 2 * t, 128), F32)],
        compiler_params=_params(("arbitrary",)),
        name="axial_attn",
    )(qkv, qkv, qkv)


def _window_body(sink_ref, q_ref, k_ref, v_ref, bias_ref, o_ref, *, nb):
    g = B_HEADS // B_KV_HEADS
    t = WINDOW
    kv = pl.program_id(1)
    sink = jnp.concatenate([jnp.full((1, t), sink_ref[kv * g + h] * LOG2E, F32) for h in range(g)], axis=1)

    def block(i, carry):
        r0 = pl.multiple_of(i * t, t)
        w0 = pl.multiple_of(jnp.clip(i - 1, 0, nb - 3) * t, t)
        tile = jnp.where(i == 0, 0, jnp.where(i == nb - 1, 2, 1))
        q = q_ref[pl.ds(r0, t), :]
        qs = jnp.concatenate([q[:, h * HEAD_DIM:(h + 1) * HEAD_DIM] for h in range(g)], axis=0)
        st = _dot_nt(k_ref[pl.ds(w0, 3 * t), :], qs) + bias_ref[tile]
        mx = jnp.maximum(jnp.max(st, axis=0, keepdims=True), sink)
        e = jnp.exp2(st - mx)
        l = jnp.sum(e, axis=0, keepdims=True) + jnp.exp2(sink - mx)
        p = (e * (1.0 / l)).astype(BF16)
        ot = lax.dot_general(v_ref[pl.ds(w0, 3 * t), :], p, (((0,), (0,)), ((), ())),
                             preferred_element_type=F32)
        for h in range(g):
            o_ref[pl.ds(r0, t), h * HEAD_DIM:(h + 1) * HEAD_DIM] = ot[:, h * t:(h + 1) * t].T.astype(BF16)
        return carry

    lax.fori_loop(0, nb, block, 0, unroll=4)


def _window_attention(qkv, bias_b, sink_logits, layer, n_seq, s):
    m = qkv.shape[0]
    g = B_HEADS // B_KV_HEADS
    gw = g * HEAD_DIM
    t = WINDOW
    nb = s // t
    grid_spec = pltpu.PrefetchScalarGridSpec(
        num_scalar_prefetch=1,
        grid=(n_seq, B_KV_HEADS),
        in_specs=[
            pl.BlockSpec((s, gw), lambda b, kv, sink: (b, QB_OFF // gw + kv)),
            pl.BlockSpec((s, HEAD_DIM), lambda b, kv, sink: (b, KB_OFF // HEAD_DIM + kv)),
            pl.BlockSpec((s, HEAD_DIM), lambda b, kv, sink: (b, VB_OFF // HEAD_DIM + kv)),
            pl.BlockSpec((3, None, 3 * t, g * t), lambda b, kv, sink: (0, kv, 0, 0)),
        ],
        out_specs=pl.BlockSpec((s, gw), lambda b, kv, sink: (b, kv)),
    )
    return pl.pallas_call(
        functools.partial(_window_body, nb=nb),
        out_shape=jax.ShapeDtypeStruct((m, BRANCH_WIDTH), BF16),
        grid_spec=grid_spec,
        compiler_params=_params(("parallel", "parallel")),
        name="window_attn",
    )(sink_logits[layer], qkv, qkv, qkv, bias_b)


def _gate_body(x_ref, ya_ref, yb_ref, yc_ref, wg_ref, wb_ref, o_ref, xb_ref):
    @pl.when(pl.program_id(1) == 0)
    def _():
        xb_ref[...] = x_ref[...].astype(BF16)

    xb = xb_ref[...]
    merged = None
    for n, y_ref in enumerate((ya_ref, yb_ref, yc_ref)):
        gate = jax.nn.sigmoid(_dot(xb, wg_ref[n]))
        term = gate * _dot(y_ref[...], wb_ref[n])
        merged = term if merged is None else merged + term
    o_ref[...] = merged.astype(BF16)


def _outproj_body(x_ref, m_ref, wo_ref, g_ref, b_ref, o_ref, *, n_split):
    rows = x_ref.shape[0] // n_split
    for r in range(n_split):
        sl = slice(r * rows, (r + 1) * rows)
        y = ALPHA * x_ref[sl, :] + _dot(m_ref[sl, :], wo_ref[...])
        o_ref[sl, :] = _layer_norm(y, g_ref[...], b_ref[...])


def _merge(x, ya, yb, yc, w_gate, w_branch, w_out, ln_g, ln_b, layer, *, tm=512, tc=512):
    m = x.shape[0]
    yspec = pl.BlockSpec((tm, BRANCH_WIDTH), lambda i, j: (i, 0))
    merged = pl.pallas_call(
        _gate_body,
        out_shape=jax.ShapeDtypeStruct((m, D_MODEL), BF16),
        grid=(m // tm, D_MODEL // tc),
        in_specs=[
            pl.BlockSpec((tm, D_MODEL), lambda i, j: (i, 0)),
            yspec, yspec, yspec,
            pl.BlockSpec((None, N_BRANCH, D_MODEL, tc), lambda i, j: (layer, 0, 0, j)),
            pl.BlockSpec((None, N_BRANCH, BRANCH_WIDTH, tc), lambda i, j: (layer, 0, 0, j)),
        ],
        out_specs=pl.BlockSpec((tm, tc), lambda i, j: (i, j)),
        scratch_shapes=[pltpu.VMEM((tm, D_MODEL), BF16)],
        compiler_params=_params(("parallel", "arbitrary")),
        name="gate_merge",
    )(x, ya, yb, yc, w_gate, w_branch)
    return pl.pallas_call(
        functools.partial(_outproj_body, n_split=2),
        out_shape=jax.ShapeDtypeStruct((m, D_MODEL), F32),
        grid=(m // tm,),
        in_specs=[
            pl.BlockSpec((tm, D_MODEL), lambda i: (i, 0)),
            pl.BlockSpec((tm, D_MODEL), lambda i: (i, 0)),
            pl.BlockSpec((None, D_MODEL, D_MODEL), lambda i: (layer, 0, 0), pipeline_mode=pl.Buffered(1)),
            pl.BlockSpec((None, None, 1, D_MODEL), lambda i: (layer, 1, 0, 0)),
            pl.BlockSpec((None, None, 1, D_MODEL), lambda i: (layer, 1, 0, 0)),
        ],
        out_specs=pl.BlockSpec((tm, D_MODEL), lambda i: (i, 0)),
        compiler_params=_params(("parallel",)),
        name="out_proj",
    )(x, merged, w_out, ln_g, ln_b)


def _memkv_body(m_ref, w_ref, o_ref):
    o_ref[...] = _dot(m_ref[...].astype(BF16), w_ref[...]).astype(BF16)


def _memkv(mem, w_kv, layer):
    rows = mem.shape[0]
    return pl.pallas_call(
        _memkv_body,
        out_shape=jax.ShapeDtypeStruct((rows, 2 * MEM_INNER), BF16),
        grid=(rows // MEM_TOKENS,),
        in_specs=[
            pl.BlockSpec((MEM_TOKENS, D_MODEL), lambda i: (i, 0)),
            pl.BlockSpec((None, D_MODEL, 2 * MEM_INNER), lambda i: (layer, 0, 0)),
        ],
        out_specs=pl.BlockSpec((MEM_TOKENS, 2 * MEM_INNER), lambda i: (i, 0)),
        compiler_params=_params(("parallel",)),
        name="memkv",
    )(mem, w_kv)


def _mem_body(x_ref, kv_ref, wq_ref, wo_ref, g_ref, b_ref, o_ref, *, n_split):
    kv = kv_ref[...]
    rows = x_ref.shape[0] // n_split
    for r in range(n_split):
        sl = slice(r * rows, (r + 1) * rows)
        x = x_ref[sl, :]
        q = (_dot(x.astype(BF16), wq_ref[...]) * QSCALE).astype(BF16)
        heads = []
        for h in range(MEM_HEADS):
            sc = _dot_nt(q[:, h * HEAD_DIM:(h + 1) * HEAD_DIM], kv[:, h * HEAD_DIM:(h + 1) * HEAD_DIM])
            e = jnp.exp2(sc - jnp.max(sc, axis=-1, keepdims=True))
            l = jnp.sum(e, axis=-1, keepdims=True)
            v = kv[:, MEM_INNER + h * HEAD_DIM:MEM_INNER + (h + 1) * HEAD_DIM]
            heads.append((_dot(e.astype(BF16), v) / l).astype(BF16))
        o = jnp.concatenate(heads, axis=1)
        y = ALPHA * x + _dot(o, wo_ref[...])
        o_ref[sl, :] = _layer_norm(y, g_ref[...], b_ref[...])


def _mem_attention(x, kv, w_q, w_o, ln_g, ln_b, layer, s, *, tm=512):
    m = x.shape[0]
    per_seq = s // tm
    return pl.pallas_call(
        functools.partial(_mem_body, n_split=1),
        out_shape=jax.ShapeDtypeStruct((m, D_MODEL), F32),
        grid=(m // tm,),
        in_specs=[
            pl.BlockSpec((tm, D_MODEL), lambda i: (i, 0)),
            pl.BlockSpec((MEM_TOKENS, 2 * MEM_INNER), lambda i: (i // per_seq, 0)),
            pl.BlockSpec((None, D_MODEL, MEM_INNER), lambda i: (layer, 0, 0)),
            pl.BlockSpec((None, MEM_INNER, D_MODEL), lambda i: (layer, 0, 0)),
            pl.BlockSpec((None, None, 1, D_MODEL), lambda i: (layer, 2, 0, 0)),
            pl.BlockSpec((None, None, 1, D_MODEL), lambda i: (layer, 2, 0, 0)),
        ],
        out_specs=pl.BlockSpec((tm, D_MODEL), lambda i: (i, 0)),
        compiler_params=_params(("parallel",)),
        name="mem_attn",
    )(x, kv, w_q, w_o, ln_g, ln_b)


def _t5_bucket_np(rel):
    half = NUM_BUCKETS // 2
    max_exact = half // 2
    rel = np.asarray(rel, np.int32)
    ret = np.where(rel > 0, half, 0)
    n = np.abs(rel)
    ratio = np.log(np.maximum(n, 1).astype(np.float32) / np.float32(max_exact)) / np.float32(math.log(MAX_DISTANCE / max_exact))
    large = max_exact + (ratio * np.float32(half - max_exact)).astype(np.int32)
    large = np.minimum(large, half - 1)
    return (ret + np.where(n < max_exact, n, large)).astype(np.int32)


def _lookup_bias(table, buckets):
    n_heads = table.shape[1]
    b = jnp.asarray(buckets)[None]
    tab = table * LOG2E
    out = jnp.zeros((n_heads,) + buckets.shape, F32)
    for k in range(NUM_BUCKETS):
        out = jnp.where(b == k, tab[k].reshape((n_heads,) + (1,) * buckets.ndim), out)
    return out


def _diff_bias_tiles(table_a, t):
    ii = np.arange(t)[:, None]
    jj = np.arange(t)[None, :]
    buckets = np.stack([_t5_bucket_np(d * t + jj - ii) for d in (-2, -1, 0, 1, 2)])
    return _lookup_bias(table_a, buckets)


def _window_bias_tiles(table_b):
    g = B_HEADS // B_KV_HEADS
    rel = np.stack([(np.arange(3 * WINDOW) - p * WINDOW)[None, :] - np.arange(WINDOW)[:, None] for p in range(3)])
    bias = _lookup_bias(table_b, _t5_bucket_np(rel))
    bias = jnp.where(jnp.asarray(np.abs(rel) <= WINDOW)[None], bias, NEG_INF)
    bias = bias.reshape(B_KV_HEADS, g, 3, WINDOW, 3 * WINDOW)
    return jnp.transpose(bias, (2, 0, 4, 1, 3)).reshape(3, B_KV_HEADS, 3 * WINDOW, g * WINDOW)


def _rope_tables(s):
    n_rows = s // GRID_W
    row = np.repeat(np.arange(n_rows), GRID_W).astype(np.float32)
    col = np.tile(np.arange(GRID_W), n_rows).astype(np.float32)
    axis_dims = HEAD_DIM // 2
    inv = ROPE_THETA ** (-jnp.arange(0, axis_dims, 2, dtype=F32) / axis_dims)
    ang = jnp.concatenate([row[:, None] * inv[None, :], col[:, None] * inv[None, :]], axis=-1)
    cos, sin = jnp.cos(ang), jnp.sin(ang)
    return jnp.concatenate([cos, cos], axis=-1), jnp.concatenate([-sin, sin], axis=-1)


def _run_trunk(x, mem, wts, bias_a, bias_b):
    n_seq, s, _ = x.shape
    x = x.reshape(n_seq * s, D_MODEL)
    mem = mem.reshape(n_seq * MEM_TOKENS, D_MODEL)
    cos2, sin2 = _rope_tables(s)
    for i in range(DEPTH):
        lambda_init = 0.8 - 0.6 * math.exp(-0.3 * i)
        x = _ffn(x, wts["w_ffn_gu"], wts["w_ffn_d"], wts["ln_g"], wts["ln_b"], i, 0, 0)
        qkv = _qkv(x, wts["w_qkv"], wts["qk_norm_g"], cos2, sin2, i)
        ya = _diff_attention(qkv, bias_a, wts["lambda_qk"], wts["subln_g"], i, n_seq, s, lambda_init)
        yb = _window_attention(qkv, bias_b, wts["sink_logits"], i, n_seq, s)
        yc = _axial_attention(qkv, n_seq, s)
        x = _merge(x, ya, yb, yc, wts["w_gate"], wts["w_branch"], wts["w_out"], wts["ln_g"], wts["ln_b"], i)
        kv = _memkv(mem, wts["w_mem_kv"], i)
        x = _mem_attention(x, kv, wts["w_mem_q"], wts["w_mem_o"], wts["ln_g"], wts["ln_b"], i, s)
        x = _ffn(x, wts["w_ffn_gu"], wts["w_ffn_d"], wts["ln_g"], wts["ln_b"], i, 1, 3)
    return x.reshape(n_seq, s, D_MODEL)


def kernel(x_prompt, x_sample, mem_prompt, mem_sample, rel_bias_table, w_in, w_branch, w_out, lambda_qk, subln_g, sink_logits, qk_norm_g, w_mem_q, w_mem_kv, w_mem_o, w_ffn_gu, w_ffn_d, ln_g, ln_b):
    wts = {
        "w_qkv": w_in[:, :, :QKV_WIDTH].astype(BF16),
        "w_gate": jnp.transpose(w_in[:, :, QKV_WIDTH:].reshape(DEPTH, D_MODEL, N_BRANCH, D_MODEL), (0, 2, 1, 3)).astype(BF16),
        "w_branch": w_branch.astype(BF16),
        "w_out": w_out.astype(BF16),
        "w_mem_q": w_mem_q.astype(BF16),
        "w_mem_kv": w_mem_kv.astype(BF16),
        "w_mem_o": w_mem_o.astype(BF16),
        "w_ffn_gu": w_ffn_gu.astype(BF16),
        "w_ffn_d": w_ffn_d.astype(BF16),
        "lambda_qk": lambda_qk,
        "subln_g": subln_g.reshape(DEPTH, 1, A_V),
        "sink_logits": sink_logits,
        "qk_norm_g": qk_norm_g,
        "ln_g": ln_g.reshape(DEPTH, 4, 1, D_MODEL),
        "ln_b": ln_b.reshape(DEPTH, 4, 1, D_MODEL),
    }
    bias_a = _diff_bias_tiles(rel_bias_table[:, :A_HEADS], 256)
    bias_b = _window_bias_tiles(rel_bias_table[:, A_HEADS:])
    y_prompt = _run_trunk(x_prompt, mem_prompt, wts, bias_a, bias_b)
    y_sample = _run_trunk(x_sample, mem_sample, wts, bias_a, bias_b)
    return (y_prompt, y_sample)
```

```python
import functools
import math

import numpy as np
import jax
import jax.numpy as jnp
from jax import lax
from jax.experimental import pallas as pl
from jax.experimental.pallas import tpu as pltpu

D_MODEL = 2048
DEPTH = 4
HEAD_DIM = 128
A_HEADS = 4
A_V = 2 * HEAD_DIM
B_HEADS = 8
B_KV_HEADS = 2
WINDOW = 128
C_HEADS = 8
C_KV_HEADS = 2
GRID_W = 64
ROPE_THETA = 10000.0
NUM_BUCKETS = 32
MAX_DISTANCE = 128
MEM_TOKENS = 256
MEM_HEADS = 4
MEM_INNER = MEM_HEADS * HEAD_DIM
D_FF = 5632
N_BRANCH = 3
BRANCH_WIDTH = 1024
QKV_WIDTH = 6144
ALPHA = (2 * DEPTH) ** 0.25
LN_EPS = 1e-5
RMS_EPS = 1e-6
NEG_INF = -1e30
LOG2E = math.log2(math.e)
QSCALE = HEAD_DIM ** -0.5 * LOG2E

QA_OFF, KA_OFF, VA_OFF = 0, 1024, 2048
QB_OFF, KB_OFF, VB_OFF = 3072, 4096, 4352
QC_OFF, KC_OFF, VC_OFF = 4608, 5632, 5888

V7X_VMEM_BYTES = 64 * 1024 * 1024
VMEM_LIMIT = 56 * 1024 * 1024
F32 = jnp.float32
BF16 = jnp.bfloat16


def _params(semantics, vmem_limit=VMEM_LIMIT):
    return pltpu.CompilerParams(dimension_semantics=semantics, vmem_limit_bytes=vmem_limit)


def _layer_norm(y, g, b):
    mu = jnp.mean(y, axis=-1, keepdims=True)
    d = y - mu
    var = jnp.mean(d * d, axis=-1, keepdims=True)
    return d * lax.rsqrt(var + LN_EPS) * g + b


def _dot(a, b):
    return jnp.dot(a, b, preferred_element_type=F32)


def _dot_nt(a, b):
    return lax.dot_general(a, b, (((1,), (1,)), ((), ())), preferred_element_type=F32)


def _lane_fold(x, op):
    parts = [x[:, j * 128:(j + 1) * 128] for j in range(x.shape[1] // 128)]
    return functools.reduce(op, parts)


def _exp2_minus(x, col):
    parts = [jnp.exp2(x[:, j * 128:(j + 1) * 128] - col) for j in range(x.shape[1] // 128)]
    return parts[0] if len(parts) == 1 else jnp.concatenate(parts, axis=1)


def _lane_bcast(col):
    return jnp.broadcast_to(col, (col.shape[0], 128))


def _ffn_body(x_ref, wg_ref, wu_ref, wd_ref, g_ref, b_ref, o_ref, xb_ref, acc_ref, *, n_f):
    f = pl.program_id(1)

    @pl.when(f == 0)
    def _():
        xb_ref[...] = x_ref[...].astype(BF16)
        acc_ref[...] = jnp.zeros_like(acc_ref)

    xb = xb_ref[...]
    h = _dot(xb, wg_ref[...])
    u = _dot(xb, wu_ref[...])
    a = (h * jax.nn.sigmoid(h) * u).astype(BF16)
    acc_ref[...] += _dot(a, wd_ref[...])

    @pl.when(f == n_f - 1)
    def _():
        y = ALPHA * x_ref[...] + 0.5 * acc_ref[...]
        o_ref[...] = _layer_norm(y, g_ref[...], b_ref[...])


FFN_TF = 512


def _column_blocks(w, width):
    lead, (k, n) = w.shape[:-2], w.shape[-2:]
    w = w.astype(BF16).reshape(lead + (k, n // width, width))
    return jnp.swapaxes(w, -3, -2)


def _ffn(x, w_gu, w_d, ln_g, ln_b, layer, which, ln_idx, *, tm=512, tf=FFN_TF):
    m = x.shape[0]
    n_f = D_FF // tf
    return pl.pallas_call(
        functools.partial(_ffn_body, n_f=n_f),
        out_shape=jax.ShapeDtypeStruct((m, D_MODEL), F32),
        grid=(m // tm, n_f),
        in_specs=[
            pl.BlockSpec((tm, D_MODEL), lambda i, f: (i, 0)),
            pl.BlockSpec((None, None, None, D_MODEL, tf), lambda i, f: (layer, which, f, 0, 0)),
            pl.BlockSpec((None, None, None, D_MODEL, tf), lambda i, f: (layer, which, f + n_f, 0, 0)),
            pl.BlockSpec((None, None, tf, D_MODEL), lambda i, f: (layer, which, f, 0)),
            pl.BlockSpec((None, None, 1, D_MODEL), lambda i, f: (layer, ln_idx, 0, 0)),
            pl.BlockSpec((None, None, 1, D_MODEL), lambda i, f: (layer, ln_idx, 0, 0)),
        ],
        out_specs=pl.BlockSpec((tm, D_MODEL), lambda i, f: (i, 0)),
        scratch_shapes=[pltpu.VMEM((tm, D_MODEL), BF16), pltpu.VMEM((tm, D_MODEL), F32)],
        compiler_params=_params(("parallel", "arbitrary")),
        name="ffn",
    )(x, w_gu, w_gu, w_d, ln_g, ln_b)


def _qkv_body(x_ref, w_ref, gn_ref, cos_ref, sin_ref, o_ref, *, tn):
    xb = x_ref[...].astype(BF16)
    cos = cos_ref[...]
    sin = sin_ref[...]
    starts = [j * tn for j in range(QKV_WIDTH // tn)]
    for c0 in sorted(starts, key=lambda c: not (QC_OFF <= c < VC_OFF)):
        acc = _dot(xb, w_ref[:, c0:c0 + tn])
        is_q = (QA_OFF <= c0 < KA_OFF) or (QB_OFF <= c0 < KB_OFF) or (QC_OFF <= c0 < KC_OFF)
        if QC_OFF <= c0 < VC_OFF:
            g = gn_ref[0:1, :] if c0 < KC_OFF else gn_ref[1:2, :]
            for hh in range(tn // HEAD_DIM):
                a = acc[:, hh * HEAD_DIM:(hh + 1) * HEAD_DIM]
                r = lax.rsqrt(jnp.mean(a * a, axis=-1, keepdims=True) + RMS_EPS)
                y = a * r * g
                y = y * cos + pltpu.roll(y, HEAD_DIM // 2, 1) * sin
                if is_q:
                    y = y * QSCALE
                o_ref[:, c0 + hh * HEAD_DIM:c0 + (hh + 1) * HEAD_DIM] = y.astype(BF16)
        else:
            if is_q:
                acc = acc * QSCALE
            o_ref[:, c0:c0 + tn] = acc.astype(BF16)


def _qkv(x, w_qkv, qk_norm_g, cos2, sin2, layer, *, tm=512, tn=256):
    m = x.shape[0]
    s = cos2.shape[0]
    n_pos = s // tm
    return pl.pallas_call(
        functools.partial(_qkv_body, tn=tn),
        out_shape=jax.ShapeDtypeStruct((m, QKV_WIDTH), BF16),
        grid=(m // tm,),
        in_specs=[
            pl.BlockSpec((tm, D_MODEL), lambda i: (i, 0)),
            pl.BlockSpec((None, D_MODEL, QKV_WIDTH), lambda i: (layer, 0, 0), pipeline_mode=pl.Buffered(1)),
            pl.BlockSpec((None, 2, HEAD_DIM), lambda i: (layer, 0, 0)),
            pl.BlockSpec((tm, HEAD_DIM), lambda i: (i % n_pos, 0)),
            pl.BlockSpec((tm, HEAD_DIM), lambda i: (i % n_pos, 0)),
        ],
        out_specs=pl.BlockSpec((tm, QKV_WIDTH), lambda i: (i, 0)),
        compiler_params=_params(("parallel",)),
        name="qkv",
    )(x, w_qkv, qk_norm_g, cos2, sin2)


def _diff_body(q_ref, k_ref, v_ref, bias_ref, lam_ref, sg_ref, o_ref, s1_ref, s2_ref, m_ref, *, t, nc, nq, n_items, lambda_init):
    g = pl.program_id(0)
    i = jnp.minimum(g, n_items - 1) % nq

    @pl.when(g == 0)
    def _():
        s1_ref[1] = jnp.zeros(s1_ref.shape[1:], F32)
        s2_ref[1] = jnp.zeros(s2_ref.shape[1:], F32)
        m_ref[1] = jnp.zeros(m_ref.shape[1:], F32)

    def step(cur, prv):
        q = q_ref[...]
        q1 = q[:, :HEAD_DIM]
        q2 = q[:, HEAD_DIM:]
        m1p = m_ref[prv, 0]
        m2p = m_ref[prv, 1]
        m1 = m2 = acc = l1 = l2 = None
        for c in range(nc):
            kc = k_ref[c * t:(c + 1) * t, :]
            b = bias_ref[jnp.clip(c - i, -2, 2) + 2]
            s1 = _dot_nt(q1, kc[:, :HEAD_DIM]) + b
            s2 = _dot_nt(q2, kc[:, HEAD_DIM:]) + b
            s1_ref[cur, c] = s1
            s2_ref[cur, c] = s2
            f1 = _lane_fold(s1, jnp.maximum)
            f2 = _lane_fold(s2, jnp.maximum)
            m1 = f1 if m1 is None else jnp.maximum(m1, f1)
            m2 = f2 if m2 is None else jnp.maximum(m2, f2)

            e1 = _exp2_minus(s1_ref[prv, c], m1p)
            e2 = _exp2_minus(s2_ref[prv, c], m2p)
            e = jnp.concatenate([e1.astype(BF16), e2.astype(BF16)], axis=0)
            d = _dot(e, v_ref[c * t:(c + 1) * t, :])
            f1 = _lane_fold(e1, jnp.add)
            f2 = _lane_fold(e2, jnp.add)
            acc = d if acc is None else acc + d
            l1 = f1 if l1 is None else l1 + f1
            l2 = f2 if l2 is None else l2 + f2
        m_ref[cur, 0] = _lane_bcast(jnp.max(m1, axis=-1, keepdims=True))
        m_ref[cur, 1] = _lane_bcast(jnp.max(m2, axis=-1, keepdims=True))

        l1 = jnp.sum(l1, axis=-1, keepdims=True)
        l2 = jnp.sum(l2, axis=-1, keepdims=True)
        lam = lam_ref[...]
        lam_full = (jnp.exp(jnp.sum(lam[0:1] * lam[1:2], axis=-1, keepdims=True))
                    - jnp.exp(jnp.sum(lam[2:3] * lam[3:4], axis=-1, keepdims=True)) + lambda_init)
        o = acc[0:t, :] / l1 - lam_full * (acc[t:2 * t, :] / l2)
        o = o * lax.rsqrt(jnp.mean(o * o, axis=-1, keepdims=True) + RMS_EPS) * sg_ref[...]
        o_ref[...] = (o * (1.0 - lambda_init)).astype(BF16)

    @pl.when(g % 2 == 0)
    def _():
        step(0, 1)

    @pl.when(g % 2 == 1)
    def _():
        step(1, 0)


def _pipeline_items(n_items, n_heads, nq):
    def scored(g):
        it = jnp.minimum(g, n_items - 1)
        return it // (n_heads * nq), (it // nq) % n_heads, it % nq

    def finished(g):
        it = jnp.maximum(g - 1, 0)
        return it // (n_heads * nq), (it // nq) % n_heads, it % nq

    return scored, finished


def _diff_attention(qkv, bias_a, lambda_qk, subln_g, layer, n_seq, s, lambda_init, *, t=256):
    m = qkv.shape[0]
    nc = s // t
    nq = s // t
    n_items = n_seq * A_HEADS * nq
    scored, finished = _pipeline_items(n_items, A_HEADS, nq)

    def q_map(g):
        b, h, i = scored(g)
        return (b * nq + i, QA_OFF // A_V + h)

    def k_map(g):
        b, h, _ = scored(g)
        return (b, KA_OFF // A_V + h)

    def v_map(g):
        b, h, _ = finished(g)
        return (b, VA_OFF // A_V + h)

    def o_map(g):
        b, h, i = finished(g)
        return (b * nq + i, h)

    return pl.pallas_call(
        functools.partial(_diff_body, t=t, nc=nc, nq=nq, n_items=n_items, lambda_init=lambda_init),
        out_shape=jax.ShapeDtypeStruct((m, BRANCH_WIDTH), BF16),
        grid=(n_items + 1,),
        in_specs=[
            pl.BlockSpec((t, A_V), q_map),
            pl.BlockSpec((s, A_V), k_map),
            pl.BlockSpec((s, A_V), v_map),
            pl.BlockSpec((None, 5, t, t), lambda g: (scored(g)[1], 0, 0, 0)),
            pl.BlockSpec((None, 4, HEAD_DIM), lambda g: (layer, 0, 0)),
            pl.BlockSpec((None, 1, A_V), lambda g: (layer, 0, 0)),
        ],
        out_specs=pl.BlockSpec((t, A_V), o_map),
        scratch_shapes=[pltpu.VMEM((2, nc, t, t), F32), pltpu.VMEM((2, nc, t, t), F32), pltpu.VMEM((2, 2, t, 128), F32)],
        compiler_params=_params(("arbitrary",)),
        name="diff_attn",
    )(qkv, qkv, qkv, bias_a, lambda_qk, subln_g)


def _axial_body(q_ref, k_ref, v_ref, o_ref, s_ref, m_ref, *, t, tk, nc):
    gq = C_HEADS // C_KV_HEADS
    g = pl.program_id(0)
    half = gq // 2

    @pl.when(g == 0)
    def _():
        s_ref[1] = jnp.zeros(s_ref.shape[1:], F32)
        m_ref[1] = jnp.zeros(m_ref.shape[1:], F32)

    def step(cur, prv):
        q = q_ref[...]
        qs = [jnp.concatenate([q[:, h * HEAD_DIM:(h + 1) * HEAD_DIM] for h in range(r * half, (r + 1) * half)], axis=0)
              for r in range(2)]
        mp = [m_ref[prv, r] for r in range(2)]
        m = [None, None]
        acc = [None, None]
        l = [None, None]
        for c in range(nc):
            kc = k_ref[c * tk:(c + 1) * tk, :]
            vc = v_ref[c * tk:(c + 1) * tk, :]
            for r in range(2):
                sc = _dot_nt(qs[r], kc)
                s_ref[cur, r, c] = sc
                f = _lane_fold(sc, jnp.maximum)
                m[r] = f if m[r] is None else jnp.maximum(m[r], f)

                e = _exp2_minus(s_ref[prv, r, c], mp[r])
                d = _dot(e.astype(BF16), vc)
                f = _lane_fold(e, jnp.add)
                acc[r] = d if acc[r] is None else acc[r] + d
                l[r] = f if l[r] is None else l[r] + f
        for r in range(2):
            m_ref[cur, r] = _lane_bcast(jnp.max(m[r], axis=-1, keepdims=True))
            o = acc[r] / jnp.sum(l[r], axis=-1, keepdims=True)
            for hh in range(half):
                h = r * half + hh
                o_ref[:, h * HEAD_DIM:(h + 1) * HEAD_DIM] = o[hh * t:(hh + 1) * t, :].astype(BF16)

    @pl.when(g % 2 == 0)
    def _():
        step(0, 1)

    @pl.when(g % 2 == 1)
    def _():
        step(1, 0)


def _axial_attention(qkv, n_seq, s, *, t=256, tk=256):
    m = qkv.shape[0]
    gq = C_HEADS // C_KV_HEADS
    gw = gq * HEAD_DIM
    nq = s // t
    nc = s // tk
    n_items = n_seq * C_KV_HEADS * nq
    scored, finished = _pipeline_items(n_items, C_KV_HEADS, nq)

    def q_map(g):
        b, kv, i = scored(g)
        return (b * nq + i, QC_OFF // gw + kv)

    def k_map(g):
        b, kv, _ = scored(g)
        return (b, KC_OFF // HEAD_DIM + kv)

    def v_map(g):
        b, kv, _ = finished(g)
        return (b, VC_OFF // HEAD_DIM + kv)

    def o_map(g):
        b, kv, i = finished(g)
        return (b * nq + i, kv)

    return pl.pallas_call(
        functools.partial(_axial_body, t=t, tk=tk, nc=nc),
        out_shape=jax.ShapeDtypeStruct((m, BRANCH_WIDTH), BF16),
        grid=(n_items + 1,),
        in_specs=[
            pl.BlockSpec((t, gw), q_map),
            pl.BlockSpec((s, HEAD_DIM), k_map),
            pl.BlockSpec((s, HEAD_DIM), v_map),
        ],
        out_specs=pl.BlockSpec((t, gw), o_map),
        scratch_shapes=[pltpu.VMEM((2, 2, nc, gq // 2 * t, tk), F32), pltpu.VMEM((2, 2, gq // 2 * t, 128), F32)],
        compiler_params=_params(("arbitrary",)),
        name="axial_attn",
    )(qkv, qkv, qkv)


def _window_body(sink_ref, q_ref, k_ref, v_ref, bias_ref, o_ref, *, nb):
    g = B_HEADS // B_KV_HEADS
    t = WINDOW
    kv = pl.program_id(1)
    sink = jnp.concatenate([jnp.full((1, t), sink_ref[kv * g + h] * LOG2E, F32) for h in range(g)], axis=1)

    def block(i, carry):
        r0 = pl.multiple_of(i * t, t)
        w0 = pl.multiple_of(jnp.clip(i - 1, 0, nb - 3) * t, t)
        tile = jnp.where(i == 0, 0, jnp.where(i == nb - 1, 2, 1))
        q = q_ref[pl.ds(r0, t), :]
        qs = jnp.concatenate([q[:, h * HEAD_DIM:(h + 1) * HEAD_DIM] for h in range(g)], axis=0)
        st = _dot_nt(k_ref[pl.ds(w0, 3 * t), :], qs) + bias_ref[tile]
        mx = jnp.maximum(jnp.max(st, axis=0, keepdims=True), sink)
        e = jnp.exp2(st - mx)
        l = jnp.sum(e, axis=0, keepdims=True) + jnp.exp2(sink - mx)
        p = (e * (1.0 / l)).astype(BF16)
        ot = lax.dot_general(v_ref[pl.ds(w0, 3 * t), :], p, (((0,), (0,)), ((), ())),
                             preferred_element_type=F32)
        for h in range(g):
            o_ref[pl.ds(r0, t), h * HEAD_DIM:(h + 1) * HEAD_DIM] = ot[:, h * t:(h + 1) * t].T.astype(BF16)
        return carry

    lax.fori_loop(0, nb, block, 0, unroll=4)


def _window_attention(qkv, bias_b, sink_logits, layer, n_seq, s):
    m = qkv.shape[0]
    g = B_HEADS // B_KV_HEADS
    gw = g * HEAD_DIM
    t = WINDOW
    nb = s // t
    grid_spec = pltpu.PrefetchScalarGridSpec(
        num_scalar_prefetch=1,
        grid=(n_seq, B_KV_HEADS),
        in_specs=[
            pl.BlockSpec((s, gw), lambda b, kv, sink: (b, QB_OFF // gw + kv)),
            pl.BlockSpec((s, HEAD_DIM), lambda b, kv, sink: (b, KB_OFF // HEAD_DIM + kv)),
            pl.BlockSpec((s, HEAD_DIM), lambda b, kv, sink: (b, VB_OFF // HEAD_DIM + kv)),
            pl.BlockSpec((3, None, 3 * t, g * t), lambda b, kv, sink: (0, kv, 0, 0)),
        ],
        out_specs=pl.BlockSpec((s, gw), lambda b, kv, sink: (b, kv)),
    )
    return pl.pallas_call(
        functools.partial(_window_body, nb=nb),
        out_shape=jax.ShapeDtypeStruct((m, BRANCH_WIDTH), BF16),
        grid_spec=grid_spec,
        compiler_params=_params(("parallel", "parallel")),
        name="window_attn",
    )(sink_logits[layer], qkv, qkv, qkv, bias_b)


def _gate_body(x_ref, ya_ref, yb_ref, yc_ref, wg_ref, wb_ref, o_ref, xb_ref):
    @pl.when(pl.program_id(1) == 0)
    def _():
        xb_ref[...] = x_ref[...].astype(BF16)

    xb = xb_ref[...]
    merged = None
    for n, y_ref in enumerate((ya_ref, yb_ref, yc_ref)):
        gate = jax.nn.sigmoid(_dot(xb, wg_ref[n]))
        term = gate * _dot(y_ref[...], wb_ref[n])
        merged = term if merged is None else merged + term
    o_ref[...] = merged.astype(BF16)


def _outproj_body(x_ref, m_ref, wo_ref, g_ref, b_ref, o_ref, *, n_split):
    rows = x_ref.shape[0] // n_split
    for r in range(n_split):
        sl = slice(r * rows, (r + 1) * rows)
        y = ALPHA * x_ref[sl, :] + _dot(m_ref[sl, :], wo_ref[...])
        o_ref[sl, :] = _layer_norm(y, g_ref[...], b_ref[...])


GATE_TC = 512


def _merge(x, ya, yb, yc, w_gate, w_branch, w_out, ln_g, ln_b, layer, *, tm=512, tc=GATE_TC):
    m = x.shape[0]
    yspec = pl.BlockSpec((tm, BRANCH_WIDTH), lambda i, j: (i, 0))
    merged = pl.pallas_call(
        _gate_body,
        out_shape=jax.ShapeDtypeStruct((m, D_MODEL), BF16),
        grid=(m // tm, D_MODEL // tc),
        in_specs=[
            pl.BlockSpec((tm, D_MODEL), lambda i, j: (i, 0)),
            yspec, yspec, yspec,
            pl.BlockSpec((None, None, N_BRANCH, D_MODEL, tc), lambda i, j: (layer, j, 0, 0, 0)),
            pl.BlockSpec((None, None, N_BRANCH, BRANCH_WIDTH, tc), lambda i, j: (layer, j, 0, 0, 0)),
        ],
        out_specs=pl.BlockSpec((tm, tc), lambda i, j: (i, j)),
        scratch_shapes=[pltpu.VMEM((tm, D_MODEL), BF16)],
        compiler_params=_params(("parallel", "arbitrary")),
        name="gate_merge",
    )(x, ya, yb, yc, w_gate, w_branch)
    return pl.pallas_call(
        functools.partial(_outproj_body, n_split=2),
        out_shape=jax.ShapeDtypeStruct((m, D_MODEL), F32),
        grid=(m // tm,),
        in_specs=[
            pl.BlockSpec((tm, D_MODEL), lambda i: (i, 0)),
            pl.BlockSpec((tm, D_MODEL), lambda i: (i, 0)),
            pl.BlockSpec((None, D_MODEL, D_MODEL), lambda i: (layer, 0, 0), pipeline_mode=pl.Buffered(1)),
            pl.BlockSpec((None, None, 1, D_MODEL), lambda i: (layer, 1, 0, 0)),
            pl.BlockSpec((None, None, 1, D_MODEL), lambda i: (layer, 1, 0, 0)),
        ],
        out_specs=pl.BlockSpec((tm, D_MODEL), lambda i: (i, 0)),
        compiler_params=_params(("parallel",)),
        name="out_proj",
    )(x, merged, w_out, ln_g, ln_b)


def _memkv_body(m_ref, w_ref, o_ref):
    o_ref[...] = _dot(m_ref[...].astype(BF16), w_ref[...]).astype(BF16)


def _memkv(mem, w_kv, layer):
    rows = mem.shape[0]
    return pl.pallas_call(
        _memkv_body,
        out_shape=jax.ShapeDtypeStruct((rows, 2 * MEM_INNER), BF16),
        grid=(rows // MEM_TOKENS,),
        in_specs=[
            pl.BlockSpec((MEM_TOKENS, D_MODEL), lambda i: (i, 0)),
            pl.BlockSpec((None, D_MODEL, 2 * MEM_INNER), lambda i: (layer, 0, 0)),
        ],
        out_specs=pl.BlockSpec((MEM_TOKENS, 2 * MEM_INNER), lambda i: (i, 0)),
        compiler_params=_params(("parallel",)),
        name="memkv",
    )(mem, w_kv)


def _mem_body(x_ref, kv_ref, wq_ref, wo_ref, g_ref, b_ref, o_ref, *, n_split):
    kv = kv_ref[...]
    rows = x_ref.shape[0] // n_split
    for r in range(n_split):
        sl = slice(r * rows, (r + 1) * rows)
        x = x_ref[sl, :]
        q = (_dot(x.astype(BF16), wq_ref[...]) * QSCALE).astype(BF16)
        heads = []
        for h in range(MEM_HEADS):
            sc = _dot_nt(q[:, h * HEAD_DIM:(h + 1) * HEAD_DIM], kv[:, h * HEAD_DIM:(h + 1) * HEAD_DIM])
            e = jnp.exp2(sc - jnp.max(sc, axis=-1, keepdims=True))
            l = jnp.sum(e, axis=-1, keepdims=True)
            v = kv[:, MEM_INNER + h * HEAD_DIM:MEM_INNER + (h + 1) * HEAD_DIM]
            heads.append((_dot(e.astype(BF16), v) / l).astype(BF16))
        o = jnp.concatenate(heads, axis=1)
        y = ALPHA * x + _dot(o, wo_ref[...])
        o_ref[sl, :] = _layer_norm(y, g_ref[...], b_ref[...])


def _mem_attention(x, kv, w_q, w_o, ln_g, ln_b, layer, s, *, tm=512):
    m = x.shape[0]
    per_seq = s // tm
    return pl.pallas_call(
        functools.partial(_mem_body, n_split=1),
        out_shape=jax.ShapeDtypeStruct((m, D_MODEL), F32),
        grid=(m // tm,),
        in_specs=[
            pl.BlockSpec((tm, D_MODEL), lambda i: (i, 0)),
            pl.BlockSpec((MEM_TOKENS, 2 * MEM_INNER), lambda i: (i // per_seq, 0)),
            pl.BlockSpec((None, D_MODEL, MEM_INNER), lambda i: (layer, 0, 0)),
            pl.BlockSpec((None, MEM_INNER, D_MODEL), lambda i: (layer, 0, 0)),
            pl.BlockSpec((None, None, 1, D_MODEL), lambda i: (layer, 2, 0, 0)),
            pl.BlockSpec((None, None, 1, D_MODEL), lambda i: (layer, 2, 0, 0)),
        ],
        out_specs=pl.BlockSpec((tm, D_MODEL), lambda i: (i, 0)),
        compiler_params=_params(("parallel",)),
        name="mem_attn",
    )(x, kv, w_q, w_o, ln_g, ln_b)


def _t5_bucket_np(rel):
    half = NUM_BUCKETS // 2
    max_exact = half // 2
    rel = np.asarray(rel, np.int32)
    ret = np.where(rel > 0, half, 0)
    n = np.abs(rel)
    ratio = np.log(np.maximum(n, 1).astype(np.float32) / np.float32(max_exact)) / np.float32(math.log(MAX_DISTANCE / max_exact))
    large = max_exact + (ratio * np.float32(half - max_exact)).astype(np.int32)
    large = np.minimum(large, half - 1)
    return (ret + np.where(n < max_exact, n, large)).astype(np.int32)


def _lookup_bias(table, buckets):
    n_heads = table.shape[1]
    b = jnp.asarray(buckets)[None]
    tab = table * LOG2E
    out = jnp.zeros((n_heads,) + buckets.shape, F32)
    for k in range(NUM_BUCKETS):
        out = jnp.where(b == k, tab[k].reshape((n_heads,) + (1,) * buckets.ndim), out)
    return out


def _diff_bias_tiles(table_a, t):
    ii = np.arange(t)[:, None]
    jj = np.arange(t)[None, :]
    buckets = np.stack([_t5_bucket_np(d * t + jj - ii) for d in (-2, -1, 0, 1, 2)])
    return _lookup_bias(table_a, buckets)


def _window_bias_tiles(table_b):
    g = B_HEADS // B_KV_HEADS
    rel = np.stack([(np.arange(3 * WINDOW) - p * WINDOW)[None, :] - np.arange(WINDOW)[:, None] for p in range(3)])
    bias = _lookup_bias(table_b, _t5_bucket_np(rel))
    bias = jnp.where(jnp.asarray(np.abs(rel) <= WINDOW)[None], bias, NEG_INF)
    bias = bias.reshape(B_KV_HEADS, g, 3, WINDOW, 3 * WINDOW)
    return jnp.transpose(bias, (2, 0, 4, 1, 3)).reshape(3, B_KV_HEADS, 3 * WINDOW, g * WINDOW)


def _rope_tables(s):
    n_rows = s // GRID_W
    row = np.repeat(np.arange(n_rows), GRID_W).astype(np.float32)
    col = np.tile(np.arange(GRID_W), n_rows).astype(np.float32)
    axis_dims = HEAD_DIM // 2
    inv = ROPE_THETA ** (-jnp.arange(0, axis_dims, 2, dtype=F32) / axis_dims)
    ang = jnp.concatenate([row[:, None] * inv[None, :], col[:, None] * inv[None, :]], axis=-1)
    cos, sin = jnp.cos(ang), jnp.sin(ang)
    return jnp.concatenate([cos, cos], axis=-1), jnp.concatenate([-sin, sin], axis=-1)


def _run_trunk(x, mem, wts, bias_a, bias_b):
    n_seq, s, _ = x.shape
    x = x.reshape(n_seq * s, D_MODEL)
    mem = mem.reshape(n_seq * MEM_TOKENS, D_MODEL)
    cos2, sin2 = _rope_tables(s)
    for i in range(DEPTH):
        lambda_init = 0.8 - 0.6 * math.exp(-0.3 * i)
        x = _ffn(x, wts["w_ffn_gu"], wts["w_ffn_d"], wts["ln_g"], wts["ln_b"], i, 0, 0)
        qkv = _qkv(x, wts["w_qkv"], wts["qk_norm_g"], cos2, sin2, i)
        ya = _diff_attention(qkv, bias_a, wts["lambda_qk"], wts["subln_g"], i, n_seq, s, lambda_init)
        yb = _window_attention(qkv, bias_b, wts["sink_logits"], i, n_seq, s)
        yc = _axial_attention(qkv, n_seq, s)
        x = _merge(x, ya, yb, yc, wts["w_gate"], wts["w_branch"], wts["w_out"], wts["ln_g"], wts["ln_b"], i)
        kv = _memkv(mem, wts["w_mem_kv"], i)
        x = _mem_attention(x, kv, wts["w_mem_q"], wts["w_mem_o"], wts["ln_g"], wts["ln_b"], i, s)
        x = _ffn(x, wts["w_ffn_gu"], wts["w_ffn_d"], wts["ln_g"], wts["ln_b"], i, 1, 3)
    return x.reshape(n_seq, s, D_MODEL)


def kernel(x_prompt, x_sample, mem_prompt, mem_sample, rel_bias_table, w_in, w_branch, w_out, lambda_qk, subln_g, sink_logits, qk_norm_g, w_mem_q, w_mem_kv, w_mem_o, w_ffn_gu, w_ffn_d, ln_g, ln_b):
    wts = {
        "w_qkv": w_in[:, :, :QKV_WIDTH].astype(BF16),
        "w_gate": jnp.transpose(w_in[:, :, QKV_WIDTH:].astype(BF16).reshape(DEPTH, D_MODEL, N_BRANCH, D_MODEL // GATE_TC, GATE_TC),
                                (0, 3, 2, 1, 4)),
        "w_branch": jnp.swapaxes(_column_blocks(w_branch, GATE_TC), 1, 2),
        "w_out": w_out.astype(BF16),
        "w_mem_q": w_mem_q.astype(BF16),
        "w_mem_kv": w_mem_kv.astype(BF16),
        "w_mem_o": w_mem_o.astype(BF16),
        "w_ffn_gu": _column_blocks(w_ffn_gu, FFN_TF),
        "w_ffn_d": w_ffn_d.astype(BF16),
        "lambda_qk": lambda_qk,
        "subln_g": subln_g.reshape(DEPTH, 1, A_V),
        "sink_logits": sink_logits,
        "qk_norm_g": qk_norm_g,
        "ln_g": ln_g.reshape(DEPTH, 4, 1, D_MODEL),
        "ln_b": ln_b.reshape(DEPTH, 4, 1, D_MODEL),
    }
    bias_a = _diff_bias_tiles(rel_bias_table[:, :A_HEADS], 256)
    bias_b = _window_bias_tiles(rel_bias_table[:, A_HEADS:])
    y_prompt = _run_trunk(x_prompt, mem_prompt, wts, bias_a, bias_b)
    y_sample = _run_trunk(x_sample, mem_sample, wts, bias_a, bias_b)
    return (y_prompt, y_sample)
```

```python
import functools
import math

import numpy as np
import jax
import jax.numpy as jnp
from jax import lax
from jax.experimental import pallas as pl
from jax.experimental.pallas import tpu as pltpu

D_MODEL = 2048
DEPTH = 4
HEAD_DIM = 128
A_HEADS = 4
A_V = 2 * HEAD_DIM
B_HEADS = 8
B_KV_HEADS = 2
WINDOW = 128
C_HEADS = 8
C_KV_HEADS = 2
GRID_W = 64
ROPE_THETA = 10000.0
NUM_BUCKETS = 32
MAX_DISTANCE = 128
MEM_TOKENS = 256
MEM_HEADS = 4
MEM_INNER = MEM_HEADS * HEAD_DIM
D_FF = 5632
N_BRANCH = 3
BRANCH_WIDTH = 1024
QKV_WIDTH = 6144
ALPHA = (2 * DEPTH) ** 0.25
LN_EPS = 1e-5
RMS_EPS = 1e-6
NEG_INF = -1e30
LOG2E = math.log2(math.e)
QSCALE = HEAD_DIM ** -0.5 * LOG2E

QA_OFF, KA_OFF, VA_OFF = 0, 1024, 2048
QB_OFF, KB_OFF, VB_OFF = 3072, 4096, 4352
QC_OFF, KC_OFF, VC_OFF = 4608, 5632, 5888

V7X_VMEM_BYTES = 64 * 1024 * 1024
VMEM_LIMIT = 56 * 1024 * 1024
F32 = jnp.float32
BF16 = jnp.bfloat16


def _params(semantics, vmem_limit=VMEM_LIMIT):
    return pltpu.CompilerParams(dimension_semantics=semantics, vmem_limit_bytes=vmem_limit)


def _layer_norm(y, g, b):
    mu = jnp.mean(y, axis=-1, keepdims=True)
    d = y - mu
    var = jnp.mean(d * d, axis=-1, keepdims=True)
    return d * lax.rsqrt(var + LN_EPS) * g + b


def _dot(a, b):
    return jnp.dot(a, b, preferred_element_type=F32)


def _dot_nt(a, b):
    return lax.dot_general(a, b, (((1,), (1,)), ((), ())), preferred_element_type=F32)


def _lane_fold(x, op):
    parts = [x[:, j * 128:(j + 1) * 128] for j in range(x.shape[1] // 128)]
    return functools.reduce(op, parts)


def _exp2_minus(x, col):
    parts = [jnp.exp2(x[:, j * 128:(j + 1) * 128] - col) for j in range(x.shape[1] // 128)]
    return parts[0] if len(parts) == 1 else jnp.concatenate(parts, axis=1)


def _lane_bcast(col):
    return jnp.broadcast_to(col, (col.shape[0], 128))


def _ffn_body(x_ref, wg_ref, wu_ref, wd_ref, g_ref, b_ref, o_ref, xb_ref, acc_ref, *, n_f):
    f = pl.program_id(1)

    @pl.when(f == 0)
    def _():
        xb_ref[...] = x_ref[...].astype(BF16)
        acc_ref[...] = jnp.zeros_like(acc_ref)

    xb = xb_ref[...]
    h = _dot(xb, wg_ref[...])
    u = _dot(xb, wu_ref[...])
    a = (h * jax.nn.sigmoid(h) * u).astype(BF16)
    acc_ref[...] += _dot(a, wd_ref[...])

    @pl.when(f == n_f - 1)
    def _():
        y = ALPHA * x_ref[...] + 0.5 * acc_ref[...]
        o_ref[...] = _layer_norm(y, g_ref[...], b_ref[...])


def _ffn(x, w_gu, w_d, ln_g, ln_b, layer, which, ln_idx, *, tm=512, tf=512):
    m = x.shape[0]
    n_f = D_FF // tf
    return pl.pallas_call(
        functools.partial(_ffn_body, n_f=n_f),
        out_shape=jax.ShapeDtypeStruct((m, D_MODEL), F32),
        grid=(m // tm, n_f),
        in_specs=[
            pl.BlockSpec((tm, D_MODEL), lambda i, f: (i, 0)),
            pl.BlockSpec((None, None, D_MODEL, tf), lambda i, f: (layer, which, 0, f)),
            pl.BlockSpec((None, None, D_MODEL, tf), lambda i, f: (layer, which, 0, f + n_f)),
            pl.BlockSpec((None, None, tf, D_MODEL), lambda i, f: (layer, which, f, 0)),
            pl.BlockSpec((None, None, 1, D_MODEL), lambda i, f: (layer, ln_idx, 0, 0)),
            pl.BlockSpec((None, None, 1, D_MODEL), lambda i, f: (layer, ln_idx, 0, 0)),
        ],
        out_specs=pl.BlockSpec((tm, D_MODEL), lambda i, f: (i, 0)),
        scratch_shapes=[pltpu.VMEM((tm, D_MODEL), BF16), pltpu.VMEM((tm, D_MODEL), F32)],
        compiler_params=_params(("parallel", "arbitrary")),
        name="ffn",
    )(x, w_gu, w_gu, w_d, ln_g, ln_b)


def _qkv_body(x_ref, w_ref, gn_ref, cos_ref, sin_ref, o_ref, *, tn):
    xb = x_ref[...].astype(BF16)
    cos = cos_ref[...]
    sin = sin_ref[...]
    starts = [j * tn for j in range(QKV_WIDTH // tn)]
    for c0 in sorted(starts, key=lambda c: not (QC_OFF <= c < VC_OFF)):
        acc = _dot(xb, w_ref[:, c0:c0 + tn])
        is_q = (QA_OFF <= c0 < KA_OFF) or (QB_OFF <= c0 < KB_OFF) or (QC_OFF <= c0 < KC_OFF)
        if QC_OFF <= c0 < VC_OFF:
            g = gn_ref[0:1, :] if c0 < KC_OFF else gn_ref[1:2, :]
            for hh in range(tn // HEAD_DIM):
                a = acc[:, hh * HEAD_DIM:(hh + 1) * HEAD_DIM]
                r = lax.rsqrt(jnp.mean(a * a, axis=-1, keepdims=True) + RMS_EPS)
                y = a * r * g
                y = y * cos + pltpu.roll(y, HEAD_DIM // 2, 1) * sin
                if is_q:
                    y = y * QSCALE
                o_ref[:, c0 + hh * HEAD_DIM:c0 + (hh + 1) * HEAD_DIM] = y.astype(BF16)
        else:
            if is_q:
                acc = acc * QSCALE
            o_ref[:, c0:c0 + tn] = acc.astype(BF16)


def _qkv(x, w_qkv, qk_norm_g, cos2, sin2, layer, *, tm=512, tn=256):
    m = x.shape[0]
    s = cos2.shape[0]
    n_pos = s // tm
    return pl.pallas_call(
        functools.partial(_qkv_body, tn=tn),
        out_shape=jax.ShapeDtypeStruct((m, QKV_WIDTH), BF16),
        grid=(m // tm,),
        in_specs=[
            pl.BlockSpec((tm, D_MODEL), lambda i: (i, 0)),
            pl.BlockSpec((None, D_MODEL, QKV_WIDTH), lambda i: (layer, 0, 0), pipeline_mode=pl.Buffered(1)),
            pl.BlockSpec((None, 2, HEAD_DIM), lambda i: (layer, 0, 0)),
            pl.BlockSpec((tm, HEAD_DIM), lambda i: (i % n_pos, 0)),
            pl.BlockSpec((tm, HEAD_DIM), lambda i: (i % n_pos, 0)),
        ],
        out_specs=pl.BlockSpec((tm, QKV_WIDTH), lambda i: (i, 0)),
        compiler_params=_params(("parallel",)),
        name="qkv",
    )(x, w_qkv, qk_norm_g, cos2, sin2)


def _diff_body(q_ref, k_ref, v_ref, bias_ref, lam_ref, sg_ref, o_ref, s1_ref, s2_ref, m_ref, *, t, tk, nc, nq, n_items, lambda_init):
    g = pl.program_id(0)
    i = jnp.minimum(g, n_items - 1) % nq

    @pl.when(g == 0)
    def _():
        s1_ref[1] = jnp.zeros(s1_ref.shape[1:], F32)
        s2_ref[1] = jnp.zeros(s2_ref.shape[1:], F32)
        m_ref[1] = jnp.zeros(m_ref.shape[1:], F32)

    def step(cur, prv):
        q = q_ref[...]
        q1 = q[:, :HEAD_DIM]
        q2 = q[:, HEAD_DIM:]
        m1p = m_ref[prv, 0]
        m2p = m_ref[prv, 1]
        m1 = m2 = acc = l1 = l2 = None
        for c in range(nc):
            kc = k_ref[c * tk:(c + 1) * tk, :]
            b = bias_ref[jnp.clip(c - i * (t // tk), -2, t // tk + 1) + 2]
            s1 = _dot_nt(q1, kc[:, :HEAD_DIM]) + b
            s2 = _dot_nt(q2, kc[:, HEAD_DIM:]) + b
            s1_ref[cur, c] = s1
            s2_ref[cur, c] = s2
            f1 = _lane_fold(s1, jnp.maximum)
            f2 = _lane_fold(s2, jnp.maximum)
            m1 = f1 if m1 is None else jnp.maximum(m1, f1)
            m2 = f2 if m2 is None else jnp.maximum(m2, f2)

            e1 = _exp2_minus(s1_ref[prv, c], m1p)
            e2 = _exp2_minus(s2_ref[prv, c], m2p)
            e = jnp.concatenate([e1.astype(BF16), e2.astype(BF16)], axis=0)
            d = _dot(e, v_ref[c * tk:(c + 1) * tk, :])
            f1 = _lane_fold(e1, jnp.add)
            f2 = _lane_fold(e2, jnp.add)
            acc = d if acc is None else acc + d
            l1 = f1 if l1 is None else l1 + f1
            l2 = f2 if l2 is None else l2 + f2
        m_ref[cur, 0] = _lane_bcast(jnp.max(m1, axis=-1, keepdims=True))
        m_ref[cur, 1] = _lane_bcast(jnp.max(m2, axis=-1, keepdims=True))

        l1 = jnp.sum(l1, axis=-1, keepdims=True)
        l2 = jnp.sum(l2, axis=-1, keepdims=True)
        lam = lam_ref[...]
        lam_full = (jnp.exp(jnp.sum(lam[0:1] * lam[1:2], axis=-1, keepdims=True))
                    - jnp.exp(jnp.sum(lam[2:3] * lam[3:4], axis=-1, keepdims=True)) + lambda_init)
        o = acc[0:t, :] / l1 - lam_full * (acc[t:2 * t, :] / l2)
        o = o * lax.rsqrt(jnp.mean(o * o, axis=-1, keepdims=True) + RMS_EPS) * sg_ref[...]
        o_ref[...] = (o * (1.0 - lambda_init)).astype(BF16)

    @pl.when(g % 2 == 0)
    def _():
        step(0, 1)

    @pl.when(g % 2 == 1)
    def _():
        step(1, 0)


def _pipeline_items(n_items, n_heads, nq):
    def scored(g):
        it = jnp.minimum(g, n_items - 1)
        return it // (n_heads * nq), (it // nq) % n_heads, it % nq

    def finished(g):
        it = jnp.maximum(g - 1, 0)
        return it // (n_heads * nq), (it // nq) % n_heads, it % nq

    return scored, finished


DIFF_TK = 256


def _diff_rows(s):
    return 512 if s <= 2048 else 256


def _diff_attention(qkv, bias_a, lambda_qk, subln_g, layer, n_seq, s, lambda_init, *, tk=DIFF_TK):
    t = _diff_rows(s)
    m = qkv.shape[0]
    nc = s // tk
    nq = s // t
    n_tiles = t // tk + 4
    n_items = n_seq * A_HEADS * nq
    scored, finished = _pipeline_items(n_items, A_HEADS, nq)

    def q_map(g):
        b, h, i = scored(g)
        return (b * nq + i, QA_OFF // A_V + h)

    def k_map(g):
        b, h, _ = scored(g)
        return (b, KA_OFF // A_V + h)

    def v_map(g):
        b, h, _ = finished(g)
        return (b, VA_OFF // A_V + h)

    def o_map(g):
        b, h, i = finished(g)
        return (b * nq + i, h)

    return pl.pallas_call(
        functools.partial(_diff_body, t=t, tk=tk, nc=nc, nq=nq, n_items=n_items, lambda_init=lambda_init),
        out_shape=jax.ShapeDtypeStruct((m, BRANCH_WIDTH), BF16),
        grid=(n_items + 1,),
        in_specs=[
            pl.BlockSpec((t, A_V), q_map),
            pl.BlockSpec((s, A_V), k_map),
            pl.BlockSpec((s, A_V), v_map),
            pl.BlockSpec((None, n_tiles, t, tk), lambda g: (scored(g)[1], 0, 0, 0)),
            pl.BlockSpec((None, 4, HEAD_DIM), lambda g: (layer, 0, 0)),
            pl.BlockSpec((None, 1, A_V), lambda g: (layer, 0, 0)),
        ],
        out_specs=pl.BlockSpec((t, A_V), o_map),
        scratch_shapes=[pltpu.VMEM((2, nc, t, tk), F32), pltpu.VMEM((2, nc, t, tk), F32), pltpu.VMEM((2, 2, t, 128), F32)],
        compiler_params=_params(("arbitrary",)),
        name="diff_attn",
    )(qkv, qkv, qkv, bias_a, lambda_qk, subln_g)


def _axial_body(q_ref, k_ref, v_ref, o_ref, s_ref, m_ref, *, t, tk, nc):
    gq = C_HEADS // C_KV_HEADS
    g = pl.program_id(0)
    half = gq // 2

    @pl.when(g == 0)
    def _():
        s_ref[1] = jnp.zeros(s_ref.shape[1:], F32)
        m_ref[1] = jnp.zeros(m_ref.shape[1:], F32)

    def step(cur, prv):
        q = q_ref[...]
        qs = [jnp.concatenate([q[:, h * HEAD_DIM:(h + 1) * HEAD_DIM] for h in range(r * half, (r + 1) * half)], axis=0)
              for r in range(2)]
        mp = [m_ref[prv, r] for r in range(2)]
        m = [None, None]
        acc = [None, None]
        l = [None, None]
        for c in range(nc):
            kc = k_ref[c * tk:(c + 1) * tk, :]
            vc = v_ref[c * tk:(c + 1) * tk, :]
            for r in range(2):
                sc = _dot_nt(qs[r], kc)
                s_ref[cur, r, c] = sc
                f = _lane_fold(sc, jnp.maximum)
                m[r] = f if m[r] is None else jnp.maximum(m[r], f)

                e = _exp2_minus(s_ref[prv, r, c], mp[r])
                d = _dot(e.astype(BF16), vc)
                f = _lane_fold(e, jnp.add)
                acc[r] = d if acc[r] is None else acc[r] + d
                l[r] = f if l[r] is None else l[r] + f
        for r in range(2):
            m_ref[cur, r] = _lane_bcast(jnp.max(m[r], axis=-1, keepdims=True))
            o = acc[r] / jnp.sum(l[r], axis=-1, keepdims=True)
            for hh in range(half):
                h = r * half + hh
                o_ref[:, h * HEAD_DIM:(h + 1) * HEAD_DIM] = o[hh * t:(hh + 1) * t, :].astype(BF16)

    @pl.when(g % 2 == 0)
    def _():
        step(0, 1)

    @pl.when(g % 2 == 1)
    def _():
        step(1, 0)


def _axial_attention(qkv, n_seq, s, *, t=256, tk=256):
    m = qkv.shape[0]
    gq = C_HEADS // C_KV_HEADS
    gw = gq * HEAD_DIM
    nq = s // t
    nc = s // tk
    n_items = n_seq * C_KV_HEADS * nq
    scored, finished = _pipeline_items(n_items, C_KV_HEADS, nq)

    def q_map(g):
        b, kv, i = scored(g)
        return (b * nq + i, QC_OFF // gw + kv)

    def k_map(g):
        b, kv, _ = scored(g)
        return (b, KC_OFF // HEAD_DIM + kv)

    def v_map(g):
        b, kv, _ = finished(g)
        return (b, VC_OFF // HEAD_DIM + kv)

    def o_map(g):
        b, kv, i = finished(g)
        return (b * nq + i, kv)

    return pl.pallas_call(
        functools.partial(_axial_body, t=t, tk=tk, nc=nc),
        out_shape=jax.ShapeDtypeStruct((m, BRANCH_WIDTH), BF16),
        grid=(n_items + 1,),
        in_specs=[
            pl.BlockSpec((t, gw), q_map),
            pl.BlockSpec((s, HEAD_DIM), k_map),
            pl.BlockSpec((s, HEAD_DIM), v_map),
        ],
        out_specs=pl.BlockSpec((t, gw), o_map),
        scratch_shapes=[pltpu.VMEM((2, 2, nc, gq // 2 * t, tk), F32), pltpu.VMEM((2, 2, gq // 2 * t, 128), F32)],
        compiler_params=_params(("arbitrary",)),
        name="axial_attn",
    )(qkv, qkv, qkv)


def _window_body(sink_ref, q_ref, k_ref, v_ref, bias_ref, o_ref, *, nb):
    g = B_HEADS // B_KV_HEADS
    t = WINDOW
    kv = pl.program_id(1)
    sink = jnp.concatenate([jnp.full((1, t), sink_ref[kv * g + h] * LOG2E, F32) for h in range(g)], axis=1)

    def block(i, carry):
        r0 = pl.multiple_of(i * t, t)
        w0 = pl.multiple_of(jnp.clip(i - 1, 0, nb - 3) * t, t)
        tile = jnp.where(i == 0, 0, jnp.where(i == nb - 1, 2, 1))
        q = q_ref[pl.ds(r0, t), :]
        qs = jnp.concatenate([q[:, h * HEAD_DIM:(h + 1) * HEAD_DIM] for h in range(g)], axis=0)
        st = _dot_nt(k_ref[pl.ds(w0, 3 * t), :], qs) + bias_ref[tile]
        mx = jnp.maximum(jnp.max(st, axis=0, keepdims=True), sink)
        e = jnp.exp2(st - mx)
        l = jnp.sum(e, axis=0, keepdims=True) + jnp.exp2(sink - mx)
        p = (e * (1.0 / l)).astype(BF16)
        ot = lax.dot_general(v_ref[pl.ds(w0, 3 * t), :], p, (((0,), (0,)), ((), ())),
                             preferred_element_type=F32)
        for h in range(g):
            o_ref[pl.ds(r0, t), h * HEAD_DIM:(h + 1) * HEAD_DIM] = ot[:, h * t:(h + 1) * t].T.astype(BF16)
        return carry

    lax.fori_loop(0, nb, block, 0, unroll=4)


def _window_attention(qkv, bias_b, sink_logits, layer, n_seq, s):
    m = qkv.shape[0]
    g = B_HEADS // B_KV_HEADS
    gw = g * HEAD_DIM
    t = WINDOW
    nb = s // t
    grid_spec = pltpu.PrefetchScalarGridSpec(
        num_scalar_prefetch=1,
        grid=(n_seq, B_KV_HEADS),
        in_specs=[
            pl.BlockSpec((s, gw), lambda b, kv, sink: (b, QB_OFF // gw + kv)),
            pl.BlockSpec((s, HEAD_DIM), lambda b, kv, sink: (b, KB_OFF // HEAD_DIM + kv)),
            pl.BlockSpec((s, HEAD_DIM), lambda b, kv, sink: (b, VB_OFF // HEAD_DIM + kv)),
            pl.BlockSpec((3, None, 3 * t, g * t), lambda b, kv, sink: (0, kv, 0, 0)),
        ],
        out_specs=pl.BlockSpec((s, gw), lambda b, kv, sink: (b, kv)),
    )
    return pl.pallas_call(
        functools.partial(_window_body, nb=nb),
        out_shape=jax.ShapeDtypeStruct((m, BRANCH_WIDTH), BF16),
        grid_spec=grid_spec,
        compiler_params=_params(("parallel", "parallel")),
        name="window_attn",
    )(sink_logits[layer], qkv, qkv, qkv, bias_b)


def _gate_body(x_ref, ya_ref, yb_ref, yc_ref, wga_ref, wgb_ref, wgc_ref, wb_ref, o_ref, xb_ref):
    @pl.when(pl.program_id(1) == 0)
    def _():
        xb_ref[...] = x_ref[...].astype(BF16)

    xb = xb_ref[...]
    merged = None
    for n, (y_ref, wg_ref) in enumerate(((ya_ref, wga_ref), (yb_ref, wgb_ref), (yc_ref, wgc_ref))):
        gate = jax.nn.sigmoid(_dot(xb, wg_ref[...]))
        term = gate * _dot(y_ref[...], wb_ref[n])
        merged = term if merged is None else merged + term
    o_ref[...] = merged.astype(BF16)


def _outproj_body(x_ref, m_ref, wo_ref, g_ref, b_ref, o_ref, *, n_split):
    rows = x_ref.shape[0] // n_split
    for r in range(n_split):
        sl = slice(r * rows, (r + 1) * rows)
        y = ALPHA * x_ref[sl, :] + _dot(m_ref[sl, :], wo_ref[...])
        o_ref[sl, :] = _layer_norm(y, g_ref[...], b_ref[...])


def _merge(x, ya, yb, yc, w_in, w_branch, w_out, ln_g, ln_b, layer, *, tm=512, tc=512):
    m = x.shape[0]
    yspec = pl.BlockSpec((tm, BRANCH_WIDTH), lambda i, j: (i, 0))

    def gate_spec(n):
        return pl.BlockSpec((None, D_MODEL, tc), lambda i, j: (layer, 0, (QKV_WIDTH + n * D_MODEL) // tc + j))

    merged = pl.pallas_call(
        _gate_body,
        out_shape=jax.ShapeDtypeStruct((m, D_MODEL), BF16),
        grid=(m // tm, D_MODEL // tc),
        in_specs=[
            pl.BlockSpec((tm, D_MODEL), lambda i, j: (i, 0)),
            yspec, yspec, yspec,
            gate_spec(0), gate_spec(1), gate_spec(2),
            pl.BlockSpec((None, N_BRANCH, BRANCH_WIDTH, tc), lambda i, j: (layer, 0, 0, j)),
        ],
        out_specs=pl.BlockSpec((tm, tc), lambda i, j: (i, j)),
        scratch_shapes=[pltpu.VMEM((tm, D_MODEL), BF16)],
        compiler_params=_params(("parallel", "arbitrary")),
        name="gate_merge",
    )(x, ya, yb, yc, w_in, w_in, w_in, w_branch)
    return pl.pallas_call(
        functools.partial(_outproj_body, n_split=2),
        out_shape=jax.ShapeDtypeStruct((m, D_MODEL), F32),
        grid=(m // tm,),
        in_specs=[
            pl.BlockSpec((tm, D_MODEL), lambda i: (i, 0)),
            pl.BlockSpec((tm, D_MODEL), lambda i: (i, 0)),
            pl.BlockSpec((None, D_MODEL, D_MODEL), lambda i: (layer, 0, 0), pipeline_mode=pl.Buffered(1)),
            pl.BlockSpec((None, None, 1, D_MODEL), lambda i: (layer, 1, 0, 0)),
            pl.BlockSpec((None, None, 1, D_MODEL), lambda i: (layer, 1, 0, 0)),
        ],
        out_specs=pl.BlockSpec((tm, D_MODEL), lambda i: (i, 0)),
        compiler_params=_params(("parallel",)),
        name="out_proj",
    )(x, merged, w_out, ln_g, ln_b)


def _memkv_body(m_ref, w_ref, o_ref):
    o_ref[...] = _dot(m_ref[...].astype(BF16), w_ref[...]).astype(BF16)


def _memkv(mem, w_kv, layer):
    rows = mem.shape[0]
    return pl.pallas_call(
        _memkv_body,
        out_shape=jax.ShapeDtypeStruct((rows, 2 * MEM_INNER), BF16),
        grid=(rows // MEM_TOKENS,),
        in_specs=[
            pl.BlockSpec((MEM_TOKENS, D_MODEL), lambda i: (i, 0)),
            pl.BlockSpec((None, D_MODEL, 2 * MEM_INNER), lambda i: (layer, 0, 0)),
        ],
        out_specs=pl.BlockSpec((MEM_TOKENS, 2 * MEM_INNER), lambda i: (i, 0)),
        compiler_params=_params(("parallel",)),
        name="memkv",
    )(mem, w_kv)


def _mem_body(x_ref, kv_ref, wq_ref, wo_ref, g_ref, b_ref, o_ref, *, n_split):
    kv = kv_ref[...]
    rows = x_ref.shape[0] // n_split
    for r in range(n_split):
        sl = slice(r * rows, (r + 1) * rows)
        x = x_ref[sl, :]
        q = (_dot(x.astype(BF16), wq_ref[...]) * QSCALE).astype(BF16)
        heads = []
        for h in range(MEM_HEADS):
            sc = _dot_nt(q[:, h * HEAD_DIM:(h + 1) * HEAD_DIM], kv[:, h * HEAD_DIM:(h + 1) * HEAD_DIM])
            e = jnp.exp2(sc - jnp.max(sc, axis=-1, keepdims=True))
            l = jnp.sum(e, axis=-1, keepdims=True)
            v = kv[:, MEM_INNER + h * HEAD_DIM:MEM_INNER + (h + 1) * HEAD_DIM]
            heads.append((_dot(e.astype(BF16), v) / l).astype(BF16))
        o = jnp.concatenate(heads, axis=1)
        y = ALPHA * x + _dot(o, wo_ref[...])
        o_ref[sl, :] = _layer_norm(y, g_ref[...], b_ref[...])


def _mem_attention(x, kv, w_q, w_o, ln_g, ln_b, layer, s, *, tm=512):
    m = x.shape[0]
    per_seq = s // tm
    return pl.pallas_call(
        functools.partial(_mem_body, n_split=1),
        out_shape=jax.ShapeDtypeStruct((m, D_MODEL), F32),
        grid=(m // tm,),
        in_specs=[
            pl.BlockSpec((tm, D_MODEL), lambda i: (i, 0)),
            pl.BlockSpec((MEM_TOKENS, 2 * MEM_INNER), lambda i: (i // per_seq, 0)),
            pl.BlockSpec((None, D_MODEL, MEM_INNER), lambda i: (layer, 0, 0)),
            pl.BlockSpec((None, MEM_INNER, D_MODEL), lambda i: (layer, 0, 0)),
            pl.BlockSpec((None, None, 1, D_MODEL), lambda i: (layer, 2, 0, 0)),
            pl.BlockSpec((None, None, 1, D_MODEL), lambda i: (layer, 2, 0, 0)),
        ],
        out_specs=pl.BlockSpec((tm, D_MODEL), lambda i: (i, 0)),
        compiler_params=_params(("parallel",)),
        name="mem_attn",
    )(x, kv, w_q, w_o, ln_g, ln_b)


def _t5_bucket_np(rel):
    half = NUM_BUCKETS // 2
    max_exact = half // 2
    rel = np.asarray(rel, np.int32)
    ret = np.where(rel > 0, half, 0)
    n = np.abs(rel)
    ratio = np.log(np.maximum(n, 1).astype(np.float32) / np.float32(max_exact)) / np.float32(math.log(MAX_DISTANCE / max_exact))
    large = max_exact + (ratio * np.float32(half - max_exact)).astype(np.int32)
    large = np.minimum(large, half - 1)
    return (ret + np.where(n < max_exact, n, large)).astype(np.int32)


def _lookup_bias(table, buckets):
    n_heads = table.shape[1]
    b = jnp.asarray(buckets)[None]
    tab = table * LOG2E
    out = jnp.zeros((n_heads,) + buckets.shape, F32)
    for k in range(NUM_BUCKETS):
        out = jnp.where(b == k, tab[k].reshape((n_heads,) + (1,) * buckets.ndim), out)
    return out


def _diff_bias_tiles(table_a, t, tk):
    assert tk >= MAX_DISTANCE and t % tk == 0
    ii = np.arange(t)[:, None]
    jj = np.arange(tk)[None, :]
    buckets = np.stack([_t5_bucket_np(d * tk + jj - ii) for d in range(-2, t // tk + 2)])
    return _lookup_bias(table_a, buckets)


def _window_bias_tiles(table_b):
    g = B_HEADS // B_KV_HEADS
    rel = np.stack([(np.arange(3 * WINDOW) - p * WINDOW)[None, :] - np.arange(WINDOW)[:, None] for p in range(3)])
    bias = _lookup_bias(table_b, _t5_bucket_np(rel))
    bias = jnp.where(jnp.asarray(np.abs(rel) <= WINDOW)[None], bias, NEG_INF)
    bias = bias.reshape(B_KV_HEADS, g, 3, WINDOW, 3 * WINDOW)
    return jnp.transpose(bias, (2, 0, 4, 1, 3)).reshape(3, B_KV_HEADS, 3 * WINDOW, g * WINDOW)


def _rope_tables(s):
    n_rows = s // GRID_W
    row = np.repeat(np.arange(n_rows), GRID_W).astype(np.float32)
    col = np.tile(np.arange(GRID_W), n_rows).astype(np.float32)
    axis_dims = HEAD_DIM // 2
    inv = ROPE_THETA ** (-jnp.arange(0, axis_dims, 2, dtype=F32) / axis_dims)
    ang = jnp.concatenate([row[:, None] * inv[None, :], col[:, None] * inv[None, :]], axis=-1)
    cos, sin = jnp.cos(ang), jnp.sin(ang)
    return jnp.concatenate([cos, cos], axis=-1), jnp.concatenate([-sin, sin], axis=-1)


def _run_trunk(x, mem, wts, table_a, bias_b):
    n_seq, s, _ = x.shape
    x = x.reshape(n_seq * s, D_MODEL)
    mem = mem.reshape(n_seq * MEM_TOKENS, D_MODEL)
    cos2, sin2 = _rope_tables(s)
    bias_a = _diff_bias_tiles(table_a, _diff_rows(s), DIFF_TK)
    for i in range(DEPTH):
        lambda_init = 0.8 - 0.6 * math.exp(-0.3 * i)
        x = _ffn(x, wts["w_ffn_gu"], wts["w_ffn_d"], wts["ln_g"], wts["ln_b"], i, 0, 0)
        qkv = _qkv(x, wts["w_in"], wts["qk_norm_g"], cos2, sin2, i)
        ya = _diff_attention(qkv, bias_a, wts["lambda_qk"], wts["subln_g"], i, n_seq, s, lambda_init)
        yb = _window_attention(qkv, bias_b, wts["sink_logits"], i, n_seq, s)
        yc = _axial_attention(qkv, n_seq, s)
        x = _merge(x, ya, yb, yc, wts["w_in"], wts["w_branch"], wts["w_out"], wts["ln_g"], wts["ln_b"], i)
        kv = _memkv(mem, wts["w_mem_kv"], i)
        x = _mem_attention(x, kv, wts["w_mem_q"], wts["w_mem_o"], wts["ln_g"], wts["ln_b"], i, s)
        x = _ffn(x, wts["w_ffn_gu"], wts["w_ffn_d"], wts["ln_g"], wts["ln_b"], i, 1, 3)
    return x.reshape(n_seq, s, D_MODEL)


def kernel(x_prompt, x_sample, mem_prompt, mem_sample, rel_bias_table, w_in, w_branch, w_out, lambda_qk, subln_g, sink_logits, qk_norm_g, w_mem_q, w_mem_kv, w_mem_o, w_ffn_gu, w_ffn_d, ln_g, ln_b):
    wts = {
        "w_in": w_in.astype(BF16),
        "w_branch": w_branch.astype(BF16),
        "w_out": w_out.astype(BF16),
        "w_mem_q": w_mem_q.astype(BF16),
        "w_mem_kv": w_mem_kv.astype(BF16),
        "w_mem_o": w_mem_o.astype(BF16),
        "w_ffn_gu": w_ffn_gu.astype(BF16),
        "w_ffn_d": w_ffn_d.astype(BF16),
        "lambda_qk": lambda_qk,
        "subln_g": subln_g.reshape(DEPTH, 1, A_V),
        "sink_logits": sink_logits,
        "qk_norm_g": qk_norm_g,
        "ln_g": ln_g.reshape(DEPTH, 4, 1, D_MODEL),
        "ln_b": ln_b.reshape(DEPTH, 4, 1, D_MODEL),
    }
    table_a = rel_bias_table[:, :A_HEADS]
    bias_b = _window_bias_tiles(rel_bias_table[:, A_HEADS:])
    y_prompt = _run_trunk(x_prompt, mem_prompt, wts, table_a, bias_b)
    y_sample = _run_trunk(x_sample, mem_sample, wts, table_a, bias_b)
    return (y_prompt, y_sample)
```

```python
import functools
import math

import numpy as np
import jax
import jax.numpy as jnp
from jax import lax
from jax.experimental import pallas as pl
from jax.experimental.pallas import tpu as pltpu

D_MODEL = 2048
DEPTH = 4
HEAD_DIM = 128
A_HEADS = 4
A_V = 2 * HEAD_DIM
B_HEADS = 8
B_KV_HEADS = 2
WINDOW = 128
C_HEADS = 8
C_KV_HEADS = 2
GRID_W = 64
ROPE_THETA = 10000.0
NUM_BUCKETS = 32
MAX_DISTANCE = 128
MEM_TOKENS = 256
MEM_HEADS = 4
MEM_INNER = MEM_HEADS * HEAD_DIM
D_FF = 5632
N_BRANCH = 3
BRANCH_WIDTH = 1024
QKV_WIDTH = 6144
ALPHA = (2 * DEPTH) ** 0.25
LN_EPS = 1e-5
RMS_EPS = 1e-6
NEG_INF = -1e30
LOG2E = math.log2(math.e)
QSCALE = HEAD_DIM ** -0.5 * LOG2E

QA_OFF, KA_OFF, VA_OFF = 0, 1024, 2048
QB_OFF, KB_OFF, VB_OFF = 3072, 4096, 4352
QC_OFF, KC_OFF, VC_OFF = 4608, 5632, 5888

V7X_VMEM_BYTES = 64 * 1024 * 1024
VMEM_LIMIT = 56 * 1024 * 1024
F32 = jnp.float32
BF16 = jnp.bfloat16


def _params(semantics):
    return pltpu.CompilerParams(dimension_semantics=semantics, vmem_limit_bytes=VMEM_LIMIT)


def _layer_norm(y, g, b):
    mu = jnp.mean(y, axis=-1, keepdims=True)
    d = y - mu
    var = jnp.mean(d * d, axis=-1, keepdims=True)
    return d * lax.rsqrt(var + LN_EPS) * g + b


def _dot(a, b):
    return jnp.dot(a, b, preferred_element_type=F32)


def _dot_nt(a, b):
    return lax.dot_general(a, b, (((1,), (1,)), ((), ())), preferred_element_type=F32)


def _lane_fold(x, op):
    parts = [x[:, j * 128:(j + 1) * 128] for j in range(x.shape[1] // 128)]
    return functools.reduce(op, parts)


def _exp2_minus(x, col):
    parts = [jnp.exp2(x[:, j * 128:(j + 1) * 128] - col) for j in range(x.shape[1] // 128)]
    return parts[0] if len(parts) == 1 else jnp.concatenate(parts, axis=1)


def _lane_bcast(col):
    return jnp.broadcast_to(col, (col.shape[0], 128))


FFN_TF = 1024


def _ffn_body(idx_ref, x_ref, wgu_hbm, wd_hbm, g_ref, b_ref, o_ref, xb_ref, acc_ref, wg_buf, wu_buf, wd_buf, sem,
              *, tf, n_tiles):
    i = pl.program_id(0)
    layer = idx_ref[0]
    which = idx_ref[1]
    chunks = [(off, min(tf, D_FF - off)) for off in range(0, D_FF, tf)]
    n_c = len(chunks)

    def chunk_copies(c):
        off, w = chunks[c]
        slot = c % 2
        return (
            pltpu.make_async_copy(wgu_hbm.at[layer, which, :, pl.ds(off, w)], wg_buf.at[slot, :, pl.ds(0, w)], sem.at[0, slot]),
            pltpu.make_async_copy(wgu_hbm.at[layer, which, :, pl.ds(D_FF + off, w)], wu_buf.at[slot, :, pl.ds(0, w)], sem.at[1, slot]),
            pltpu.make_async_copy(wd_hbm.at[layer, which, pl.ds(off, w), :], wd_buf.at[slot, pl.ds(0, w), :], sem.at[2, slot]),
        )

    def start(c):
        for cp in chunk_copies(c):
            cp.start()

    def wait(c):
        for cp in chunk_copies(c):
            cp.wait()

    @pl.when(i == 0)
    def _():
        start(0)

    xb_ref[...] = x_ref[...].astype(BF16)
    acc_ref[...] = jnp.zeros_like(acc_ref)
    wait(0)
    xb = xb_ref[...]
    for c in range(n_c):
        w = chunks[c][1]
        slot = c % 2
        if c + 1 < n_c:
            start(c + 1)
        else:
            @pl.when(i + 1 < n_tiles)
            def _():
                start(0)
        h = _dot(xb, wg_buf[slot, :, 0:w])
        u = _dot(xb, wu_buf[slot, :, 0:w])
        a = (h * jax.nn.sigmoid(h) * u).astype(BF16)
        acc_ref[...] += _dot(a, wd_buf[slot, 0:w, :])
        if c + 1 < n_c:
            wait(c + 1)

    y = ALPHA * x_ref[...] + 0.5 * acc_ref[...]
    o_ref[...] = _layer_norm(y, g_ref[...], b_ref[...])


def _ffn(x, w_gu, w_d, ln_g, ln_b, layer, which, ln_idx, *, tm=512, tf=FFN_TF):
    m = x.shape[0]
    n_tiles = m // tm
    assert pl.cdiv(D_FF, tf) % 2 == 0, "the next tile's chunk 0 must reuse the slot the last chunk leaves free"
    idx = jnp.array([layer, which, ln_idx], jnp.int32)
    ln_spec = pl.BlockSpec((None, None, 1, D_MODEL), lambda i, idx: (idx[0], idx[2], 0, 0))
    grid_spec = pltpu.PrefetchScalarGridSpec(
        num_scalar_prefetch=1,
        grid=(n_tiles,),
        in_specs=[
            pl.BlockSpec((tm, D_MODEL), lambda i, idx: (i, 0)),
            pl.BlockSpec(memory_space=pl.ANY),
            pl.BlockSpec(memory_space=pl.ANY),
            ln_spec, ln_spec,
        ],
        out_specs=pl.BlockSpec((tm, D_MODEL), lambda i, idx: (i, 0)),
        scratch_shapes=[
            pltpu.VMEM((tm, D_MODEL), BF16), pltpu.VMEM((tm, D_MODEL), F32),
            pltpu.VMEM((2, D_MODEL, tf), BF16), pltpu.VMEM((2, D_MODEL, tf), BF16),
            pltpu.VMEM((2, tf, D_MODEL), BF16),
            pltpu.SemaphoreType.DMA((3, 2)),
        ],
    )
    return pl.pallas_call(
        functools.partial(_ffn_body, tf=tf, n_tiles=n_tiles),
        out_shape=jax.ShapeDtypeStruct((m, D_MODEL), F32),
        grid_spec=grid_spec,
        compiler_params=_params(("arbitrary",)),
        name="ffn",
    )(idx, x, w_gu, w_d, ln_g, ln_b)


def _qkv_body(x_ref, w_ref, gn_ref, cos_ref, sin_ref, o_ref, *, tn):
    xb = x_ref[...].astype(BF16)
    cos = cos_ref[...]
    sin = sin_ref[...]
    starts = [j * tn for j in range(QKV_WIDTH // tn)]
    for c0 in sorted(starts, key=lambda c: not (QC_OFF <= c < VC_OFF)):
        acc = _dot(xb, w_ref[:, c0:c0 + tn])
        is_q = (QA_OFF <= c0 < KA_OFF) or (QB_OFF <= c0 < KB_OFF) or (QC_OFF <= c0 < KC_OFF)
        if QC_OFF <= c0 < VC_OFF:
            g = gn_ref[0:1, :] if c0 < KC_OFF else gn_ref[1:2, :]
            for hh in range(tn // HEAD_DIM):
                a = acc[:, hh * HEAD_DIM:(hh + 1) * HEAD_DIM]
                r = lax.rsqrt(jnp.mean(a * a, axis=-1, keepdims=True) + RMS_EPS)
                y = a * r * g
                y = y * cos + pltpu.roll(y, HEAD_DIM // 2, 1) * sin
                if is_q:
                    y = y * QSCALE
                o_ref[:, c0 + hh * HEAD_DIM:c0 + (hh + 1) * HEAD_DIM] = y.astype(BF16)
        else:
            if is_q:
                acc = acc * QSCALE
            o_ref[:, c0:c0 + tn] = acc.astype(BF16)


def _qkv(x, w_qkv, qk_norm_g, cos2, sin2, layer, *, tm=512, tn=256):
    m = x.shape[0]
    s = cos2.shape[0]
    n_pos = s // tm
    return pl.pallas_call(
        functools.partial(_qkv_body, tn=tn),
        out_shape=jax.ShapeDtypeStruct((m, QKV_WIDTH), BF16),
        grid=(m // tm,),
        in_specs=[
            pl.BlockSpec((tm, D_MODEL), lambda i: (i, 0)),
            pl.BlockSpec((None, D_MODEL, QKV_WIDTH), lambda i: (layer, 0, 0), pipeline_mode=pl.Buffered(1)),
            pl.BlockSpec((None, 2, HEAD_DIM), lambda i: (layer, 0, 0)),
            pl.BlockSpec((tm, HEAD_DIM), lambda i: (i % n_pos, 0)),
            pl.BlockSpec((tm, HEAD_DIM), lambda i: (i % n_pos, 0)),
        ],
        out_specs=pl.BlockSpec((tm, QKV_WIDTH), lambda i: (i, 0)),
        compiler_params=_params(("parallel",)),
        name="qkv",
    )(x, w_qkv, qk_norm_g, cos2, sin2)


def _diff_body(q_ref, k_ref, v_ref, bias_ref, lam_ref, sg_ref, o_ref, s1_ref, s2_ref, m_ref, *, t, tk, nc, nq, n_items, lambda_init):
    g = pl.program_id(0)
    i = jnp.minimum(g, n_items - 1) % nq

    @pl.when(g == 0)
    def _():
        s1_ref[1] = jnp.zeros(s1_ref.shape[1:], F32)
        s2_ref[1] = jnp.zeros(s2_ref.shape[1:], F32)
        m_ref[1] = jnp.zeros(m_ref.shape[1:], F32)

    def step(cur, prv):
        q = q_ref[...]
        q1 = q[:, :HEAD_DIM]
        q2 = q[:, HEAD_DIM:]
        m1p = m_ref[prv, 0]
        m2p = m_ref[prv, 1]
        m1 = m2 = acc = l1 = l2 = None
        for c in range(nc):
            kc = k_ref[c * tk:(c + 1) * tk, :]
            b = bias_ref[jnp.clip(c - i * (t // tk), -2, t // tk + 1) + 2]
            s1 = _dot_nt(q1, kc[:, :HEAD_DIM]) + b
            s2 = _dot_nt(q2, kc[:, HEAD_DIM:]) + b
            s1_ref[cur, c] = s1
            s2_ref[cur, c] = s2
            f1 = _lane_fold(s1, jnp.maximum)
            f2 = _lane_fold(s2, jnp.maximum)
            m1 = f1 if m1 is None else jnp.maximum(m1, f1)
            m2 = f2 if m2 is None else jnp.maximum(m2, f2)

            e1 = _exp2_minus(s1_ref[prv, c], m1p)
            e2 = _exp2_minus(s2_ref[prv, c], m2p)
            e = jnp.concatenate([e1.astype(BF16), e2.astype(BF16)], axis=0)
            d = _dot(e, v_ref[c * tk:(c + 1) * tk, :])
            f1 = _lane_fold(e1, jnp.add)
            f2 = _lane_fold(e2, jnp.add)
            acc = d if acc is None else acc + d
            l1 = f1 if l1 is None else l1 + f1
            l2 = f2 if l2 is None else l2 + f2
        m_ref[cur, 0] = _lane_bcast(jnp.max(m1, axis=-1, keepdims=True))
        m_ref[cur, 1] = _lane_bcast(jnp.max(m2, axis=-1, keepdims=True))

        l1 = jnp.sum(l1, axis=-1, keepdims=True)
        l2 = jnp.sum(l2, axis=-1, keepdims=True)
        lam = lam_ref[...]
        lam_full = (jnp.exp(jnp.sum(lam[0:1] * lam[1:2], axis=-1, keepdims=True))
                    - jnp.exp(jnp.sum(lam[2:3] * lam[3:4], axis=-1, keepdims=True)) + lambda_init)
        o = acc[0:t, :] / l1 - lam_full * (acc[t:2 * t, :] / l2)
        o = o * lax.rsqrt(jnp.mean(o * o, axis=-1, keepdims=True) + RMS_EPS) * sg_ref[...]
        o_ref[...] = (o * (1.0 - lambda_init)).astype(BF16)

    @pl.when(g % 2 == 0)
    def _():
        step(0, 1)

    @pl.when(g % 2 == 1)
    def _():
        step(1, 0)


def _pipeline_items(n_items, n_heads, nq):
    def scored(g):
        it = jnp.minimum(g, n_items - 1)
        return it // (n_heads * nq), (it // nq) % n_heads, it % nq

    def finished(g):
        it = jnp.maximum(g - 1, 0)
        return it // (n_heads * nq), (it // nq) % n_heads, it % nq

    return scored, finished


DIFF_TK = 256


def _diff_rows(s):
    return 512 if s <= 2048 else 256


def _diff_attention(qkv, bias_a, lambda_qk, subln_g, layer, n_seq, s, lambda_init, *, tk=DIFF_TK):
    t = _diff_rows(s)
    m = qkv.shape[0]
    nc = s // tk
    nq = s // t
    n_tiles = t // tk + 4
    n_items = n_seq * A_HEADS * nq
    scored, finished = _pipeline_items(n_items, A_HEADS, nq)

    def q_map(g):
        b, h, i = scored(g)
        return (b * nq + i, QA_OFF // A_V + h)

    def k_map(g):
        b, h, _ = scored(g)
        return (b, KA_OFF // A_V + h)

    def v_map(g):
        b, h, _ = finished(g)
        return (b, VA_OFF // A_V + h)

    def o_map(g):
        b, h, i = finished(g)
        return (b * nq + i, h)

    return pl.pallas_call(
        functools.partial(_diff_body, t=t, tk=tk, nc=nc, nq=nq, n_items=n_items, lambda_init=lambda_init),
        out_shape=jax.ShapeDtypeStruct((m, BRANCH_WIDTH), BF16),
        grid=(n_items + 1,),
        in_specs=[
            pl.BlockSpec((t, A_V), q_map),
            pl.BlockSpec((s, A_V), k_map),
            pl.BlockSpec((s, A_V), v_map),
            pl.BlockSpec((None, n_tiles, t, tk), lambda g: (scored(g)[1], 0, 0, 0)),
            pl.BlockSpec((None, 4, HEAD_DIM), lambda g: (layer, 0, 0)),
            pl.BlockSpec((None, 1, A_V), lambda g: (layer, 0, 0)),
        ],
        out_specs=pl.BlockSpec((t, A_V), o_map),
        scratch_shapes=[pltpu.VMEM((2, nc, t, tk), F32), pltpu.VMEM((2, nc, t, tk), F32), pltpu.VMEM((2, 2, t, 128), F32)],
        compiler_params=_params(("arbitrary",)),
        name="diff_attn",
    )(qkv, qkv, qkv, bias_a, lambda_qk, subln_g)


def _axial_body(q_ref, k_ref, v_ref, o_ref, s_ref, m_ref, *, t, tk, nc):
    gq = C_HEADS // C_KV_HEADS
    g = pl.program_id(0)
    half = gq // 2

    @pl.when(g == 0)
    def _():
        s_ref[1] = jnp.zeros(s_ref.shape[1:], F32)
        m_ref[1] = jnp.zeros(m_ref.shape[1:], F32)

    def step(cur, prv):
        q = q_ref[...]
        qs = [jnp.concatenate([q[:, h * HEAD_DIM:(h + 1) * HEAD_DIM] for h in range(r * half, (r + 1) * half)], axis=0)
              for r in range(2)]
        mp = [m_ref[prv, r] for r in range(2)]
        m = [None, None]
        acc = [None, None]
        l = [None, None]
        for c in range(nc):
            kc = k_ref[c * tk:(c + 1) * tk, :]
            vc = v_ref[c * tk:(c + 1) * tk, :]
            for r in range(2):
                sc = _dot_nt(qs[r], kc)
                s_ref[cur, r, c] = sc
                f = _lane_fold(sc, jnp.maximum)
                m[r] = f if m[r] is None else jnp.maximum(m[r], f)

                e = _exp2_minus(s_ref[prv, r, c], mp[r])
                d = _dot(e.astype(BF16), vc)
                f = _lane_fold(e, jnp.add)
                acc[r] = d if acc[r] is None else acc[r] + d
                l[r] = f if l[r] is None else l[r] + f
        for r in range(2):
            m_ref[cur, r] = _lane_bcast(jnp.max(m[r], axis=-1, keepdims=True))
            o = acc[r] / jnp.sum(l[r], axis=-1, keepdims=True)
            for hh in range(half):
                h = r * half + hh
                o_ref[:, h * HEAD_DIM:(h + 1) * HEAD_DIM] = o[hh * t:(hh + 1) * t, :].astype(BF16)

    @pl.when(g % 2 == 0)
    def _():
        step(0, 1)

    @pl.when(g % 2 == 1)
    def _():
        step(1, 0)


def _axial_attention(qkv, n_seq, s, *, t=256, tk=256):
    m = qkv.shape[0]
    gq = C_HEADS // C_KV_HEADS
    gw = gq * HEAD_DIM
    nq = s // t
    nc = s // tk
    n_items = n_seq * C_KV_HEADS * nq
    scored, finished = _pipeline_items(n_items, C_KV_HEADS, nq)

    def q_map(g):
        b, kv, i = scored(g)
        return (b * nq + i, QC_OFF // gw + kv)

    def k_map(g):
        b, kv, _ = scored(g)
        return (b, KC_OFF // HEAD_DIM + kv)

    def v_map(g):
        b, kv, _ = finished(g)
        return (b, VC_OFF // HEAD_DIM + kv)

    def o_map(g):
        b, kv, i = finished(g)
        return (b * nq + i, kv)

    return pl.pallas_call(
        functools.partial(_axial_body, t=t, tk=tk, nc=nc),
        out_shape=jax.ShapeDtypeStruct((m, BRANCH_WIDTH), BF16),
        grid=(n_items + 1,),
        in_specs=[
            pl.BlockSpec((t, gw), q_map),
            pl.BlockSpec((s, HEAD_DIM), k_map),
            pl.BlockSpec((s, HEAD_DIM), v_map),
        ],
        out_specs=pl.BlockSpec((t, gw), o_map),
        scratch_shapes=[pltpu.VMEM((2, 2, nc, gq // 2 * t, tk), F32), pltpu.VMEM((2, 2, gq // 2 * t, 128), F32)],
        compiler_params=_params(("arbitrary",)),
        name="axial_attn",
    )(qkv, qkv, qkv)


def _window_body(sink_ref, q_ref, k_ref, v_ref, bias_ref, o_ref, *, nb):
    g = B_HEADS // B_KV_HEADS
    t = WINDOW
    kv = pl.program_id(1)
    sink = jnp.concatenate([jnp.full((1, t), sink_ref[kv * g + h] * LOG2E, F32) for h in range(g)], axis=1)

    def block(i, carry):
        r0 = pl.multiple_of(i * t, t)
        w0 = pl.multiple_of(jnp.clip(i - 1, 0, nb - 3) * t, t)
        tile = jnp.where(i == 0, 0, jnp.where(i == nb - 1, 2, 1))
        q = q_ref[pl.ds(r0, t), :]
        qs = jnp.concatenate([q[:, h * HEAD_DIM:(h + 1) * HEAD_DIM] for h in range(g)], axis=0)
        st = _dot_nt(k_ref[pl.ds(w0, 3 * t), :], qs) + bias_ref[tile]
        mx = jnp.maximum(jnp.max(st, axis=0, keepdims=True), sink)
        e = jnp.exp2(st - mx)
        l = jnp.sum(e, axis=0, keepdims=True) + jnp.exp2(sink - mx)
        p = (e * (1.0 / l)).astype(BF16)
        ot = lax.dot_general(v_ref[pl.ds(w0, 3 * t), :], p, (((0,), (0,)), ((), ())),
                             preferred_element_type=F32)
        for h in range(g):
            o_ref[pl.ds(r0, t), h * HEAD_DIM:(h + 1) * HEAD_DIM] = ot[:, h * t:(h + 1) * t].T.astype(BF16)
        return carry

    lax.fori_loop(0, nb, block, 0, unroll=4)


def _window_attention(qkv, bias_b, sink_logits, layer, n_seq, s):
    m = qkv.shape[0]
    g = B_HEADS // B_KV_HEADS
    gw = g * HEAD_DIM
    t = WINDOW
    nb = s // t
    grid_spec = pltpu.PrefetchScalarGridSpec(
        num_scalar_prefetch=1,
        grid=(n_seq, B_KV_HEADS),
        in_specs=[
            pl.BlockSpec((s, gw), lambda b, kv, sink: (b, QB_OFF // gw + kv)),
            pl.BlockSpec((s, HEAD_DIM), lambda b, kv, sink: (b, KB_OFF // HEAD_DIM + kv)),
            pl.BlockSpec((s, HEAD_DIM), lambda b, kv, sink: (b, VB_OFF // HEAD_DIM + kv)),
            pl.BlockSpec((3, None, 3 * t, g * t), lambda b, kv, sink: (0, kv, 0, 0)),
        ],
        out_specs=pl.BlockSpec((s, gw), lambda b, kv, sink: (b, kv)),
    )
    return pl.pallas_call(
        functools.partial(_window_body, nb=nb),
        out_shape=jax.ShapeDtypeStruct((m, BRANCH_WIDTH), BF16),
        grid_spec=grid_spec,
        compiler_params=_params(("parallel", "parallel")),
        name="window_attn",
    )(sink_logits[layer], qkv, qkv, qkv, bias_b)


def _gate_body(x_ref, ya_ref, yb_ref, yc_ref, wga_ref, wgb_ref, wgc_ref, wb_ref, o_ref, xb_ref):
    @pl.when(pl.program_id(1) == 0)
    def _():
        xb_ref[...] = x_ref[...].astype(BF16)

    xb = xb_ref[...]
    merged = None
    for n, (y_ref, wg_ref) in enumerate(((ya_ref, wga_ref), (yb_ref, wgb_ref), (yc_ref, wgc_ref))):
        gate = jax.nn.sigmoid(_dot(xb, wg_ref[...]))
        term = gate * _dot(y_ref[...], wb_ref[n])
        merged = term if merged is None else merged + term
    o_ref[...] = merged.astype(BF16)


def _outproj_body(x_ref, m_ref, wo_ref, g_ref, b_ref, o_ref, *, n_split):
    rows = x_ref.shape[0] // n_split
    for r in range(n_split):
        sl = slice(r * rows, (r + 1) * rows)
        y = ALPHA * x_ref[sl, :] + _dot(m_ref[sl, :], wo_ref[...])
        o_ref[sl, :] = _layer_norm(y, g_ref[...], b_ref[...])


def _merge(x, ya, yb, yc, w_in, w_branch, w_out, ln_g, ln_b, layer, *, tm=512, tc=512):
    m = x.shape[0]
    yspec = pl.BlockSpec((tm, BRANCH_WIDTH), lambda i, j: (i, 0))

    def gate_spec(n):
        return pl.BlockSpec((None, D_MODEL, tc), lambda i, j: (layer, 0, (QKV_WIDTH + n * D_MODEL) // tc + j))

    merged = pl.pallas_call(
        _gate_body,
        out_shape=jax.ShapeDtypeStruct((m, D_MODEL), BF16),
        grid=(m // tm, D_MODEL // tc),
        in_specs=[
            pl.BlockSpec((tm, D_MODEL), lambda i, j: (i, 0)),
            yspec, yspec, yspec,
            gate_spec(0), gate_spec(1), gate_spec(2),
            pl.BlockSpec((None, N_BRANCH, BRANCH_WIDTH, tc), lambda i, j: (layer, 0, 0, j)),
        ],
        out_specs=pl.BlockSpec((tm, tc), lambda i, j: (i, j)),
        scratch_shapes=[pltpu.VMEM((tm, D_MODEL), BF16)],
        compiler_params=_params(("parallel", "arbitrary")),
        name="gate_merge",
    )(x, ya, yb, yc, w_in, w_in, w_in, w_branch)
    return pl.pallas_call(
        functools.partial(_outproj_body, n_split=2),
        out_shape=jax.ShapeDtypeStruct((m, D_MODEL), F32),
        grid=(m // tm,),
        in_specs=[
            pl.BlockSpec((tm, D_MODEL), lambda i: (i, 0)),
            pl.BlockSpec((tm, D_MODEL), lambda i: (i, 0)),
            pl.BlockSpec((None, D_MODEL, D_MODEL), lambda i: (layer, 0, 0), pipeline_mode=pl.Buffered(1)),
            pl.BlockSpec((None, None, 1, D_MODEL), lambda i: (layer, 1, 0, 0)),
            pl.BlockSpec((None, None, 1, D_MODEL), lambda i: (layer, 1, 0, 0)),
        ],
        out_specs=pl.BlockSpec((tm, D_MODEL), lambda i: (i, 0)),
        compiler_params=_params(("parallel",)),
        name="out_proj",
    )(x, merged, w_out, ln_g, ln_b)


def _memkv_body(m_ref, w_ref, o_ref):
    o_ref[...] = _dot(m_ref[...].astype(BF16), w_ref[...]).astype(BF16)


def _memkv(mem, w_kv, layer):
    rows = mem.shape[0]
    return pl.pallas_call(
        _memkv_body,
        out_shape=jax.ShapeDtypeStruct((rows, 2 * MEM_INNER), BF16),
        grid=(rows // MEM_TOKENS,),
        in_specs=[
            pl.BlockSpec((MEM_TOKENS, D_MODEL), lambda i: (i, 0)),
            pl.BlockSpec((None, D_MODEL, 2 * MEM_INNER), lambda i: (layer, 0, 0)),
        ],
        out_specs=pl.BlockSpec((MEM_TOKENS, 2 * MEM_INNER), lambda i: (i, 0)),
        compiler_params=_params(("parallel",)),
        name="memkv",
    )(mem, w_kv)


def _mem_body(x_ref, kv_ref, wq_ref, wo_ref, g_ref, b_ref, o_ref, *, n_split):
    kv = kv_ref[...]
    rows = x_ref.shape[0] // n_split
    for r in range(n_split):
        sl = slice(r * rows, (r + 1) * rows)
        x = x_ref[sl, :]
        q = (_dot(x.astype(BF16), wq_ref[...]) * QSCALE).astype(BF16)
        heads = []
        for h in range(MEM_HEADS):
            sc = _dot_nt(q[:, h * HEAD_DIM:(h + 1) * HEAD_DIM], kv[:, h * HEAD_DIM:(h + 1) * HEAD_DIM])
            e = jnp.exp2(sc - jnp.max(sc, axis=-1, keepdims=True))
            l = jnp.sum(e, axis=-1, keepdims=True)
            v = kv[:, MEM_INNER + h * HEAD_DIM:MEM_INNER + (h + 1) * HEAD_DIM]
            heads.append((_dot(e.astype(BF16), v) / l).astype(BF16))
        o = jnp.concatenate(heads, axis=1)
        y = ALPHA * x + _dot(o, wo_ref[...])
        o_ref[sl, :] = _layer_norm(y, g_ref[...], b_ref[...])


def _mem_attention(x, kv, w_q, w_o, ln_g, ln_b, layer, s, *, tm=512):
    m = x.shape[0]
    per_seq = s // tm
    return pl.pallas_call(
        functools.partial(_mem_body, n_split=1),
        out_shape=jax.ShapeDtypeStruct((m, D_MODEL), F32),
        grid=(m // tm,),
        in_specs=[
            pl.BlockSpec((tm, D_MODEL), lambda i: (i, 0)),
            pl.BlockSpec((MEM_TOKENS, 2 * MEM_INNER), lambda i: (i // per_seq, 0)),
            pl.BlockSpec((None, D_MODEL, MEM_INNER), lambda i: (layer, 0, 0)),
            pl.BlockSpec((None, MEM_INNER, D_MODEL), lambda i: (layer, 0, 0)),
            pl.BlockSpec((None, None, 1, D_MODEL), lambda i: (layer, 2, 0, 0)),
            pl.BlockSpec((None, None, 1, D_MODEL), lambda i: (layer, 2, 0, 0)),
        ],
        out_specs=pl.BlockSpec((tm, D_MODEL), lambda i: (i, 0)),
        compiler_params=_params(("parallel",)),
        name="mem_attn",
    )(x, kv, w_q, w_o, ln_g, ln_b)


def _t5_bucket_np(rel):
    half = NUM_BUCKETS // 2
    max_exact = half // 2
    rel = np.asarray(rel, np.int32)
    ret = np.where(rel > 0, half, 0)
    n = np.abs(rel)
    ratio = np.log(np.maximum(n, 1).astype(np.float32) / np.float32(max_exact)) / np.float32(math.log(MAX_DISTANCE / max_exact))
    large = max_exact + (ratio * np.float32(half - max_exact)).astype(np.int32)
    large = np.minimum(large, half - 1)
    return (ret + np.where(n < max_exact, n, large)).astype(np.int32)


def _lookup_bias(table, buckets):
    n_heads = table.shape[1]
    b = jnp.asarray(buckets)[None]
    tab = table * LOG2E
    out = jnp.zeros((n_heads,) + buckets.shape, F32)
    for k in range(NUM_BUCKETS):
        out = jnp.where(b == k, tab[k].reshape((n_heads,) + (1,) * buckets.ndim), out)
    return out


def _diff_bias_tiles(table_a, t, tk):
    assert tk >= MAX_DISTANCE and t % tk == 0
    ii = np.arange(t)[:, None]
    jj = np.arange(tk)[None, :]
    buckets = np.stack([_t5_bucket_np(d * tk + jj - ii) for d in range(-2, t // tk + 2)])
    return _lookup_bias(table_a, buckets)


def _window_bias_tiles(table_b):
    g = B_HEADS // B_KV_HEADS
    rel = np.stack([(np.arange(3 * WINDOW) - p * WINDOW)[None, :] - np.arange(WINDOW)[:, None] for p in range(3)])
    bias = _lookup_bias(table_b, _t5_bucket_np(rel))
    bias = jnp.where(jnp.asarray(np.abs(rel) <= WINDOW)[None], bias, NEG_INF)
    bias = bias.reshape(B_KV_HEADS, g, 3, WINDOW, 3 * WINDOW)
    return jnp.transpose(bias, (2, 0, 4, 1, 3)).reshape(3, B_KV_HEADS, 3 * WINDOW, g * WINDOW)


def _rope_tables(s):
    n_rows = s // GRID_W
    row = np.repeat(np.arange(n_rows), GRID_W).astype(np.float32)
    col = np.tile(np.arange(GRID_W), n_rows).astype(np.float32)
    axis_dims = HEAD_DIM // 2
    inv = ROPE_THETA ** (-jnp.arange(0, axis_dims, 2, dtype=F32) / axis_dims)
    ang = jnp.concatenate([row[:, None] * inv[None, :], col[:, None] * inv[None, :]], axis=-1)
    cos, sin = jnp.cos(ang), jnp.sin(ang)
    return jnp.concatenate([cos, cos], axis=-1), jnp.concatenate([-sin, sin], axis=-1)


def _run_trunk(x, mem, wts, table_a, bias_b):
    n_seq, s, _ = x.shape
    x = x.reshape(n_seq * s, D_MODEL)
    mem = mem.reshape(n_seq * MEM_TOKENS, D_MODEL)
    cos2, sin2 = _rope_tables(s)
    bias_a = _diff_bias_tiles(table_a, _diff_rows(s), DIFF_TK)
    for i in range(DEPTH):
        lambda_init = 0.8 - 0.6 * math.exp(-0.3 * i)
        x = _ffn(x, wts["w_ffn_gu"], wts["w_ffn_d"], wts["ln_g"], wts["ln_b"], i, 0, 0)
        qkv = _qkv(x, wts["w_in"], wts["qk_norm_g"], cos2, sin2, i)
        ya = _diff_attention(qkv, bias_a, wts["lambda_qk"], wts["subln_g"], i, n_seq, s, lambda_init)
        yb = _window_attention(qkv, bias_b, wts["sink_logits"], i, n_seq, s)
        yc = _axial_attention(qkv, n_seq, s)
        x = _merge(x, ya, yb, yc, wts["w_in"], wts["w_branch"], wts["w_out"], wts["ln_g"], wts["ln_b"], i)
        kv = _memkv(mem, wts["w_mem_kv"], i)
        x = _mem_attention(x, kv, wts["w_mem_q"], wts["w_mem_o"], wts["ln_g"], wts["ln_b"], i, s)
        x = _ffn(x, wts["w_ffn_gu"], wts["w_ffn_d"], wts["ln_g"], wts["ln_b"], i, 1, 3)
    return x.reshape(n_seq, s, D_MODEL)


def kernel(x_prompt, x_sample, mem_prompt, mem_sample, rel_bias_table, w_in, w_branch, w_out, lambda_qk, subln_g, sink_logits, qk_norm_g, w_mem_q, w_mem_kv, w_mem_o, w_ffn_gu, w_ffn_d, ln_g, ln_b):
    wts = {
        "w_in": w_in.astype(BF16),
        "w_branch": w_branch.astype(BF16),
        "w_out": w_out.astype(BF16),
        "w_mem_q": w_mem_q.astype(BF16),
        "w_mem_kv": w_mem_kv.astype(BF16),
        "w_mem_o": w_mem_o.astype(BF16),
        "w_ffn_gu": w_ffn_gu.astype(BF16),
        "w_ffn_d": w_ffn_d.astype(BF16),
        "lambda_qk": lambda_qk,
        "subln_g": subln_g.reshape(DEPTH, 1, A_V),
        "sink_logits": sink_logits,
        "qk_norm_g": qk_norm_g,
        "ln_g": ln_g.reshape(DEPTH, 4, 1, D_MODEL),
        "ln_b": ln_b.reshape(DEPTH, 4, 1, D_MODEL),
    }
    table_a = rel_bias_table[:, :A_HEADS]
    bias_b = _window_bias_tiles(rel_bias_table[:, A_HEADS:])
    y_prompt = _run_trunk(x_prompt, mem_prompt, wts, table_a, bias_b)
    y_sample = _run_trunk(x_sample, mem_sample, wts, table_a, bias_b)
    return (y_prompt, y_sample)
```

```python
import functools
import math

import numpy as np
import jax
import jax.numpy as jnp
from jax import lax
from jax.experimental import pallas as pl
from jax.experimental.pallas import tpu as pltpu

D_MODEL = 2048
DEPTH = 4
HEAD_DIM = 128
A_HEADS = 4
A_V = 2 * HEAD_DIM
B_HEADS = 8
B_KV_HEADS = 2
WINDOW = 128
C_HEADS = 8
C_KV_HEADS = 2
GRID_W = 64
ROPE_THETA = 10000.0
NUM_BUCKETS = 32
MAX_DISTANCE = 128
MEM_TOKENS = 256
MEM_HEADS = 4
MEM_INNER = MEM_HEADS * HEAD_DIM
D_FF = 5632
N_BRANCH = 3
BRANCH_WIDTH = 1024
QKV_WIDTH = 6144
ALPHA = (2 * DEPTH) ** 0.25
LN_EPS = 1e-5
RMS_EPS = 1e-6
NEG_INF = -1e30
LOG2E = math.log2(math.e)
QSCALE = HEAD_DIM ** -0.5 * LOG2E

QA_OFF, KA_OFF, VA_OFF = 0, 1024, 2048
QB_OFF, KB_OFF, VB_OFF = 3072, 4096, 4352
QC_OFF, KC_OFF, VC_OFF = 4608, 5632, 5888

V7X_VMEM_BYTES = 64 * 1024 * 1024
VMEM_LIMIT = 56 * 1024 * 1024
F32 = jnp.float32
BF16 = jnp.bfloat16


def _params(semantics):
    return pltpu.CompilerParams(dimension_semantics=semantics, vmem_limit_bytes=VMEM_LIMIT)


def _layer_norm(y, g, b):
    mu = jnp.mean(y, axis=-1, keepdims=True)
    d = y - mu
    var = jnp.mean(d * d, axis=-1, keepdims=True)
    return d * lax.rsqrt(var + LN_EPS) * g + b


def _dot(a, b):
    return jnp.dot(a, b, preferred_element_type=F32)


def _dot_nt(a, b):
    return lax.dot_general(a, b, (((1,), (1,)), ((), ())), preferred_element_type=F32)


def _lane_fold(x, op):
    parts = [x[:, j * 128:(j + 1) * 128] for j in range(x.shape[1] // 128)]
    return functools.reduce(op, parts)


def _exp2_minus(x, col):
    parts = [jnp.exp2(x[:, j * 128:(j + 1) * 128] - col) for j in range(x.shape[1] // 128)]
    return parts[0] if len(parts) == 1 else jnp.concatenate(parts, axis=1)


def _lane_bcast(col):
    return jnp.broadcast_to(col, (col.shape[0], 128))


FFN_TF = 1024


def _ffn_body(idx_ref, x_ref, wgu_hbm, wd_hbm, g_ref, b_ref, o_ref, xb_ref, acc_ref, wg_buf, wu_buf, wd_buf, sem,
              *, tf, n_tiles):
    i = pl.program_id(0)
    layer = idx_ref[0]
    which = idx_ref[1]
    chunks = [(off, min(tf, D_FF - off)) for off in range(0, D_FF, tf)]
    n_c = len(chunks)

    def chunk_copies(c):
        off, w = chunks[c]
        slot = c % 2
        return (
            pltpu.make_async_copy(wgu_hbm.at[layer, which, :, pl.ds(off, w)], wg_buf.at[slot, :, pl.ds(0, w)], sem.at[0, slot]),
            pltpu.make_async_copy(wgu_hbm.at[layer, which, :, pl.ds(D_FF + off, w)], wu_buf.at[slot, :, pl.ds(0, w)], sem.at[1, slot]),
            pltpu.make_async_copy(wd_hbm.at[layer, which, pl.ds(off, w), :], wd_buf.at[slot, pl.ds(0, w), :], sem.at[2, slot]),
        )

    def start(c):
        for cp in chunk_copies(c):
            cp.start()

    def wait(c):
        for cp in chunk_copies(c):
            cp.wait()

    @pl.when(i == 0)
    def _():
        start(0)

    xb_ref[...] = x_ref[...].astype(BF16)
    wait(0)

    def down(rows, c):
        w = chunks[c][1]
        slot = c % 2
        xb = xb_ref[rows, :]
        h = _dot(xb, wg_buf[slot, :, 0:w])
        u = _dot(xb, wu_buf[slot, :, 0:w])
        a = (h * jax.nn.sigmoid(h) * u).astype(BF16)
        return _dot(a, wd_buf[slot, 0:w, :])

    every = slice(None)
    for c in range(n_c - 1):
        start(c + 1)
        d = down(every, c)
        if c == 0:
            acc_ref[...] = d
        else:
            acc_ref[...] += d
        wait(c + 1)

    @pl.when(i + 1 < n_tiles)
    def _():
        start(0)

    half = x_ref.shape[0] // 2
    for r in range(2):
        rows = slice(r * half, (r + 1) * half)
        y = ALPHA * x_ref[rows, :] + 0.5 * (acc_ref[rows, :] + down(rows, n_c - 1))
        o_ref[rows, :] = _layer_norm(y, g_ref[...], b_ref[...])


def _ffn(x, w_gu, w_d, ln_g, ln_b, layer, which, ln_idx, *, tm=512, tf=FFN_TF):
    m = x.shape[0]
    n_tiles = m // tm
    assert pl.cdiv(D_FF, tf) % 2 == 0, "the next tile's chunk 0 must reuse the slot the last chunk leaves free"
    idx = jnp.array([layer, which, ln_idx], jnp.int32)
    ln_spec = pl.BlockSpec((None, None, 1, D_MODEL), lambda i, idx: (idx[0], idx[2], 0, 0))
    grid_spec = pltpu.PrefetchScalarGridSpec(
        num_scalar_prefetch=1,
        grid=(n_tiles,),
        in_specs=[
            pl.BlockSpec((tm, D_MODEL), lambda i, idx: (i, 0)),
            pl.BlockSpec(memory_space=pl.ANY),
            pl.BlockSpec(memory_space=pl.ANY),
            ln_spec, ln_spec,
        ],
        out_specs=pl.BlockSpec((tm, D_MODEL), lambda i, idx: (i, 0)),
        scratch_shapes=[
            pltpu.VMEM((tm, D_MODEL), BF16), pltpu.VMEM((tm, D_MODEL), F32),
            pltpu.VMEM((2, D_MODEL, tf), BF16), pltpu.VMEM((2, D_MODEL, tf), BF16),
            pltpu.VMEM((2, tf, D_MODEL), BF16),
            pltpu.SemaphoreType.DMA((3, 2)),
        ],
    )
    return pl.pallas_call(
        functools.partial(_ffn_body, tf=tf, n_tiles=n_tiles),
        out_shape=jax.ShapeDtypeStruct((m, D_MODEL), F32),
        grid_spec=grid_spec,
        compiler_params=_params(("arbitrary",)),
        name="ffn",
    )(idx, x, w_gu, w_d, ln_g, ln_b)


def _qkv_body(x_ref, w_ref, gn_ref, cos_ref, sin_ref, o_ref, *, tn):
    xb = x_ref[...].astype(BF16)
    cos = cos_ref[...]
    sin = sin_ref[...]
    starts = [j * tn for j in range(QKV_WIDTH // tn)]
    for c0 in sorted(starts, key=lambda c: not (QC_OFF <= c < VC_OFF)):
        acc = _dot(xb, w_ref[:, c0:c0 + tn])
        is_q = (QA_OFF <= c0 < KA_OFF) or (QB_OFF <= c0 < KB_OFF) or (QC_OFF <= c0 < KC_OFF)
        if QC_OFF <= c0 < VC_OFF:
            g = gn_ref[0:1, :] if c0 < KC_OFF else gn_ref[1:2, :]
            for hh in range(tn // HEAD_DIM):
                a = acc[:, hh * HEAD_DIM:(hh + 1) * HEAD_DIM]
                r = lax.rsqrt(jnp.mean(a * a, axis=-1, keepdims=True) + RMS_EPS)
                y = a * r * g
                y = y * cos + pltpu.roll(y, HEAD_DIM // 2, 1) * sin
                if is_q:
                    y = y * QSCALE
                o_ref[:, c0 + hh * HEAD_DIM:c0 + (hh + 1) * HEAD_DIM] = y.astype(BF16)
        else:
            if is_q:
                acc = acc * QSCALE
            o_ref[:, c0:c0 + tn] = acc.astype(BF16)


def _qkv(x, w_qkv, qk_norm_g, cos2, sin2, layer, *, tm=512, tn=256):
    m = x.shape[0]
    s = cos2.shape[0]
    n_pos = s // tm
    return pl.pallas_call(
        functools.partial(_qkv_body, tn=tn),
        out_shape=jax.ShapeDtypeStruct((m, QKV_WIDTH), BF16),
        grid=(m // tm,),
        in_specs=[
            pl.BlockSpec((tm, D_MODEL), lambda i: (i, 0)),
            pl.BlockSpec((None, D_MODEL, QKV_WIDTH), lambda i: (layer, 0, 0), pipeline_mode=pl.Buffered(1)),
            pl.BlockSpec((None, 2, HEAD_DIM), lambda i: (layer, 0, 0)),
            pl.BlockSpec((tm, HEAD_DIM), lambda i: (i % n_pos, 0)),
            pl.BlockSpec((tm, HEAD_DIM), lambda i: (i % n_pos, 0)),
        ],
        out_specs=pl.BlockSpec((tm, QKV_WIDTH), lambda i: (i, 0)),
        compiler_params=_params(("parallel",)),
        name="qkv",
    )(x, w_qkv, qk_norm_g, cos2, sin2)


def _diff_body(q_ref, k_ref, v_ref, bias_ref, lam_ref, sg_ref, o_ref, s1_ref, s2_ref, m_ref, *, t, tk, nc, nq, n_items, lambda_init):
    g = pl.program_id(0)
    i = jnp.minimum(g, n_items - 1) % nq

    @pl.when(g == 0)
    def _():
        s1_ref[1] = jnp.zeros(s1_ref.shape[1:], F32)
        s2_ref[1] = jnp.zeros(s2_ref.shape[1:], F32)
        m_ref[1] = jnp.zeros(m_ref.shape[1:], F32)

    def step(cur, prv):
        q = q_ref[...]
        q1 = q[:, :HEAD_DIM]
        q2 = q[:, HEAD_DIM:]
        m1p = m_ref[prv, 0]
        m2p = m_ref[prv, 1]
        m1 = m2 = acc = l1 = l2 = None
        for c in range(nc):
            kc = k_ref[c * tk:(c + 1) * tk, :]
            b = bias_ref[jnp.clip(c - i * (t // tk), -2, t // tk + 1) + 2]
            s1 = _dot_nt(q1, kc[:, :HEAD_DIM]) + b
            s2 = _dot_nt(q2, kc[:, HEAD_DIM:]) + b
            s1_ref[cur, c] = s1
            s2_ref[cur, c] = s2
            f1 = _lane_fold(s1, jnp.maximum)
            f2 = _lane_fold(s2, jnp.maximum)
            m1 = f1 if m1 is None else jnp.maximum(m1, f1)
            m2 = f2 if m2 is None else jnp.maximum(m2, f2)

            e1 = _exp2_minus(s1_ref[prv, c], m1p)
            e2 = _exp2_minus(s2_ref[prv, c], m2p)
            e = jnp.concatenate([e1.astype(BF16), e2.astype(BF16)], axis=0)
            d = _dot(e, v_ref[c * tk:(c + 1) * tk, :])
            f1 = _lane_fold(e1, jnp.add)
            f2 = _lane_fold(e2, jnp.add)
            acc = d if acc is None else acc + d
            l1 = f1 if l1 is None else l1 + f1
            l2 = f2 if l2 is None else l2 + f2
        m_ref[cur, 0] = _lane_bcast(jnp.max(m1, axis=-1, keepdims=True))
        m_ref[cur, 1] = _lane_bcast(jnp.max(m2, axis=-1, keepdims=True))

        l1 = jnp.sum(l1, axis=-1, keepdims=True)
        l2 = jnp.sum(l2, axis=-1, keepdims=True)
        lam = lam_ref[...]
        lam_full = (jnp.exp(jnp.sum(lam[0:1] * lam[1:2], axis=-1, keepdims=True))
                    - jnp.exp(jnp.sum(lam[2:3] * lam[3:4], axis=-1, keepdims=True)) + lambda_init)
        o = acc[0:t, :] / l1 - lam_full * (acc[t:2 * t, :] / l2)
        o = o * lax.rsqrt(jnp.mean(o * o, axis=-1, keepdims=True) + RMS_EPS) * sg_ref[...]
        o_ref[...] = (o * (1.0 - lambda_init)).astype(BF16)

    @pl.when(g % 2 == 0)
    def _():
        step(0, 1)

    @pl.when(g % 2 == 1)
    def _():
        step(1, 0)


def _pipeline_items(n_items, n_heads, nq):
    def scored(g):
        it = jnp.minimum(g, n_items - 1)
        return it // (n_heads * nq), (it // nq) % n_heads, it % nq

    def finished(g):
        it = jnp.maximum(g - 1, 0)
        return it // (n_heads * nq), (it // nq) % n_heads, it % nq

    return scored, finished


DIFF_TK = 256


def _diff_rows(s):
    return 512 if s <= 2048 else 256


def _diff_attention(qkv, bias_a, lambda_qk, subln_g, layer, n_seq, s, lambda_init, *, tk=DIFF_TK):
    t = _diff_rows(s)
    m = qkv.shape[0]
    nc = s // tk
    nq = s // t
    n_tiles = t // tk + 4
    n_items = n_seq * A_HEADS * nq
    scored, finished = _pipeline_items(n_items, A_HEADS, nq)

    def q_map(g):
        b, h, i = scored(g)
        return (b * nq + i, QA_OFF // A_V + h)

    def k_map(g):
        b, h, _ = scored(g)
        return (b, KA_OFF // A_V + h)

    def v_map(g):
        b, h, _ = finished(g)
        return (b, VA_OFF // A_V + h)

    def o_map(g):
        b, h, i = finished(g)
        return (b * nq + i, h)

    return pl.pallas_call(
        functools.partial(_diff_body, t=t, tk=tk, nc=nc, nq=nq, n_items=n_items, lambda_init=lambda_init),
        out_shape=jax.ShapeDtypeStruct((m, BRANCH_WIDTH), BF16),
        grid=(n_items + 1,),
        in_specs=[
            pl.BlockSpec((t, A_V), q_map),
            pl.BlockSpec((s, A_V), k_map),
            pl.BlockSpec((s, A_V), v_map),
            pl.BlockSpec((None, n_tiles, t, tk), lambda g: (scored(g)[1], 0, 0, 0)),
            pl.BlockSpec((None, 4, HEAD_DIM), lambda g: (layer, 0, 0)),
            pl.BlockSpec((None, 1, A_V), lambda g: (layer, 0, 0)),
        ],
        out_specs=pl.BlockSpec((t, A_V), o_map),
        scratch_shapes=[pltpu.VMEM((2, nc, t, tk), F32), pltpu.VMEM((2, nc, t, tk), F32), pltpu.VMEM((2, 2, t, 128), F32)],
        compiler_params=_params(("arbitrary",)),
        name="diff_attn",
    )(qkv, qkv, qkv, bias_a, lambda_qk, subln_g)


def _axial_body(q_ref, k_ref, v_ref, o_ref, s_ref, m_ref, *, t, tk, nc):
    gq = C_HEADS // C_KV_HEADS
    g = pl.program_id(0)
    half = gq // 2

    @pl.when(g == 0)
    def _():
        s_ref[1] = jnp.zeros(s_ref.shape[1:], F32)
        m_ref[1] = jnp.zeros(m_ref.shape[1:], F32)

    def step(cur, prv):
        q = q_ref[...]
        qs = [jnp.concatenate([q[:, h * HEAD_DIM:(h + 1) * HEAD_DIM] for h in range(r * half, (r + 1) * half)], axis=0)
              for r in range(2)]
        mp = [m_ref[prv, r] for r in range(2)]
        m = [None, None]
        acc = [None, None]
        l = [None, None]
        for c in range(nc):
            kc = k_ref[c * tk:(c + 1) * tk, :]
            vc = v_ref[c * tk:(c + 1) * tk, :]
            for r in range(2):
                sc = _dot_nt(qs[r], kc)
                s_ref[cur, r, c] = sc
                f = _lane_fold(sc, jnp.maximum)
                m[r] = f if m[r] is None else jnp.maximum(m[r], f)

                e = _exp2_minus(s_ref[prv, r, c], mp[r])
                d = _dot(e.astype(BF16), vc)
                f = _lane_fold(e, jnp.add)
                acc[r] = d if acc[r] is None else acc[r] + d
                l[r] = f if l[r] is None else l[r] + f
        for r in range(2):
            m_ref[cur, r] = _lane_bcast(jnp.max(m[r], axis=-1, keepdims=True))
            o = acc[r] / jnp.sum(l[r], axis=-1, keepdims=True)
            for hh in range(half):
                h = r * half + hh
                o_ref[:, h * HEAD_DIM:(h + 1) * HEAD_DIM] = o[hh * t:(hh + 1) * t, :].astype(BF16)

    @pl.when(g % 2 == 0)
    def _():
        step(0, 1)

    @pl.when(g % 2 == 1)
    def _():
        step(1, 0)


def _axial_attention(qkv, n_seq, s, *, t=256, tk=256):
    m = qkv.shape[0]
    gq = C_HEADS // C_KV_HEADS
    gw = gq * HEAD_DIM
    nq = s // t
    nc = s // tk
    n_items = n_seq * C_KV_HEADS * nq
    scored, finished = _pipeline_items(n_items, C_KV_HEADS, nq)

    def q_map(g):
        b, kv, i = scored(g)
        return (b * nq + i, QC_OFF // gw + kv)

    def k_map(g):
        b, kv, _ = scored(g)
        return (b, KC_OFF // HEAD_DIM + kv)

    def v_map(g):
        b, kv, _ = finished(g)
        return (b, VC_OFF // HEAD_DIM + kv)

    def o_map(g):
        b, kv, i = finished(g)
        return (b * nq + i, kv)

    return pl.pallas_call(
        functools.partial(_axial_body, t=t, tk=tk, nc=nc),
        out_shape=jax.ShapeDtypeStruct((m, BRANCH_WIDTH), BF16),
        grid=(n_items + 1,),
        in_specs=[
            pl.BlockSpec((t, gw), q_map),
            pl.BlockSpec((s, HEAD_DIM), k_map),
            pl.BlockSpec((s, HEAD_DIM), v_map),
        ],
        out_specs=pl.BlockSpec((t, gw), o_map),
        scratch_shapes=[pltpu.VMEM((2, 2, nc, gq // 2 * t, tk), F32), pltpu.VMEM((2, 2, gq // 2 * t, 128), F32)],
        compiler_params=_params(("arbitrary",)),
        name="axial_attn",
    )(qkv, qkv, qkv)


def _window_body(sink_ref, q_ref, k_ref, v_ref, bias_ref, o_ref, *, nb):
    g = B_HEADS // B_KV_HEADS
    t = WINDOW
    kv = pl.program_id(1)
    sink = jnp.concatenate([jnp.full((1, t), sink_ref[kv * g + h] * LOG2E, F32) for h in range(g)], axis=1)

    def block(i, carry):
        r0 = pl.multiple_of(i * t, t)
        w0 = pl.multiple_of(jnp.clip(i - 1, 0, nb - 3) * t, t)
        tile = jnp.where(i == 0, 0, jnp.where(i == nb - 1, 2, 1))
        q = q_ref[pl.ds(r0, t), :]
        qs = jnp.concatenate([q[:, h * HEAD_DIM:(h + 1) * HEAD_DIM] for h in range(g)], axis=0)
        st = _dot_nt(k_ref[pl.ds(w0, 3 * t), :], qs) + bias_ref[tile]
        mx = jnp.maximum(jnp.max(st, axis=0, keepdims=True), sink)
        e = jnp.exp2(st - mx)
        l = jnp.sum(e, axis=0, keepdims=True) + jnp.exp2(sink - mx)
        p = (e * (1.0 / l)).astype(BF16)
        ot = lax.dot_general(v_ref[pl.ds(w0, 3 * t), :], p, (((0,), (0,)), ((), ())),
                             preferred_element_type=F32)
        for h in range(g):
            o_ref[pl.ds(r0, t), h * HEAD_DIM:(h + 1) * HEAD_DIM] = ot[:, h * t:(h + 1) * t].T.astype(BF16)
        return carry

    lax.fori_loop(0, nb, block, 0, unroll=4)


def _window_attention(qkv, bias_b, sink_logits, layer, n_seq, s):
    m = qkv.shape[0]
    g = B_HEADS // B_KV_HEADS
    gw = g * HEAD_DIM
    t = WINDOW
    nb = s // t
    grid_spec = pltpu.PrefetchScalarGridSpec(
        num_scalar_prefetch=1,
        grid=(n_seq, B_KV_HEADS),
        in_specs=[
            pl.BlockSpec((s, gw), lambda b, kv, sink: (b, QB_OFF // gw + kv)),
            pl.BlockSpec((s, HEAD_DIM), lambda b, kv, sink: (b, KB_OFF // HEAD_DIM + kv)),
            pl.BlockSpec((s, HEAD_DIM), lambda b, kv, sink: (b, VB_OFF // HEAD_DIM + kv)),
            pl.BlockSpec((3, None, 3 * t, g * t), lambda b, kv, sink: (0, kv, 0, 0)),
        ],
        out_specs=pl.BlockSpec((s, gw), lambda b, kv, sink: (b, kv)),
    )
    return pl.pallas_call(
        functools.partial(_window_body, nb=nb),
        out_shape=jax.ShapeDtypeStruct((m, BRANCH_WIDTH), BF16),
        grid_spec=grid_spec,
        compiler_params=_params(("parallel", "parallel")),
        name="window_attn",
    )(sink_logits[layer], qkv, qkv, qkv, bias_b)


def _gate_body(x_ref, ya_ref, yb_ref, yc_ref, wga_ref, wgb_ref, wgc_ref, wb_ref, o_ref, xb_ref):
    @pl.when(pl.program_id(1) == 0)
    def _():
        xb_ref[...] = x_ref[...].astype(BF16)

    xb = xb_ref[...]
    merged = None
    for n, (y_ref, wg_ref) in enumerate(((ya_ref, wga_ref), (yb_ref, wgb_ref), (yc_ref, wgc_ref))):
        gate = jax.nn.sigmoid(_dot(xb, wg_ref[...]))
        term = gate * _dot(y_ref[...], wb_ref[n])
        merged = term if merged is None else merged + term
    o_ref[...] = merged.astype(BF16)


def _outproj_body(x_ref, m_ref, wo_ref, g_ref, b_ref, o_ref, *, n_split):
    rows = x_ref.shape[0] // n_split
    for r in range(n_split):
        sl = slice(r * rows, (r + 1) * rows)
        y = ALPHA * x_ref[sl, :] + _dot(m_ref[sl, :], wo_ref[...])
        o_ref[sl, :] = _layer_norm(y, g_ref[...], b_ref[...])


def _merge(x, ya, yb, yc, w_in, w_branch, w_out, ln_g, ln_b, layer, *, tm=512, tc=512):
    m = x.shape[0]
    yspec = pl.BlockSpec((tm, BRANCH_WIDTH), lambda i, j: (i, 0))

    def gate_spec(n):
        return pl.BlockSpec((None, D_MODEL, tc), lambda i, j: (layer, 0, (QKV_WIDTH + n * D_MODEL) // tc + j))

    merged = pl.pallas_call(
        _gate_body,
        out_shape=jax.ShapeDtypeStruct((m, D_MODEL), BF16),
        grid=(m // tm, D_MODEL // tc),
        in_specs=[
            pl.BlockSpec((tm, D_MODEL), lambda i, j: (i, 0)),
            yspec, yspec, yspec,
            gate_spec(0), gate_spec(1), gate_spec(2),
            pl.BlockSpec((None, N_BRANCH, BRANCH_WIDTH, tc), lambda i, j: (layer, 0, 0, j)),
        ],
        out_specs=pl.BlockSpec((tm, tc), lambda i, j: (i, j)),
        scratch_shapes=[pltpu.VMEM((tm, D_MODEL), BF16)],
        compiler_params=_params(("parallel", "arbitrary")),
        name="gate_merge",
    )(x, ya, yb, yc, w_in, w_in, w_in, w_branch)
    return pl.pallas_call(
        functools.partial(_outproj_body, n_split=2),
        out_shape=jax.ShapeDtypeStruct((m, D_MODEL), F32),
        grid=(m // tm,),
        in_specs=[
            pl.BlockSpec((tm, D_MODEL), lambda i: (i, 0)),
            pl.BlockSpec((tm, D_MODEL), lambda i: (i, 0)),
            pl.BlockSpec((None, D_MODEL, D_MODEL), lambda i: (layer, 0, 0), pipeline_mode=pl.Buffered(1)),
            pl.BlockSpec((None, None, 1, D_MODEL), lambda i: (layer, 1, 0, 0)),
            pl.BlockSpec((None, None, 1, D_MODEL), lambda i: (layer, 1, 0, 0)),
        ],
        out_specs=pl.BlockSpec((tm, D_MODEL), lambda i: (i, 0)),
        compiler_params=_params(("parallel",)),
        name="out_proj",
    )(x, merged, w_out, ln_g, ln_b)


def _memkv_body(m_ref, w_ref, o_ref):
    o_ref[...] = _dot(m_ref[...].astype(BF16), w_ref[...]).astype(BF16)


def _memkv(mem, w_kv, layer):
    rows = mem.shape[0]
    return pl.pallas_call(
        _memkv_body,
        out_shape=jax.ShapeDtypeStruct((rows, 2 * MEM_INNER), BF16),
        grid=(rows // MEM_TOKENS,),
        in_specs=[
            pl.BlockSpec((MEM_TOKENS, D_MODEL), lambda i: (i, 0)),
            pl.BlockSpec((None, D_MODEL, 2 * MEM_INNER), lambda i: (layer, 0, 0)),
        ],
        out_specs=pl.BlockSpec((MEM_TOKENS, 2 * MEM_INNER), lambda i: (i, 0)),
        compiler_params=_params(("parallel",)),
        name="memkv",
    )(mem, w_kv)


def _mem_body(x_ref, kv_ref, wq_ref, wo_ref, g_ref, b_ref, o_ref, *, n_split):
    kv = kv_ref[...]
    rows = x_ref.shape[0] // n_split
    for r in range(n_split):
        sl = slice(r * rows, (r + 1) * rows)
        x = x_ref[sl, :]
        q = (_dot(x.astype(BF16), wq_ref[...]) * QSCALE).astype(BF16)
        heads = []
        for h in range(MEM_HEADS):
            sc = _dot_nt(q[:, h * HEAD_DIM:(h + 1) * HEAD_DIM], kv[:, h * HEAD_DIM:(h + 1) * HEAD_DIM])
            e = jnp.exp2(sc - jnp.max(sc, axis=-1, keepdims=True))
            l = jnp.sum(e, axis=-1, keepdims=True)
            v = kv[:, MEM_INNER + h * HEAD_DIM:MEM_INNER + (h + 1) * HEAD_DIM]
            heads.append((_dot(e.astype(BF16), v) / l).astype(BF16))
        o = jnp.concatenate(heads, axis=1)
        y = ALPHA * x + _dot(o, wo_ref[...])
        o_ref[sl, :] = _layer_norm(y, g_ref[...], b_ref[...])


def _mem_attention(x, kv, w_q, w_o, ln_g, ln_b, layer, s, *, tm=512):
    m = x.shape[0]
    per_seq = s // tm
    return pl.pallas_call(
        functools.partial(_mem_body, n_split=1),
        out_shape=jax.ShapeDtypeStruct((m, D_MODEL), F32),
        grid=(m // tm,),
        in_specs=[
            pl.BlockSpec((tm, D_MODEL), lambda i: (i, 0)),
            pl.BlockSpec((MEM_TOKENS, 2 * MEM_INNER), lambda i: (i // per_seq, 0)),
            pl.BlockSpec((None, D_MODEL, MEM_INNER), lambda i: (layer, 0, 0)),
            pl.BlockSpec((None, MEM_INNER, D_MODEL), lambda i: (layer, 0, 0)),
            pl.BlockSpec((None, None, 1, D_MODEL), lambda i: (layer, 2, 0, 0)),
            pl.BlockSpec((None, None, 1, D_MODEL), lambda i: (layer, 2, 0, 0)),
        ],
        out_specs=pl.BlockSpec((tm, D_MODEL), lambda i: (i, 0)),
        compiler_params=_params(("parallel",)),
        name="mem_attn",
    )(x, kv, w_q, w_o, ln_g, ln_b)


def _t5_bucket_np(rel):
    half = NUM_BUCKETS // 2
    max_exact = half // 2
    rel = np.asarray(rel, np.int32)
    ret = np.where(rel > 0, half, 0)
    n = np.abs(rel)
    ratio = np.log(np.maximum(n, 1).astype(np.float32) / np.float32(max_exact)) / np.float32(math.log(MAX_DISTANCE / max_exact))
    large = max_exact + (ratio * np.float32(half - max_exact)).astype(np.int32)
    large = np.minimum(large, half - 1)
    return (ret + np.where(n < max_exact, n, large)).astype(np.int32)


def _lookup_bias(table, buckets):
    n_heads = table.shape[1]
    b = jnp.asarray(buckets)[None]
    tab = table * LOG2E
    out = jnp.zeros((n_heads,) + buckets.shape, F32)
    for k in range(NUM_BUCKETS):
        out = jnp.where(b == k, tab[k].reshape((n_heads,) + (1,) * buckets.ndim), out)
    return out


def _diff_bias_tiles(table_a, t, tk):
    assert tk >= MAX_DISTANCE and t % tk == 0
    ii = np.arange(t)[:, None]
    jj = np.arange(tk)[None, :]
    buckets = np.stack([_t5_bucket_np(d * tk + jj - ii) for d in range(-2, t // tk + 2)])
    return _lookup_bias(table_a, buckets)


def _window_bias_tiles(table_b):
    g = B_HEADS // B_KV_HEADS
    rel = np.stack([(np.arange(3 * WINDOW) - p * WINDOW)[None, :] - np.arange(WINDOW)[:, None] for p in range(3)])
    bias = _lookup_bias(table_b, _t5_bucket_np(rel))
    bias = jnp.where(jnp.asarray(np.abs(rel) <= WINDOW)[None], bias, NEG_INF)
    bias = bias.reshape(B_KV_HEADS, g, 3, WINDOW, 3 * WINDOW)
    return jnp.transpose(bias, (2, 0, 4, 1, 3)).reshape(3, B_KV_HEADS, 3 * WINDOW, g * WINDOW)


def _rope_tables(s):
    n_rows = s // GRID_W
    row = np.repeat(np.arange(n_rows), GRID_W).astype(np.float32)
    col = np.tile(np.arange(GRID_W), n_rows).astype(np.float32)
    axis_dims = HEAD_DIM // 2
    inv = ROPE_THETA ** (-jnp.arange(0, axis_dims, 2, dtype=F32) / axis_dims)
    ang = jnp.concatenate([row[:, None] * inv[None, :], col[:, None] * inv[None, :]], axis=-1)
    cos, sin = jnp.cos(ang), jnp.sin(ang)
    return jnp.concatenate([cos, cos], axis=-1), jnp.concatenate([-sin, sin], axis=-1)


def _run_trunk(x, mem, wts, table_a, bias_b):
    n_seq, s, _ = x.shape
    x = x.reshape(n_seq * s, D_MODEL)
    mem = mem.reshape(n_seq * MEM_TOKENS, D_MODEL)
    cos2, sin2 = _rope_tables(s)
    bias_a = _diff_bias_tiles(table_a, _diff_rows(s), DIFF_TK)
    for i in range(DEPTH):
        lambda_init = 0.8 - 0.6 * math.exp(-0.3 * i)
        x = _ffn(x, wts["w_ffn_gu"], wts["w_ffn_d"], wts["ln_g"], wts["ln_b"], i, 0, 0)
        qkv = _qkv(x, wts["w_in"], wts["qk_norm_g"], cos2, sin2, i)
        ya = _diff_attention(qkv, bias_a, wts["lambda_qk"], wts["subln_g"], i, n_seq, s, lambda_init)
        yb = _window_attention(qkv, bias_b, wts["sink_logits"], i, n_seq, s)
        yc = _axial_attention(qkv, n_seq, s)
        x = _merge(x, ya, yb, yc, wts["w_in"], wts["w_branch"], wts["w_out"], wts["ln_g"], wts["ln_b"], i)
        kv = _memkv(mem, wts["w_mem_kv"], i)
        x = _mem_attention(x, kv, wts["w_mem_q"], wts["w_mem_o"], wts["ln_g"], wts["ln_b"], i, s)
        x = _ffn(x, wts["w_ffn_gu"], wts["w_ffn_d"], wts["ln_g"], wts["ln_b"], i, 1, 3)
    return x.reshape(n_seq, s, D_MODEL)


def kernel(x_prompt, x_sample, mem_prompt, mem_sample, rel_bias_table, w_in, w_branch, w_out, lambda_qk, subln_g, sink_logits, qk_norm_g, w_mem_q, w_mem_kv, w_mem_o, w_ffn_gu, w_ffn_d, ln_g, ln_b):
    wts = {
        "w_in": w_in.astype(BF16),
        "w_branch": w_branch.astype(BF16),
        "w_out": w_out.astype(BF16),
        "w_mem_q": w_mem_q.astype(BF16),
        "w_mem_kv": w_mem_kv.astype(BF16),
        "w_mem_o": w_mem_o.astype(BF16),
        "w_ffn_gu": w_ffn_gu.astype(BF16),
        "w_ffn_d": w_ffn_d.astype(BF16),
        "lambda_qk": lambda_qk,
        "subln_g": subln_g.reshape(DEPTH, 1, A_V),
        "sink_logits": sink_logits,
        "qk_norm_g": qk_norm_g,
        "ln_g": ln_g.reshape(DEPTH, 4, 1, D_MODEL),
        "ln_b": ln_b.reshape(DEPTH, 4, 1, D_MODEL),
    }
    table_a = rel_bias_table[:, :A_HEADS]
    bias_b = _window_bias_tiles(rel_bias_table[:, A_HEADS:])
    y_prompt = _run_trunk(x_prompt, mem_prompt, wts, table_a, bias_b)
    y_sample = _run_trunk(x_sample, mem_sample, wts, table_a, bias_b)
    return (y_prompt, y_sample)
```

```python
import functools
import math

import numpy as np
import jax
import jax.numpy as jnp
from jax import lax
from jax.experimental import pallas as pl
from jax.experimental.pallas import tpu as pltpu

D_MODEL = 2048
DEPTH = 4
HEAD_DIM = 128
A_HEADS = 4
A_V = 2 * HEAD_DIM
B_HEADS = 8
B_KV_HEADS = 2
WINDOW = 128
C_HEADS = 8
C_KV_HEADS = 2
GRID_W = 64
ROPE_THETA = 10000.0
NUM_BUCKETS = 32
MAX_DISTANCE = 128
MEM_TOKENS = 256
MEM_HEADS = 4
MEM_INNER = MEM_HEADS * HEAD_DIM
D_FF = 5632
N_BRANCH = 3
BRANCH_WIDTH = 1024
QKV_WIDTH = 6144
ALPHA = (2 * DEPTH) ** 0.25
LN_EPS = 1e-5
RMS_EPS = 1e-6
NEG_INF = -1e30
LOG2E = math.log2(math.e)
QSCALE = HEAD_DIM ** -0.5 * LOG2E

QA_OFF, KA_OFF, VA_OFF = 0, 1024, 2048
QB_OFF, KB_OFF, VB_OFF = 3072, 4096, 4352
QC_OFF, KC_OFF, VC_OFF = 4608, 5632, 5888

V7X_VMEM_BYTES = 64 * 1024 * 1024
V7X_LANES = 128
VMEM_LIMIT = V7X_VMEM_BYTES - 8 * 1024 * 1024

ROW_TILE = 512
FFN_TF = 1024
QKV_TN = 256
GATE_TC = 512
ATTN_T = 256
ATTN_TK = 256
F32 = jnp.float32
BF16 = jnp.bfloat16


def _params(semantics):
    return pltpu.CompilerParams(dimension_semantics=semantics, vmem_limit_bytes=VMEM_LIMIT)


def _layer_norm(y, g, b):
    mu = jnp.mean(y, axis=-1, keepdims=True)
    d = y - mu
    var = jnp.mean(d * d, axis=-1, keepdims=True)
    return d * lax.rsqrt(var + LN_EPS) * g + b


def _dot(a, b):
    return jnp.dot(a, b, preferred_element_type=F32)


def _dot_nt(a, b):
    return lax.dot_general(a, b, (((1,), (1,)), ((), ())), preferred_element_type=F32)


def _lane_groups(x):
    return [x[:, j * V7X_LANES:(j + 1) * V7X_LANES] for j in range(x.shape[1] // V7X_LANES)]


def _lane_fold(x, op):
    return functools.reduce(op, _lane_groups(x))


def _exp2_minus(x, col):
    parts = [jnp.exp2(p - col) for p in _lane_groups(x)]
    return parts[0] if len(parts) == 1 else jnp.concatenate(parts, axis=1)


def _lane_bcast(col):
    return jnp.broadcast_to(col, (col.shape[0], V7X_LANES))


def _ffn_body(idx_ref, x_ref, wgu_hbm, wd_hbm, g_ref, b_ref, o_ref, xb_ref, acc_ref, wg_buf, wu_buf, wd_buf, sem,
              *, tf, n_tiles):
    i = pl.program_id(0)
    layer = idx_ref[0]
    which = idx_ref[1]
    chunks = [(off, min(tf, D_FF - off)) for off in range(0, D_FF, tf)]
    n_c = len(chunks)

    def chunk_copies(c):
        off, w = chunks[c]
        slot = c % 2
        return (
            pltpu.make_async_copy(wgu_hbm.at[layer, which, :, pl.ds(off, w)], wg_buf.at[slot, :, pl.ds(0, w)], sem.at[0, slot]),
            pltpu.make_async_copy(wgu_hbm.at[layer, which, :, pl.ds(D_FF + off, w)], wu_buf.at[slot, :, pl.ds(0, w)], sem.at[1, slot]),
            pltpu.make_async_copy(wd_hbm.at[layer, which, pl.ds(off, w), :], wd_buf.at[slot, pl.ds(0, w), :], sem.at[2, slot]),
        )

    def start(c):
        for cp in chunk_copies(c):
            cp.start()

    def wait(c):
        for cp in chunk_copies(c):
            cp.wait()

    @pl.when(i == 0)
    def _():
        start(0)

    xb_ref[...] = x_ref[...].astype(BF16)
    acc_ref[...] = jnp.zeros_like(acc_ref)
    wait(0)
    xb = xb_ref[...]
    for c in range(n_c):
        w = chunks[c][1]
        slot = c % 2
        if c + 1 < n_c:
            start(c + 1)
        else:
            @pl.when(i + 1 < n_tiles)
            def _():
                start(0)
        h = _dot(xb, wg_buf[slot, :, 0:w])
        u = _dot(xb, wu_buf[slot, :, 0:w])
        a = (h * jax.nn.sigmoid(h) * u).astype(BF16)
        acc_ref[...] += _dot(a, wd_buf[slot, 0:w, :])
        if c + 1 < n_c:
            wait(c + 1)

    y = ALPHA * x_ref[...] + 0.5 * acc_ref[...]
    o_ref[...] = _layer_norm(y, g_ref[...], b_ref[...])


def _ffn(x, w_gu, w_d, ln_g, ln_b, layer, which, ln_idx, *, tm=ROW_TILE, tf=FFN_TF):
    m = x.shape[0]
    n_tiles = m // tm
    assert pl.cdiv(D_FF, tf) % 2 == 0, "the next tile's chunk 0 must reuse the slot the last chunk leaves free"
    idx = jnp.array([layer, which, ln_idx], jnp.int32)
    ln_spec = pl.BlockSpec((None, None, 1, D_MODEL), lambda i, idx: (idx[0], idx[2], 0, 0))
    grid_spec = pltpu.PrefetchScalarGridSpec(
        num_scalar_prefetch=1,
        grid=(n_tiles,),
        in_specs=[
            pl.BlockSpec((tm, D_MODEL), lambda i, idx: (i, 0)),
            pl.BlockSpec(memory_space=pl.ANY),
            pl.BlockSpec(memory_space=pl.ANY),
            ln_spec, ln_spec,
        ],
        out_specs=pl.BlockSpec((tm, D_MODEL), lambda i, idx: (i, 0)),
        scratch_shapes=[
            pltpu.VMEM((tm, D_MODEL), BF16), pltpu.VMEM((tm, D_MODEL), F32),
            pltpu.VMEM((2, D_MODEL, tf), BF16), pltpu.VMEM((2, D_MODEL, tf), BF16),
            pltpu.VMEM((2, tf, D_MODEL), BF16),
            pltpu.SemaphoreType.DMA((3, 2)),
        ],
    )
    return pl.pallas_call(
        functools.partial(_ffn_body, tf=tf, n_tiles=n_tiles),
        out_shape=jax.ShapeDtypeStruct((m, D_MODEL), F32),
        grid_spec=grid_spec,
        compiler_params=_params(("arbitrary",)),
        name="ffn",
    )(idx, x, w_gu, w_d, ln_g, ln_b)


def _qkv_body(x_ref, w_ref, gn_ref, cos_ref, sin_ref, o_ref, *, tn):
    xb = x_ref[...].astype(BF16)
    cos = cos_ref[...]
    sin = sin_ref[...]
    starts = [j * tn for j in range(QKV_WIDTH // tn)]
    for c0 in sorted(starts, key=lambda c: not (QC_OFF <= c < VC_OFF)):
        acc = _dot(xb, w_ref[:, c0:c0 + tn])
        is_q = (QA_OFF <= c0 < KA_OFF) or (QB_OFF <= c0 < KB_OFF) or (QC_OFF <= c0 < KC_OFF)
        if QC_OFF <= c0 < VC_OFF:
            g = gn_ref[0:1, :] if c0 < KC_OFF else gn_ref[1:2, :]
            for hh in range(tn // HEAD_DIM):
                a = acc[:, hh * HEAD_DIM:(hh + 1) * HEAD_DIM]
                r = lax.rsqrt(jnp.mean(a * a, axis=-1, keepdims=True) + RMS_EPS)
                y = a * r * g
                y = y * cos + pltpu.roll(y, HEAD_DIM // 2, 1) * sin
                if is_q:
                    y = y * QSCALE
                o_ref[:, c0 + hh * HEAD_DIM:c0 + (hh + 1) * HEAD_DIM] = y.astype(BF16)
        else:
            if is_q:
                acc = acc * QSCALE
            o_ref[:, c0:c0 + tn] = acc.astype(BF16)


def _qkv(x, w_qkv, qk_norm_g, cos2, sin2, layer, *, tm=ROW_TILE, tn=QKV_TN):
    m = x.shape[0]
    s = cos2.shape[0]
    n_pos = s // tm
    return pl.pallas_call(
        functools.partial(_qkv_body, tn=tn),
        out_shape=jax.ShapeDtypeStruct((m, QKV_WIDTH), BF16),
        grid=(m // tm,),
        in_specs=[
            pl.BlockSpec((tm, D_MODEL), lambda i: (i, 0)),
            pl.BlockSpec((None, D_MODEL, QKV_WIDTH), lambda i: (layer, 0, 0), pipeline_mode=pl.Buffered(1)),
            pl.BlockSpec((None, 2, HEAD_DIM), lambda i: (layer, 0, 0)),
            pl.BlockSpec((tm, HEAD_DIM), lambda i: (i % n_pos, 0)),
            pl.BlockSpec((tm, HEAD_DIM), lambda i: (i % n_pos, 0)),
        ],
        out_specs=pl.BlockSpec((tm, QKV_WIDTH), lambda i: (i, 0)),
        compiler_params=_params(("parallel",)),
        name="qkv",
    )(x, w_qkv, qk_norm_g, cos2, sin2)


def _diff_body(q_ref, k_ref, v_ref, bias_ref, lam_ref, sg_ref, o_ref, s1_ref, s2_ref, m_ref, *, t, tk, nc, nq, n_items, lambda_init):
    g = pl.program_id(0)
    i = jnp.minimum(g, n_items - 1) % nq

    @pl.when(g == 0)
    def _():
        s1_ref[1] = jnp.zeros(s1_ref.shape[1:], F32)
        s2_ref[1] = jnp.zeros(s2_ref.shape[1:], F32)
        m_ref[1] = jnp.zeros(m_ref.shape[1:], F32)

    def step(cur, prv):
        q = q_ref[...]
        q1 = q[:, :HEAD_DIM]
        q2 = q[:, HEAD_DIM:]
        m1p = m_ref[prv, 0]
        m2p = m_ref[prv, 1]
        m1 = m2 = acc = l1 = l2 = None
        for c in range(nc):
            kc = k_ref[c * tk:(c + 1) * tk, :]
            b = bias_ref[jnp.clip(c - i * (t // tk), -2, t // tk + 1) + 2]
            s1 = _dot_nt(q1, kc[:, :HEAD_DIM]) + b
            s2 = _dot_nt(q2, kc[:, HEAD_DIM:]) + b
            s1_ref[cur, c] = s1
            s2_ref[cur, c] = s2
            f1 = _lane_fold(s1, jnp.maximum)
            f2 = _lane_fold(s2, jnp.maximum)
            m1 = f1 if m1 is None else jnp.maximum(m1, f1)
            m2 = f2 if m2 is None else jnp.maximum(m2, f2)

            e1 = _exp2_minus(s1_ref[prv, c], m1p)
            e2 = _exp2_minus(s2_ref[prv, c], m2p)
            e = jnp.concatenate([e1.astype(BF16), e2.astype(BF16)], axis=0)
            d = _dot(e, v_ref[c * tk:(c + 1) * tk, :])
            f1 = _lane_fold(e1, jnp.add)
            f2 = _lane_fold(e2, jnp.add)
            acc = d if acc is None else acc + d
            l1 = f1 if l1 is None else l1 + f1
            l2 = f2 if l2 is None else l2 + f2
        m_ref[cur, 0] = _lane_bcast(jnp.max(m1, axis=-1, keepdims=True))
        m_ref[cur, 1] = _lane_bcast(jnp.max(m2, axis=-1, keepdims=True))

        l1 = jnp.sum(l1, axis=-1, keepdims=True)
        l2 = jnp.sum(l2, axis=-1, keepdims=True)
        lam = lam_ref[...]
        lam_full = (jnp.exp(jnp.sum(lam[0:1] * lam[1:2], axis=-1, keepdims=True))
                    - jnp.exp(jnp.sum(lam[2:3] * lam[3:4], axis=-1, keepdims=True)) + lambda_init)
        o = acc[0:t, :] / l1 - lam_full * (acc[t:2 * t, :] / l2)
        o = o * lax.rsqrt(jnp.mean(o * o, axis=-1, keepdims=True) + RMS_EPS) * sg_ref[...]
        o_ref[...] = (o * (1.0 - lambda_init)).astype(BF16)

    @pl.when(g % 2 == 0)
    def _():
        step(0, 1)

    @pl.when(g % 2 == 1)
    def _():
        step(1, 0)


def _pipeline_items(n_items, n_heads, nq):
    def scored(g):
        it = jnp.minimum(g, n_items - 1)
        return it // (n_heads * nq), (it // nq) % n_heads, it % nq

    def finished(g):
        it = jnp.maximum(g - 1, 0)
        return it // (n_heads * nq), (it // nq) % n_heads, it % nq

    return scored, finished


def _diff_rows(s):
    return 2 * ATTN_T if 32 * ATTN_T * s <= 16 * 1024 * 1024 else ATTN_T


def _diff_attention(qkv, bias_a, lambda_qk, subln_g, layer, n_seq, s, lambda_init, *, tk=ATTN_TK):
    t = _diff_rows(s)
    m = qkv.shape[0]
    nc = s // tk
    nq = s // t
    n_tiles = t // tk + 4
    n_items = n_seq * A_HEADS * nq
    scored, finished = _pipeline_items(n_items, A_HEADS, nq)

    def q_map(g):
        b, h, i = scored(g)
        return (b * nq + i, QA_OFF // A_V + h)

    def k_map(g):
        b, h, _ = scored(g)
        return (b, KA_OFF // A_V + h)

    def v_map(g):
        b, h, _ = finished(g)
        return (b, VA_OFF // A_V + h)

    def o_map(g):
        b, h, i = finished(g)
        return (b * nq + i, h)

    return pl.pallas_call(
        functools.partial(_diff_body, t=t, tk=tk, nc=nc, nq=nq, n_items=n_items, lambda_init=lambda_init),
        out_shape=jax.ShapeDtypeStruct((m, BRANCH_WIDTH), BF16),
        grid=(n_items + 1,),
        in_specs=[
            pl.BlockSpec((t, A_V), q_map),
            pl.BlockSpec((s, A_V), k_map),
            pl.BlockSpec((s, A_V), v_map),
            pl.BlockSpec((None, n_tiles, t, tk), lambda g: (scored(g)[1], 0, 0, 0)),
            pl.BlockSpec((None, 4, HEAD_DIM), lambda g: (layer, 0, 0)),
            pl.BlockSpec((None, 1, A_V), lambda g: (layer, 0, 0)),
        ],
        out_specs=pl.BlockSpec((t, A_V), o_map),
        scratch_shapes=[pltpu.VMEM((2, nc, t, tk), F32), pltpu.VMEM((2, nc, t, tk), F32), pltpu.VMEM((2, 2, t, V7X_LANES), F32)],
        compiler_params=_params(("arbitrary",)),
        name="diff_attn",
    )(qkv, qkv, qkv, bias_a, lambda_qk, subln_g)


def _axial_body(q_ref, k_ref, v_ref, o_ref, s_ref, m_ref, *, t, tk, nc):
    gq = C_HEADS // C_KV_HEADS
    g = pl.program_id(0)
    half = gq // 2

    @pl.when(g == 0)
    def _():
        s_ref[1] = jnp.zeros(s_ref.shape[1:], F32)
        m_ref[1] = jnp.zeros(m_ref.shape[1:], F32)

    def step(cur, prv):
        q = q_ref[...]
        qs = [jnp.concatenate([q[:, h * HEAD_DIM:(h + 1) * HEAD_DIM] for h in range(r * half, (r + 1) * half)], axis=0)
              for r in range(2)]
        mp = [m_ref[prv, r] for r in range(2)]
        m = [None, None]
        acc = [None, None]
        l = [None, None]
        for c in range(nc):
            kc = k_ref[c * tk:(c + 1) * tk, :]
            vc = v_ref[c * tk:(c + 1) * tk, :]
            for r in range(2):
                sc = _dot_nt(qs[r], kc)
                s_ref[cur, r, c] = sc
                f = _lane_fold(sc, jnp.maximum)
                m[r] = f if m[r] is None else jnp.maximum(m[r], f)

                e = _exp2_minus(s_ref[prv, r, c], mp[r])
                d = _dot(e.astype(BF16), vc)
                f = _lane_fold(e, jnp.add)
                acc[r] = d if acc[r] is None else acc[r] + d
                l[r] = f if l[r] is None else l[r] + f
        for r in range(2):
            m_ref[cur, r] = _lane_bcast(jnp.max(m[r], axis=-1, keepdims=True))
            o = acc[r] / jnp.sum(l[r], axis=-1, keepdims=True)
            for hh in range(half):
                h = r * half + hh
                o_ref[:, h * HEAD_DIM:(h + 1) * HEAD_DIM] = o[hh * t:(hh + 1) * t, :].astype(BF16)

    @pl.when(g % 2 == 0)
    def _():
        step(0, 1)

    @pl.when(g % 2 == 1)
    def _():
        step(1, 0)


def _axial_attention(qkv, n_seq, s, *, t=ATTN_T, tk=ATTN_TK):
    m = qkv.shape[0]
    gq = C_HEADS // C_KV_HEADS
    gw = gq * HEAD_DIM
    nq = s // t
    nc = s // tk
    n_items = n_seq * C_KV_HEADS * nq
    scored, finished = _pipeline_items(n_items, C_KV_HEADS, nq)

    def q_map(g):
        b, kv, i = scored(g)
        return (b * nq + i, QC_OFF // gw + kv)

    def k_map(g):
        b, kv, _ = scored(g)
        return (b, KC_OFF // HEAD_DIM + kv)

    def v_map(g):
        b, kv, _ = finished(g)
        return (b, VC_OFF // HEAD_DIM + kv)

    def o_map(g):
        b, kv, i = finished(g)
        return (b * nq + i, kv)

    return pl.pallas_call(
        functools.partial(_axial_body, t=t, tk=tk, nc=nc),
        out_shape=jax.ShapeDtypeStruct((m, BRANCH_WIDTH), BF16),
        grid=(n_items + 1,),
        in_specs=[
            pl.BlockSpec((t, gw), q_map),
            pl.BlockSpec((s, HEAD_DIM), k_map),
            pl.BlockSpec((s, HEAD_DIM), v_map),
        ],
        out_specs=pl.BlockSpec((t, gw), o_map),
        scratch_shapes=[pltpu.VMEM((2, 2, nc, gq // 2 * t, tk), F32), pltpu.VMEM((2, 2, gq // 2 * t, V7X_LANES), F32)],
        compiler_params=_params(("arbitrary",)),
        name="axial_attn",
    )(qkv, qkv, qkv)


def _window_body(sink_ref, q_ref, k_ref, v_ref, bias_ref, o_ref, *, nb):
    g = B_HEADS // B_KV_HEADS
    t = WINDOW
    kv = pl.program_id(1)
    sink = jnp.concatenate([jnp.full((1, t), sink_ref[kv * g + h] * LOG2E, F32) for h in range(g)], axis=1)

    def block(i, carry):
        r0 = pl.multiple_of(i * t, t)
        w0 = pl.multiple_of(jnp.clip(i - 1, 0, nb - 3) * t, t)
        tile = jnp.where(i == 0, 0, jnp.where(i == nb - 1, 2, 1))
        q = q_ref[pl.ds(r0, t), :]
        qs = jnp.concatenate([q[:, h * HEAD_DIM:(h + 1) * HEAD_DIM] for h in range(g)], axis=0)
        st = _dot_nt(k_ref[pl.ds(w0, 3 * t), :], qs) + bias_ref[tile]
        mx = jnp.maximum(jnp.max(st, axis=0, keepdims=True), sink)
        e = jnp.exp2(st - mx)
        l = jnp.sum(e, axis=0, keepdims=True) + jnp.exp2(sink - mx)
        p = (e * (1.0 / l)).astype(BF16)
        ot = lax.dot_general(v_ref[pl.ds(w0, 3 * t), :], p, (((0,), (0,)), ((), ())),
                             preferred_element_type=F32)
        for h in range(g):
            o_ref[pl.ds(r0, t), h * HEAD_DIM:(h + 1) * HEAD_DIM] = ot[:, h * t:(h + 1) * t].T.astype(BF16)
        return carry

    lax.fori_loop(0, nb, block, 0, unroll=4)


def _window_attention(qkv, bias_b, sink_logits, layer, n_seq, s):
    m = qkv.shape[0]
    g = B_HEADS // B_KV_HEADS
    gw = g * HEAD_DIM
    t = WINDOW
    nb = s // t
    grid_spec = pltpu.PrefetchScalarGridSpec(
        num_scalar_prefetch=1,
        grid=(n_seq, B_KV_HEADS),
        in_specs=[
            pl.BlockSpec((s, gw), lambda b, kv, sink: (b, QB_OFF // gw + kv)),
            pl.BlockSpec((s, HEAD_DIM), lambda b, kv, sink: (b, KB_OFF // HEAD_DIM + kv)),
            pl.BlockSpec((s, HEAD_DIM), lambda b, kv, sink: (b, VB_OFF // HEAD_DIM + kv)),
            pl.BlockSpec((3, None, 3 * t, g * t), lambda b, kv, sink: (0, kv, 0, 0)),
        ],
        out_specs=pl.BlockSpec((s, gw), lambda b, kv, sink: (b, kv)),
    )
    return pl.pallas_call(
        functools.partial(_window_body, nb=nb),
        out_shape=jax.ShapeDtypeStruct((m, BRANCH_WIDTH), BF16),
        grid_spec=grid_spec,
        compiler_params=_params(("parallel", "parallel")),
        name="window_attn",
    )(sink_logits[layer], qkv, qkv, qkv, bias_b)


def _gate_body(x_ref, ya_ref, yb_ref, yc_ref, wga_ref, wgb_ref, wgc_ref, wb_ref, o_ref, xb_ref):
    @pl.when(pl.program_id(1) == 0)
    def _():
        xb_ref[...] = x_ref[...].astype(BF16)

    xb = xb_ref[...]
    merged = None
    for n, (y_ref, wg_ref) in enumerate(((ya_ref, wga_ref), (yb_ref, wgb_ref), (yc_ref, wgc_ref))):
        gate = jax.nn.sigmoid(_dot(xb, wg_ref[...]))
        term = gate * _dot(y_ref[...], wb_ref[n])
        merged = term if merged is None else merged + term
    o_ref[...] = merged.astype(BF16)


def _outproj_body(x_ref, m_ref, wo_ref, g_ref, b_ref, o_ref, *, n_split):
    rows = x_ref.shape[0] // n_split
    for r in range(n_split):
        sl = slice(r * rows, (r + 1) * rows)
        y = ALPHA * x_ref[sl, :] + _dot(m_ref[sl, :], wo_ref[...])
        o_ref[sl, :] = _layer_norm(y, g_ref[...], b_ref[...])


def _merge(x, ya, yb, yc, w_in, w_branch, w_out, ln_g, ln_b, layer, *, tm=ROW_TILE, tc=GATE_TC):
    m = x.shape[0]
    yspec = pl.BlockSpec((tm, BRANCH_WIDTH), lambda i, j: (i, 0))

    def gate_spec(n):
        return pl.BlockSpec((None, D_MODEL, tc), lambda i, j: (layer, 0, (QKV_WIDTH + n * D_MODEL) // tc + j))

    merged = pl.pallas_call(
        _gate_body,
        out_shape=jax.ShapeDtypeStruct((m, D_MODEL), BF16),
        grid=(m // tm, D_MODEL // tc),
        in_specs=[
            pl.BlockSpec((tm, D_MODEL), lambda i, j: (i, 0)),
            yspec, yspec, yspec,
            gate_spec(0), gate_spec(1), gate_spec(2),
            pl.BlockSpec((None, N_BRANCH, BRANCH_WIDTH, tc), lambda i, j: (layer, 0, 0, j)),
        ],
        out_specs=pl.BlockSpec((tm, tc), lambda i, j: (i, j)),
        scratch_shapes=[pltpu.VMEM((tm, D_MODEL), BF16)],
        compiler_params=_params(("parallel", "arbitrary")),
        name="gate_merge",
    )(x, ya, yb, yc, w_in, w_in, w_in, w_branch)
    return pl.pallas_call(
        functools.partial(_outproj_body, n_split=2),
        out_shape=jax.ShapeDtypeStruct((m, D_MODEL), F32),
        grid=(m // tm,),
        in_specs=[
            pl.BlockSpec((tm, D_MODEL), lambda i: (i, 0)),
            pl.BlockSpec((tm, D_MODEL), lambda i: (i, 0)),
            pl.BlockSpec((None, D_MODEL, D_MODEL), lambda i: (layer, 0, 0), pipeline_mode=pl.Buffered(1)),
            pl.BlockSpec((None, None, 1, D_MODEL), lambda i: (layer, 1, 0, 0)),
            pl.BlockSpec((None, None, 1, D_MODEL), lambda i: (layer, 1, 0, 0)),
        ],
        out_specs=pl.BlockSpec((tm, D_MODEL), lambda i: (i, 0)),
        compiler_params=_params(("parallel",)),
        name="out_proj",
    )(x, merged, w_out, ln_g, ln_b)


def _memkv_body(m_ref, w_ref, o_ref):
    o_ref[...] = _dot(m_ref[...].astype(BF16), w_ref[...]).astype(BF16)


def _memkv(mem, w_kv, layer):
    rows = mem.shape[0]
    return pl.pallas_call(
        _memkv_body,
        out_shape=jax.ShapeDtypeStruct((rows, 2 * MEM_INNER), BF16),
        grid=(rows // MEM_TOKENS,),
        in_specs=[
            pl.BlockSpec((MEM_TOKENS, D_MODEL), lambda i: (i, 0)),
            pl.BlockSpec((None, D_MODEL, 2 * MEM_INNER), lambda i: (layer, 0, 0)),
        ],
        out_specs=pl.BlockSpec((MEM_TOKENS, 2 * MEM_INNER), lambda i: (i, 0)),
        compiler_params=_params(("parallel",)),
        name="memkv",
    )(mem, w_kv)


def _mem_body(x_ref, kv_ref, wq_ref, wo_ref, g_ref, b_ref, o_ref, *, n_split):
    kv = kv_ref[...]
    rows = x_ref.shape[0] // n_split
    for r in range(n_split):
        sl = slice(r * rows, (r + 1) * rows)
        x = x_ref[sl, :]
        q = (_dot(x.astype(BF16), wq_ref[...]) * QSCALE).astype(BF16)
        heads = []
        for h in range(MEM_HEADS):
            sc = _dot_nt(q[:, h * HEAD_DIM:(h + 1) * HEAD_DIM], kv[:, h * HEAD_DIM:(h + 1) * HEAD_DIM])
            e = jnp.exp2(sc - jnp.max(sc, axis=-1, keepdims=True))
            l = jnp.sum(e, axis=-1, keepdims=True)
            v = kv[:, MEM_INNER + h * HEAD_DIM:MEM_INNER + (h + 1) * HEAD_DIM]
            heads.append((_dot(e.astype(BF16), v) / l).astype(BF16))
        o = jnp.concatenate(heads, axis=1)
        y = ALPHA * x + _dot(o, wo_ref[...])
        o_ref[sl, :] = _layer_norm(y, g_ref[...], b_ref[...])


def _mem_attention(x, kv, w_q, w_o, ln_g, ln_b, layer, s, *, tm=ROW_TILE):
    m = x.shape[0]
    per_seq = s // tm
    return pl.pallas_call(
        functools.partial(_mem_body, n_split=1),
        out_shape=jax.ShapeDtypeStruct((m, D_MODEL), F32),
        grid=(m // tm,),
        in_specs=[
            pl.BlockSpec((tm, D_MODEL), lambda i: (i, 0)),
            pl.BlockSpec((MEM_TOKENS, 2 * MEM_INNER), lambda i: (i // per_seq, 0)),
            pl.BlockSpec((None, D_MODEL, MEM_INNER), lambda i: (layer, 0, 0)),
            pl.BlockSpec((None, MEM_INNER, D_MODEL), lambda i: (layer, 0, 0)),
            pl.BlockSpec((None, None, 1, D_MODEL), lambda i: (layer, 2, 0, 0)),
            pl.BlockSpec((None, None, 1, D_MODEL), lambda i: (layer, 2, 0, 0)),
        ],
        out_specs=pl.BlockSpec((tm, D_MODEL), lambda i: (i, 0)),
        compiler_params=_params(("parallel",)),
        name="mem_attn",
    )(x, kv, w_q, w_o, ln_g, ln_b)


def _t5_bucket_np(rel):
    half = NUM_BUCKETS // 2
    max_exact = half // 2
    rel = np.asarray(rel, np.int32)
    ret = np.where(rel > 0, half, 0)
    n = np.abs(rel)
    ratio = np.log(np.maximum(n, 1).astype(np.float32) / np.float32(max_exact)) / np.float32(math.log(MAX_DISTANCE / max_exact))
    large = max_exact + (ratio * np.float32(half - max_exact)).astype(np.int32)
    large = np.minimum(large, half - 1)
    return (ret + np.where(n < max_exact, n, large)).astype(np.int32)


def _lookup_bias(table, buckets):
    n_heads = table.shape[1]
    b = jnp.asarray(buckets)[None]
    tab = table * LOG2E
    out = jnp.zeros((n_heads,) + buckets.shape, F32)
    for k in range(NUM_BUCKETS):
        out = jnp.where(b == k, tab[k].reshape((n_heads,) + (1,) * buckets.ndim), out)
    return out


def _diff_bias_tiles(table_a, t, tk):
    assert tk >= MAX_DISTANCE and t % tk == 0
    ii = np.arange(t)[:, None]
    jj = np.arange(tk)[None, :]
    buckets = np.stack([_t5_bucket_np(d * tk + jj - ii) for d in range(-2, t // tk + 2)])
    return _lookup_bias(table_a, buckets)


def _window_bias_tiles(table_b):
    g = B_HEADS // B_KV_HEADS
    rel = np.stack([(np.arange(3 * WINDOW) - p * WINDOW)[None, :] - np.arange(WINDOW)[:, None] for p in range(3)])
    bias = _lookup_bias(table_b, _t5_bucket_np(rel))
    bias = jnp.where(jnp.asarray(np.abs(rel) <= WINDOW)[None], bias, NEG_INF)
    bias = bias.reshape(B_KV_HEADS, g, 3, WINDOW, 3 * WINDOW)
    return jnp.transpose(bias, (2, 0, 4, 1, 3)).reshape(3, B_KV_HEADS, 3 * WINDOW, g * WINDOW)


def _rope_tables(s):
    n_rows = s // GRID_W
    row = np.repeat(np.arange(n_rows), GRID_W).astype(np.float32)
    col = np.tile(np.arange(GRID_W), n_rows).astype(np.float32)
    axis_dims = HEAD_DIM // 2
    inv = ROPE_THETA ** (-jnp.arange(0, axis_dims, 2, dtype=F32) / axis_dims)
    ang = jnp.concatenate([row[:, None] * inv[None, :], col[:, None] * inv[None, :]], axis=-1)
    cos, sin = jnp.cos(ang), jnp.sin(ang)
    return jnp.concatenate([cos, cos], axis=-1), jnp.concatenate([-sin, sin], axis=-1)


def _run_trunk(x, mem, wts, table_a, bias_b):
    n_seq, s, _ = x.shape
    x = x.reshape(n_seq * s, D_MODEL)
    mem = mem.reshape(n_seq * MEM_TOKENS, D_MODEL)
    cos2, sin2 = _rope_tables(s)
    bias_a = _diff_bias_tiles(table_a, _diff_rows(s), ATTN_TK)
    for i in range(DEPTH):
        lambda_init = 0.8 - 0.6 * math.exp(-0.3 * i)
        x = _ffn(x, wts["w_ffn_gu"], wts["w_ffn_d"], wts["ln_g"], wts["ln_b"], i, 0, 0)
        qkv = _qkv(x, wts["w_in"], wts["qk_norm_g"], cos2, sin2, i)
        ya = _diff_attention(qkv, bias_a, wts["lambda_qk"], wts["subln_g"], i, n_seq, s, lambda_init)
        yb = _window_attention(qkv, bias_b, wts["sink_logits"], i, n_seq, s)
        yc = _axial_attention(qkv, n_seq, s)
        x = _merge(x, ya, yb, yc, wts["w_in"], wts["w_branch"], wts["w_out"], wts["ln_g"], wts["ln_b"], i)
        kv = _memkv(mem, wts["w_mem_kv"], i)
        x = _mem_attention(x, kv, wts["w_mem_q"], wts["w_mem_o"], wts["ln_g"], wts["ln_b"], i, s)
        x = _ffn(x, wts["w_ffn_gu"], wts["w_ffn_d"], wts["ln_g"], wts["ln_b"], i, 1, 3)
    return x.reshape(n_seq, s, D_MODEL)


def kernel(x_prompt, x_sample, mem_prompt, mem_sample, rel_bias_table, w_in, w_branch, w_out, lambda_qk, subln_g, sink_logits, qk_norm_g, w_mem_q, w_mem_kv, w_mem_o, w_ffn_gu, w_ffn_d, ln_g, ln_b):
    wts = {
        "w_in": w_in.astype(BF16),
        "w_branch": w_branch.astype(BF16),
        "w_out": w_out.astype(BF16),
        "w_mem_q": w_mem_q.astype(BF16),
        "w_mem_kv": w_mem_kv.astype(BF16),
        "w_mem_o": w_mem_o.astype(BF16),
        "w_ffn_gu": w_ffn_gu.astype(BF16),
        "w_ffn_d": w_ffn_d.astype(BF16),
        "lambda_qk": lambda_qk,
        "subln_g": subln_g.reshape(DEPTH, 1, A_V),
        "sink_logits": sink_logits,
        "qk_norm_g": qk_norm_g,
        "ln_g": ln_g.reshape(DEPTH, 4, 1, D_MODEL),
        "ln_b": ln_b.reshape(DEPTH, 4, 1, D_MODEL),
    }
    table_a = rel_bias_table[:, :A_HEADS]
    bias_b = _window_bias_tiles(rel_bias_table[:, A_HEADS:])
    y_prompt = _run_trunk(x_prompt, mem_prompt, wts, table_a, bias_b)
    y_sample = _run_trunk(x_sample, mem_sample, wts, table_a, bias_b)
    return (y_prompt, y_sample)
```

```python
import functools
import math

import numpy as np
import jax
import jax.numpy as jnp
from jax import lax
from jax.experimental import pallas as pl
from jax.experimental.pallas import tpu as pltpu

D_MODEL = 2048
DEPTH = 4
HEAD_DIM = 128
A_HEADS = 4
A_V = 2 * HEAD_DIM
B_HEADS = 8
B_KV_HEADS = 2
WINDOW = 128
C_HEADS = 8
C_KV_HEADS = 2
GRID_W = 64
ROPE_THETA = 10000.0
NUM_BUCKETS = 32
MAX_DISTANCE = 128
MEM_TOKENS = 256
MEM_HEADS = 4
MEM_INNER = MEM_HEADS * HEAD_DIM
D_FF = 5632
N_BRANCH = 3
BRANCH_WIDTH = 1024
QKV_WIDTH = 6144
ALPHA = (2 * DEPTH) ** 0.25
LN_EPS = 1e-5
RMS_EPS = 1e-6
NEG_INF = -1e30
LOG2E = math.log2(math.e)
QSCALE = HEAD_DIM ** -0.5 * LOG2E

QA_OFF, KA_OFF, VA_OFF = 0, 1024, 2048
QB_OFF, KB_OFF, VB_OFF = 3072, 4096, 4352
QC_OFF, KC_OFF, VC_OFF = 4608, 5632, 5888

V7X_VMEM_BYTES = 64 * 1024 * 1024
V7X_LANES = 128
VMEM_LIMIT = V7X_VMEM_BYTES - 8 * 1024 * 1024

ROW_TILE = 512
FFN_TF = 1024
QKV_TN = 256
GATE_TC = 512
ATTN_T = 256
ATTN_TK = 256
F32 = jnp.float32
BF16 = jnp.bfloat16


def _params(semantics):
    return pltpu.CompilerParams(dimension_semantics=semantics, vmem_limit_bytes=VMEM_LIMIT)


def _layer_norm(y, g, b):
    mu = jnp.mean(y, axis=-1, keepdims=True)
    d = y - mu
    var = jnp.mean(d * d, axis=-1, keepdims=True)
    return d * lax.rsqrt(var + LN_EPS) * g + b


def _dot(a, b):
    return jnp.dot(a, b, preferred_element_type=F32)


def _dot_nt(a, b):
    return lax.dot_general(a, b, (((1,), (1,)), ((), ())), preferred_element_type=F32)


def _lane_groups(x):
    return [x[:, j * V7X_LANES:(j + 1) * V7X_LANES] for j in range(x.shape[1] // V7X_LANES)]


def _lane_fold(x, op):
    return functools.reduce(op, _lane_groups(x))


def _exp2_minus(x, col):
    parts = [jnp.exp2(p - col) for p in _lane_groups(x)]
    return parts[0] if len(parts) == 1 else jnp.concatenate(parts, axis=1)


def _lane_bcast(col):
    return jnp.broadcast_to(col, (col.shape[0], V7X_LANES))


def _ffn_body(idx_ref, x_ref, wgu_hbm, wd_hbm, g_ref, b_ref, o_ref, xb_ref, acc_ref, wg_buf, wu_buf, wd_buf, sem,
              *, tf, n_tiles):
    i = pl.program_id(0)
    layer = idx_ref[0]
    which = idx_ref[1]
    chunks = [(off, min(tf, D_FF - off)) for off in range(0, D_FF, tf)]
    n_c = len(chunks)

    def chunk_copies(c):
        off, w = chunks[c]
        slot = c % 2
        return (
            pltpu.make_async_copy(wgu_hbm.at[layer, which, :, pl.ds(off, w)], wg_buf.at[slot, :, pl.ds(0, w)], sem.at[0, slot]),
            pltpu.make_async_copy(wgu_hbm.at[layer, which, :, pl.ds(D_FF + off, w)], wu_buf.at[slot, :, pl.ds(0, w)], sem.at[1, slot]),
            pltpu.make_async_copy(wd_hbm.at[layer, which, pl.ds(off, w), :], wd_buf.at[slot, pl.ds(0, w), :], sem.at[2, slot]),
        )

    def start(c):
        for cp in chunk_copies(c):
            cp.start()

    def wait(c):
        for cp in chunk_copies(c):
            cp.wait()

    @pl.when(i == 0)
    def _():
        start(0)

    xb_ref[...] = x_ref[...].astype(BF16)
    acc_ref[...] = jnp.zeros_like(acc_ref)
    wait(0)
    xb = xb_ref[...]
    for c in range(n_c):
        w = chunks[c][1]
        slot = c % 2
        if c + 1 < n_c:
            start(c + 1)
        else:
            @pl.when(i + 1 < n_tiles)
            def _():
                start(0)
        h = _dot(xb, wg_buf[slot, :, 0:w])
        u = _dot(xb, wu_buf[slot, :, 0:w])
        a = (h * jax.nn.sigmoid(h) * u).astype(BF16)
        acc_ref[...] += _dot(a, wd_buf[slot, 0:w, :])
        if c + 1 < n_c:
            wait(c + 1)

    y = ALPHA * x_ref[...] + 0.5 * acc_ref[...]
    o_ref[...] = _layer_norm(y, g_ref[...], b_ref[...])


def _ffn(x, w_gu, w_d, ln_g, ln_b, layer, which, ln_idx, *, tm=ROW_TILE, tf=FFN_TF):
    m = x.shape[0]
    n_tiles = m // tm
    assert pl.cdiv(D_FF, tf) % 2 == 0, "the next tile's chunk 0 must reuse the slot the last chunk leaves free"
    idx = jnp.array([layer, which, ln_idx], jnp.int32)
    ln_spec = pl.BlockSpec((None, None, 1, D_MODEL), lambda i, idx: (idx[0], idx[2], 0, 0))
    grid_spec = pltpu.PrefetchScalarGridSpec(
        num_scalar_prefetch=1,
        grid=(n_tiles,),
        in_specs=[
            pl.BlockSpec((tm, D_MODEL), lambda i, idx: (i, 0)),
            pl.BlockSpec(memory_space=pl.ANY),
            pl.BlockSpec(memory_space=pl.ANY),
            ln_spec, ln_spec,
        ],
        out_specs=pl.BlockSpec((tm, D_MODEL), lambda i, idx: (i, 0)),
        scratch_shapes=[
            pltpu.VMEM((tm, D_MODEL), BF16), pltpu.VMEM((tm, D_MODEL), F32),
            pltpu.VMEM((2, D_MODEL, tf), BF16), pltpu.VMEM((2, D_MODEL, tf), BF16),
            pltpu.VMEM((2, tf, D_MODEL), BF16),
            pltpu.SemaphoreType.DMA((3, 2)),
        ],
    )
    return pl.pallas_call(
        functools.partial(_ffn_body, tf=tf, n_tiles=n_tiles),
        out_shape=jax.ShapeDtypeStruct((m, D_MODEL), F32),
        grid_spec=grid_spec,
        compiler_params=_params(("arbitrary",)),
        name="ffn",
    )(idx, x, w_gu, w_d, ln_g, ln_b)


def _qkv_body(x_ref, w_ref, gn_ref, cos_ref, sin_ref, o_ref, *, tn):
    xb = x_ref[...].astype(BF16)
    cos = cos_ref[...]
    sin = sin_ref[...]
    starts = [j * tn for j in range(QKV_WIDTH // tn)]
    for c0 in sorted(starts, key=lambda c: not (QC_OFF <= c < VC_OFF)):
        acc = _dot(xb, w_ref[:, c0:c0 + tn])
        is_q = (QA_OFF <= c0 < KA_OFF) or (QB_OFF <= c0 < KB_OFF) or (QC_OFF <= c0 < KC_OFF)
        if QC_OFF <= c0 < VC_OFF:
            g = gn_ref[0:1, :] if c0 < KC_OFF else gn_ref[1:2, :]
            for hh in range(tn // HEAD_DIM):
                a = acc[:, hh * HEAD_DIM:(hh + 1) * HEAD_DIM]
                r = lax.rsqrt(jnp.mean(a * a, axis=-1, keepdims=True) + RMS_EPS)
                y = a * r * g
                y = y * cos + pltpu.roll(y, HEAD_DIM // 2, 1) * sin
                if is_q:
                    y = y * QSCALE
                o_ref[:, c0 + hh * HEAD_DIM:c0 + (hh + 1) * HEAD_DIM] = y.astype(BF16)
        else:
            if is_q:
                acc = acc * QSCALE
            o_ref[:, c0:c0 + tn] = acc.astype(BF16)


def _qkv(x, w_qkv, qk_norm_g, cos2, sin2, layer, *, tm=ROW_TILE, tn=QKV_TN):
    m = x.shape[0]
    s = cos2.shape[0]
    n_pos = s // tm
    return pl.pallas_call(
        functools.partial(_qkv_body, tn=tn),
        out_shape=jax.ShapeDtypeStruct((m, QKV_WIDTH), BF16),
        grid=(m // tm,),
        in_specs=[
            pl.BlockSpec((tm, D_MODEL), lambda i: (i, 0)),
            pl.BlockSpec((None, D_MODEL, QKV_WIDTH), lambda i: (layer, 0, 0), pipeline_mode=pl.Buffered(1)),
            pl.BlockSpec((None, 2, HEAD_DIM), lambda i: (layer, 0, 0)),
            pl.BlockSpec((tm, HEAD_DIM), lambda i: (i % n_pos, 0)),
            pl.BlockSpec((tm, HEAD_DIM), lambda i: (i % n_pos, 0)),
        ],
        out_specs=pl.BlockSpec((tm, QKV_WIDTH), lambda i: (i, 0)),
        compiler_params=_params(("parallel",)),
        name="qkv",
    )(x, w_qkv, qk_norm_g, cos2, sin2)


def _diff_body(q_ref, k_ref, v_ref, bias_ref, lam_ref, sg_ref, o_ref, s1_ref, s2_ref, m_ref, *, t, tk, nc, nq, n_items, lambda_init):
    g = pl.program_id(0)
    i = jnp.minimum(g, n_items - 1) % nq

    @pl.when(g == 0)
    def _():
        s1_ref[1] = jnp.zeros(s1_ref.shape[1:], F32)
        s2_ref[1] = jnp.zeros(s2_ref.shape[1:], F32)
        m_ref[1] = jnp.zeros(m_ref.shape[1:], F32)

    def step(cur, prv):
        q = q_ref[...]
        q1 = q[:, :HEAD_DIM]
        q2 = q[:, HEAD_DIM:]
        m1p = m_ref[prv, 0]
        m2p = m_ref[prv, 1]
        m1 = m2 = acc = l1 = l2 = None
        for c in range(nc):
            kc = k_ref[c * tk:(c + 1) * tk, :]
            b = bias_ref[jnp.clip(c - i * (t // tk), -2, t // tk + 1) + 2]
            s1 = _dot_nt(q1, kc[:, :HEAD_DIM]) + b
            s2 = _dot_nt(q2, kc[:, HEAD_DIM:]) + b
            s1_ref[cur, c] = s1
            s2_ref[cur, c] = s2
            f1 = _lane_fold(s1, jnp.maximum)
            f2 = _lane_fold(s2, jnp.maximum)
            m1 = f1 if m1 is None else jnp.maximum(m1, f1)
            m2 = f2 if m2 is None else jnp.maximum(m2, f2)

            e1 = _exp2_minus(s1_ref[prv, c], m1p)
            e2 = _exp2_minus(s2_ref[prv, c], m2p)
            e = jnp.concatenate([e1.astype(BF16), e2.astype(BF16)], axis=0)
            d = _dot(e, v_ref[c * tk:(c + 1) * tk, :])
            f1 = _lane_fold(e1, jnp.add)
            f2 = _lane_fold(e2, jnp.add)
            acc = d if acc is None else acc + d
            l1 = f1 if l1 is None else l1 + f1
            l2 = f2 if l2 is None else l2 + f2
        m_ref[cur, 0] = _lane_bcast(jnp.max(m1, axis=-1, keepdims=True))
        m_ref[cur, 1] = _lane_bcast(jnp.max(m2, axis=-1, keepdims=True))

        l1 = jnp.sum(l1, axis=-1, keepdims=True)
        l2 = jnp.sum(l2, axis=-1, keepdims=True)
        lam = lam_ref[...]
        lam_full = (jnp.exp(jnp.sum(lam[0:1] * lam[1:2], axis=-1, keepdims=True))
                    - jnp.exp(jnp.sum(lam[2:3] * lam[3:4], axis=-1, keepdims=True)) + lambda_init)
        o = acc[0:t, :] / l1 - lam_full * (acc[t:2 * t, :] / l2)
        o = o * lax.rsqrt(jnp.mean(o * o, axis=-1, keepdims=True) + RMS_EPS) * sg_ref[...]
        o_ref[...] = (o * (1.0 - lambda_init)).astype(BF16)

    @pl.when(g % 2 == 0)
    def _():
        step(0, 1)

    @pl.when(g % 2 == 1)
    def _():
        step(1, 0)


def _pipeline_items(n_items, n_heads, nq):
    def scored(g):
        it = jnp.minimum(g, n_items - 1)
        return it // (n_heads * nq), (it // nq) % n_heads, it % nq

    def finished(g):
        it = jnp.maximum(g - 1, 0)
        return it // (n_heads * nq), (it // nq) % n_heads, it % nq

    return scored, finished


def _diff_rows(s):
    return 2 * ATTN_T if 32 * ATTN_T * s <= 16 * 1024 * 1024 else ATTN_T


def _diff_attention(qkv, bias_a, lambda_qk, subln_g, layer, n_seq, s, lambda_init, *, tk=ATTN_TK):
    t = _diff_rows(s)
    m = qkv.shape[0]
    nc = s // tk
    nq = s // t
    n_tiles = t // tk + 4
    n_items = n_seq * A_HEADS * nq
    scored, finished = _pipeline_items(n_items, A_HEADS, nq)

    def q_map(g):
        b, h, i = scored(g)
        return (b * nq + i, QA_OFF // A_V + h)

    def k_map(g):
        b, h, _ = scored(g)
        return (b, KA_OFF // A_V + h)

    def v_map(g):
        b, h, _ = finished(g)
        return (b, VA_OFF // A_V + h)

    def o_map(g):
        b, h, i = finished(g)
        return (b * nq + i, h)

    return pl.pallas_call(
        functools.partial(_diff_body, t=t, tk=tk, nc=nc, nq=nq, n_items=n_items, lambda_init=lambda_init),
        out_shape=jax.ShapeDtypeStruct((m, BRANCH_WIDTH), BF16),
        grid=(n_items + 1,),
        in_specs=[
            pl.BlockSpec((t, A_V), q_map),
            pl.BlockSpec((s, A_V), k_map),
            pl.BlockSpec((s, A_V), v_map),
            pl.BlockSpec((None, n_tiles, t, tk), lambda g: (scored(g)[1], 0, 0, 0)),
            pl.BlockSpec((None, 4, HEAD_DIM), lambda g: (layer, 0, 0)),
            pl.BlockSpec((None, 1, A_V), lambda g: (layer, 0, 0)),
        ],
        out_specs=pl.BlockSpec((t, A_V), o_map),
        scratch_shapes=[pltpu.VMEM((2, nc, t, tk), F32), pltpu.VMEM((2, nc, t, tk), F32), pltpu.VMEM((2, 2, t, V7X_LANES), F32)],
        compiler_params=_params(("arbitrary",)),
        name="diff_attn",
    )(qkv, qkv, qkv, bias_a, lambda_qk, subln_g)


def _axial_body(q_ref, k_ref, v_ref, o_ref, s_ref, m_ref, *, t, tk, nc):
    gq = C_HEADS // C_KV_HEADS
    g = pl.program_id(0)
    half = gq // 2

    @pl.when(g == 0)
    def _():
        s_ref[1] = jnp.zeros(s_ref.shape[1:], F32)
        m_ref[1] = jnp.zeros(m_ref.shape[1:], F32)

    def step(cur, prv):
        q = q_ref[...]
        qs = [jnp.concatenate([q[:, h * HEAD_DIM:(h + 1) * HEAD_DIM] for h in range(r * half, (r + 1) * half)], axis=0)
              for r in range(2)]
        mp = [m_ref[prv, r] for r in range(2)]
        m = [None, None]
        acc = [None, None]
        l = [None, None]
        for c in range(nc):
            kc = k_ref[c * tk:(c + 1) * tk, :]
            vc = v_ref[c * tk:(c + 1) * tk, :]
            for r in range(2):
                sc = _dot_nt(qs[r], kc)
                s_ref[cur, r, c] = sc
                f = _lane_fold(sc, jnp.maximum)
                m[r] = f if m[r] is None else jnp.maximum(m[r], f)

                e = _exp2_minus(s_ref[prv, r, c], mp[r])
                d = _dot(e.astype(BF16), vc)
                f = _lane_fold(e, jnp.add)
                acc[r] = d if acc[r] is None else acc[r] + d
                l[r] = f if l[r] is None else l[r] + f
        for r in range(2):
            m_ref[cur, r] = _lane_bcast(jnp.max(m[r], axis=-1, keepdims=True))
            o = acc[r] / jnp.sum(l[r], axis=-1, keepdims=True)
            for hh in range(half):
                h = r * half + hh
                o_ref[:, h * HEAD_DIM:(h + 1) * HEAD_DIM] = o[hh * t:(hh + 1) * t, :].astype(BF16)

    @pl.when(g % 2 == 0)
    def _():
        step(0, 1)

    @pl.when(g % 2 == 1)
    def _():
        step(1, 0)


def _axial_attention(qkv, n_seq, s, *, t=ATTN_T, tk=ATTN_TK):
    m = qkv.shape[0]
    gq = C_HEADS // C_KV_HEADS
    gw = gq * HEAD_DIM
    nq = s // t
    nc = s // tk
    n_items = n_seq * C_KV_HEADS * nq
    scored, finished = _pipeline_items(n_items, C_KV_HEADS, nq)

    def q_map(g):
        b, kv, i = scored(g)
        return (b * nq + i, QC_OFF // gw + kv)

    def k_map(g):
        b, kv, _ = scored(g)
        return (b, KC_OFF // HEAD_DIM + kv)

    def v_map(g):
        b, kv, _ = finished(g)
        return (b, VC_OFF // HEAD_DIM + kv)

    def o_map(g):
        b, kv, i = finished(g)
        return (b * nq + i, kv)

    return pl.pallas_call(
        functools.partial(_axial_body, t=t, tk=tk, nc=nc),
        out_shape=jax.ShapeDtypeStruct((m, BRANCH_WIDTH), BF16),
        grid=(n_items + 1,),
        in_specs=[
            pl.BlockSpec((t, gw), q_map),
            pl.BlockSpec((s, HEAD_DIM), k_map),
            pl.BlockSpec((s, HEAD_DIM), v_map),
        ],
        out_specs=pl.BlockSpec((t, gw), o_map),
        scratch_shapes=[pltpu.VMEM((2, 2, nc, gq // 2 * t, tk), F32), pltpu.VMEM((2, 2, gq // 2 * t, V7X_LANES), F32)],
        compiler_params=_params(("arbitrary",)),
        name="axial_attn",
    )(qkv, qkv, qkv)


def _window_body(sink_ref, q_ref, k_ref, v_ref, bias_ref, o_ref, *, nb):
    g = B_HEADS // B_KV_HEADS
    t = WINDOW
    kv = pl.program_id(1)
    sink = jnp.concatenate([jnp.full((1, t), sink_ref[kv * g + h] * LOG2E, F32) for h in range(g)], axis=1)

    def block(i, carry):
        r0 = pl.multiple_of(i * t, t)
        w0 = pl.multiple_of(jnp.clip(i - 1, 0, nb - 3) * t, t)
        tile = jnp.where(i == 0, 0, jnp.where(i == nb - 1, 2, 1))
        q = q_ref[pl.ds(r0, t), :]
        qs = jnp.concatenate([q[:, h * HEAD_DIM:(h + 1) * HEAD_DIM] for h in range(g)], axis=0)
        st = _dot_nt(k_ref[pl.ds(w0, 3 * t), :], qs) + bias_ref[tile]
        mx = jnp.maximum(jnp.max(st, axis=0, keepdims=True), sink)
        e = jnp.exp2(st - mx)
        l = jnp.sum(e, axis=0, keepdims=True) + jnp.exp2(sink - mx)
        p = (e * (1.0 / l)).astype(BF16)
        ot = lax.dot_general(v_ref[pl.ds(w0, 3 * t), :], p, (((0,), (0,)), ((), ())),
                             preferred_element_type=F32)
        for h in range(g):
            o_ref[pl.ds(r0, t), h * HEAD_DIM:(h + 1) * HEAD_DIM] = ot[:, h * t:(h + 1) * t].T.astype(BF16)
        return carry

    lax.fori_loop(0, nb, block, 0, unroll=4)


def _window_attention(qkv, bias_b, sink_logits, layer, n_seq, s):
    m = qkv.shape[0]
    g = B_HEADS // B_KV_HEADS
    gw = g * HEAD_DIM
    t = WINDOW
    nb = s // t
    grid_spec = pltpu.PrefetchScalarGridSpec(
        num_scalar_prefetch=1,
        grid=(n_seq, B_KV_HEADS),
        in_specs=[
            pl.BlockSpec((s, gw), lambda b, kv, sink: (b, QB_OFF // gw + kv)),
            pl.BlockSpec((s, HEAD_DIM), lambda b, kv, sink: (b, KB_OFF // HEAD_DIM + kv)),
            pl.BlockSpec((s, HEAD_DIM), lambda b, kv, sink: (b, VB_OFF // HEAD_DIM + kv)),
            pl.BlockSpec((3, None, 3 * t, g * t), lambda b, kv, sink: (0, kv, 0, 0)),
        ],
        out_specs=pl.BlockSpec((s, gw), lambda b, kv, sink: (b, kv)),
    )
    return pl.pallas_call(
        functools.partial(_window_body, nb=nb),
        out_shape=jax.ShapeDtypeStruct((m, BRANCH_WIDTH), BF16),
        grid_spec=grid_spec,
        compiler_params=_params(("parallel", "parallel")),
        name="window_attn",
    )(sink_logits[layer], qkv, qkv, qkv, bias_b)


def _gate_body(idx_ref, x_ref, ya_ref, yb_ref, yc_ref, win_hbm, wbr_hbm, o_ref, xb_ref, wg_buf, wb_buf, sem, *, tc, n_tiles):
    i = pl.program_id(0)
    layer = idx_ref[0]
    n_c = D_MODEL // tc

    def chunk_copies(c):
        slot = c % 2
        gates = [pltpu.make_async_copy(win_hbm.at[layer, :, pl.ds(QKV_WIDTH + n * D_MODEL + c * tc, tc)],
                                       wg_buf.at[slot, n], sem.at[n, slot]) for n in range(N_BRANCH)]
        branch = pltpu.make_async_copy(wbr_hbm.at[layer, :, :, pl.ds(c * tc, tc)], wb_buf.at[slot], sem.at[N_BRANCH, slot])
        return gates + [branch]

    def start(c):
        for cp in chunk_copies(c):
            cp.start()

    def wait(c):
        for cp in chunk_copies(c):
            cp.wait()

    @pl.when(i == 0)
    def _():
        start(0)

    xb_ref[...] = x_ref[...].astype(BF16)
    wait(0)
    xb = xb_ref[...]
    for c in range(n_c):
        slot = c % 2
        if c + 1 < n_c:
            start(c + 1)
        else:
            @pl.when(i + 1 < n_tiles)
            def _():
                start(0)
        merged = None
        for n, y_ref in enumerate((ya_ref, yb_ref, yc_ref)):
            gate = jax.nn.sigmoid(_dot(xb, wg_buf[slot, n]))
            term = gate * _dot(y_ref[...], wb_buf[slot, n])
            merged = term if merged is None else merged + term
        o_ref[:, c * tc:(c + 1) * tc] = merged.astype(BF16)
        if c + 1 < n_c:
            wait(c + 1)


def _outproj_body(x_ref, m_ref, wo_ref, g_ref, b_ref, o_ref, *, n_split):
    rows = x_ref.shape[0] // n_split
    for r in range(n_split):
        sl = slice(r * rows, (r + 1) * rows)
        y = ALPHA * x_ref[sl, :] + _dot(m_ref[sl, :], wo_ref[...])
        o_ref[sl, :] = _layer_norm(y, g_ref[...], b_ref[...])


def _merge(x, ya, yb, yc, w_in, w_branch, w_out, ln_g, ln_b, layer, *, tm=ROW_TILE, tc=GATE_TC):
    m = x.shape[0]
    n_tiles = m // tm
    assert (D_MODEL // tc) % 2 == 0, "the next tile's chunk 0 must reuse the slot the last chunk leaves free"
    yspec = pl.BlockSpec((tm, BRANCH_WIDTH), lambda i, idx: (i, 0))
    grid_spec = pltpu.PrefetchScalarGridSpec(
        num_scalar_prefetch=1,
        grid=(n_tiles,),
        in_specs=[
            pl.BlockSpec((tm, D_MODEL), lambda i, idx: (i, 0)),
            yspec, yspec, yspec,
            pl.BlockSpec(memory_space=pl.ANY),
            pl.BlockSpec(memory_space=pl.ANY),
        ],
        out_specs=pl.BlockSpec((tm, D_MODEL), lambda i, idx: (i, 0)),
        scratch_shapes=[
            pltpu.VMEM((tm, D_MODEL), BF16),
            pltpu.VMEM((2, N_BRANCH, D_MODEL, tc), BF16),
            pltpu.VMEM((2, N_BRANCH, BRANCH_WIDTH, tc), BF16),
            pltpu.SemaphoreType.DMA((N_BRANCH + 1, 2)),
        ],
    )
    merged = pl.pallas_call(
        functools.partial(_gate_body, tc=tc, n_tiles=n_tiles),
        out_shape=jax.ShapeDtypeStruct((m, D_MODEL), BF16),
        grid_spec=grid_spec,
        compiler_params=_params(("arbitrary",)),
        name="gate_merge",
    )(jnp.array([layer], jnp.int32), x, ya, yb, yc, w_in, w_branch)
    return pl.pallas_call(
        functools.partial(_outproj_body, n_split=2),
        out_shape=jax.ShapeDtypeStruct((m, D_MODEL), F32),
        grid=(m // tm,),
        in_specs=[
            pl.BlockSpec((tm, D_MODEL), lambda i: (i, 0)),
            pl.BlockSpec((tm, D_MODEL), lambda i: (i, 0)),
            pl.BlockSpec((None, D_MODEL, D_MODEL), lambda i: (layer, 0, 0), pipeline_mode=pl.Buffered(1)),
            pl.BlockSpec((None, None, 1, D_MODEL), lambda i: (layer, 1, 0, 0)),
            pl.BlockSpec((None, None, 1, D_MODEL), lambda i: (layer, 1, 0, 0)),
        ],
        out_specs=pl.BlockSpec((tm, D_MODEL), lambda i: (i, 0)),
        compiler_params=_params(("parallel",)),
        name="out_proj",
    )(x, merged, w_out, ln_g, ln_b)


def _memkv_body(m_ref, w_ref, o_ref):
    o_ref[...] = _dot(m_ref[...].astype(BF16), w_ref[...]).astype(BF16)


def _memkv(mem, w_kv, layer):
    rows = mem.shape[0]
    return pl.pallas_call(
        _memkv_body,
        out_shape=jax.ShapeDtypeStruct((rows, 2 * MEM_INNER), BF16),
        grid=(rows // MEM_TOKENS,),
        in_specs=[
            pl.BlockSpec((MEM_TOKENS, D_MODEL), lambda i: (i, 0)),
            pl.BlockSpec((None, D_MODEL, 2 * MEM_INNER), lambda i: (layer, 0, 0)),
        ],
        out_specs=pl.BlockSpec((MEM_TOKENS, 2 * MEM_INNER), lambda i: (i, 0)),
        compiler_params=_params(("parallel",)),
        name="memkv",
    )(mem, w_kv)


def _mem_body(x_ref, kv_ref, wq_ref, wo_ref, g_ref, b_ref, o_ref, *, n_split):
    kv = kv_ref[...]
    rows = x_ref.shape[0] // n_split
    for r in range(n_split):
        sl = slice(r * rows, (r + 1) * rows)
        x = x_ref[sl, :]
        q = (_dot(x.astype(BF16), wq_ref[...]) * QSCALE).astype(BF16)
        heads = []
        for h in range(MEM_HEADS):
            sc = _dot_nt(q[:, h * HEAD_DIM:(h + 1) * HEAD_DIM], kv[:, h * HEAD_DIM:(h + 1) * HEAD_DIM])
            e = jnp.exp2(sc - jnp.max(sc, axis=-1, keepdims=True))
            l = jnp.sum(e, axis=-1, keepdims=True)
            v = kv[:, MEM_INNER + h * HEAD_DIM:MEM_INNER + (h + 1) * HEAD_DIM]
            heads.append((_dot(e.astype(BF16), v) / l).astype(BF16))
        o = jnp.concatenate(heads, axis=1)
        y = ALPHA * x + _dot(o, wo_ref[...])
        o_ref[sl, :] = _layer_norm(y, g_ref[...], b_ref[...])


def _mem_attention(x, kv, w_q, w_o, ln_g, ln_b, layer, s, *, tm=ROW_TILE):
    m = x.shape[0]
    per_seq = s // tm
    return pl.pallas_call(
        functools.partial(_mem_body, n_split=1),
        out_shape=jax.ShapeDtypeStruct((m, D_MODEL), F32),
        grid=(m // tm,),
        in_specs=[
            pl.BlockSpec((tm, D_MODEL), lambda i: (i, 0)),
            pl.BlockSpec((MEM_TOKENS, 2 * MEM_INNER), lambda i: (i // per_seq, 0)),
            pl.BlockSpec((None, D_MODEL, MEM_INNER), lambda i: (layer, 0, 0)),
            pl.BlockSpec((None, MEM_INNER, D_MODEL), lambda i: (layer, 0, 0)),
            pl.BlockSpec((None, None, 1, D_MODEL), lambda i: (layer, 2, 0, 0)),
            pl.BlockSpec((None, None, 1, D_MODEL), lambda i: (layer, 2, 0, 0)),
        ],
        out_specs=pl.BlockSpec((tm, D_MODEL), lambda i: (i, 0)),
        compiler_params=_params(("parallel",)),
        name="mem_attn",
    )(x, kv, w_q, w_o, ln_g, ln_b)


def _t5_bucket_np(rel):
    half = NUM_BUCKETS // 2
    max_exact = half // 2
    rel = np.asarray(rel, np.int32)
    ret = np.where(rel > 0, half, 0)
    n = np.abs(rel)
    ratio = np.log(np.maximum(n, 1).astype(np.float32) / np.float32(max_exact)) / np.float32(math.log(MAX_DISTANCE / max_exact))
    large = max_exact + (ratio * np.float32(half - max_exact)).astype(np.int32)
    large = np.minimum(large, half - 1)
    return (ret + np.where(n < max_exact, n, large)).astype(np.int32)


def _lookup_bias(table, buckets):
    n_heads = table.shape[1]
    b = jnp.asarray(buckets)[None]
    tab = table * LOG2E
    out = jnp.zeros((n_heads,) + buckets.shape, F32)
    for k in range(NUM_BUCKETS):
        out = jnp.where(b == k, tab[k].reshape((n_heads,) + (1,) * buckets.ndim), out)
    return out


def _diff_bias_tiles(table_a, t, tk):
    assert tk >= MAX_DISTANCE and t % tk == 0
    ii = np.arange(t)[:, None]
    jj = np.arange(tk)[None, :]
    buckets = np.stack([_t5_bucket_np(d * tk + jj - ii) for d in range(-2, t // tk + 2)])
    return _lookup_bias(table_a, buckets)


def _window_bias_tiles(table_b):
    g = B_HEADS // B_KV_HEADS
    rel = np.stack([(np.arange(3 * WINDOW) - p * WINDOW)[None, :] - np.arange(WINDOW)[:, None] for p in range(3)])
    bias = _lookup_bias(table_b, _t5_bucket_np(rel))
    bias = jnp.where(jnp.asarray(np.abs(rel) <= WINDOW)[None], bias, NEG_INF)
    bias = bias.reshape(B_KV_HEADS, g, 3, WINDOW, 3 * WINDOW)
    return jnp.transpose(bias, (2, 0, 4, 1, 3)).reshape(3, B_KV_HEADS, 3 * WINDOW, g * WINDOW)


def _rope_tables(s):
    n_rows = s // GRID_W
    row = np.repeat(np.arange(n_rows), GRID_W).astype(np.float32)
    col = np.tile(np.arange(GRID_W), n_rows).astype(np.float32)
    axis_dims = HEAD_DIM // 2
    inv = ROPE_THETA ** (-jnp.arange(0, axis_dims, 2, dtype=F32) / axis_dims)
    ang = jnp.concatenate([row[:, None] * inv[None, :], col[:, None] * inv[None, :]], axis=-1)
    cos, sin = jnp.cos(ang), jnp.sin(ang)
    return jnp.concatenate([cos, cos], axis=-1), jnp.concatenate([-sin, sin], axis=-1)


def _run_trunk(x, mem, wts, table_a, bias_b):
    n_seq, s, _ = x.shape
    x = x.reshape(n_seq * s, D_MODEL)
    mem = mem.reshape(n_seq * MEM_TOKENS, D_MODEL)
    cos2, sin2 = _rope_tables(s)
    bias_a = _diff_bias_tiles(table_a, _diff_rows(s), ATTN_TK)
    for i in range(DEPTH):
        lambda_init = 0.8 - 0.6 * math.exp(-0.3 * i)
        x = _ffn(x, wts["w_ffn_gu"], wts["w_ffn_d"], wts["ln_g"], wts["ln_b"], i, 0, 0)
        qkv = _qkv(x, wts["w_in"], wts["qk_norm_g"], cos2, sin2, i)
        ya = _diff_attention(qkv, bias_a, wts["lambda_qk"], wts["subln_g"], i, n_seq, s, lambda_init)
        yb = _window_attention(qkv, bias_b, wts["sink_logits"], i, n_seq, s)
        yc = _axial_attention(qkv, n_seq, s)
        x = _merge(x, ya, yb, yc, wts["w_in"], wts["w_branch"], wts["w_out"], wts["ln_g"], wts["ln_b"], i)
        kv = _memkv(mem, wts["w_mem_kv"], i)
        x = _mem_attention(x, kv, wts["w_mem_q"], wts["w_mem_o"], wts["ln_g"], wts["ln_b"], i, s)
        x = _ffn(x, wts["w_ffn_gu"], wts["w_ffn_d"], wts["ln_g"], wts["ln_b"], i, 1, 3)
    return x.reshape(n_seq, s, D_MODEL)


def kernel(x_prompt, x_sample, mem_prompt, mem_sample, rel_bias_table, w_in, w_branch, w_out, lambda_qk, subln_g, sink_logits, qk_norm_g, w_mem_q, w_mem_kv, w_mem_o, w_ffn_gu, w_ffn_d, ln_g, ln_b):
    wts = {
        "w_in": w_in.astype(BF16),
        "w_branch": w_branch.astype(BF16),
        "w_out": w_out.astype(BF16),
        "w_mem_q": w_mem_q.astype(BF16),
        "w_mem_kv": w_mem_kv.astype(BF16),
        "w_mem_o": w_mem_o.astype(BF16),
        "w_ffn_gu": w_ffn_gu.astype(BF16),
        "w_ffn_d": w_ffn_d.astype(BF16),
        "lambda_qk": lambda_qk,
        "subln_g": subln_g.reshape(DEPTH, 1, A_V),
        "sink_logits": sink_logits,
        "qk_norm_g": qk_norm_g,
        "ln_g": ln_g.reshape(DEPTH, 4, 1, D_MODEL),
        "ln_b": ln_b.reshape(DEPTH, 4, 1, D_MODEL),
    }
    table_a = rel_bias_table[:, :A_HEADS]
    bias_b = _window_bias_tiles(rel_bias_table[:, A_HEADS:])
    y_prompt = _run_trunk(x_prompt, mem_prompt, wts, table_a, bias_b)
    y_sample = _run_trunk(x_sample, mem_sample, wts, table_a, bias_b)
    return (y_prompt, y_sample)
```

```python
import functools
import math

import numpy as np
import jax
import jax.numpy as jnp
from jax import lax
from jax.experimental import pallas as pl
from jax.experimental.pallas import tpu as pltpu

D_MODEL = 2048
DEPTH = 4
HEAD_DIM = 128
A_HEADS = 4
A_V = 2 * HEAD_DIM
B_HEADS = 8
B_KV_HEADS = 2
WINDOW = 128
C_HEADS = 8
C_KV_HEADS = 2
GRID_W = 64
ROPE_THETA = 10000.0
NUM_BUCKETS = 32
MAX_DISTANCE = 128
MEM_TOKENS = 256
MEM_HEADS = 4
MEM_INNER = MEM_HEADS * HEAD_DIM
D_FF = 5632
N_BRANCH = 3
BRANCH_WIDTH = 1024
QKV_WIDTH = 6144
ALPHA = (2 * DEPTH) ** 0.25
LN_EPS = 1e-5
RMS_EPS = 1e-6
NEG_INF = -1e30
LOG2E = math.log2(math.e)
QSCALE = HEAD_DIM ** -0.5 * LOG2E

QA_OFF, KA_OFF, VA_OFF = 0, 1024, 2048
QB_OFF, KB_OFF, VB_OFF = 3072, 4096, 4352
QC_OFF, KC_OFF, VC_OFF = 4608, 5632, 5888

V7X_VMEM_BYTES = 64 * 1024 * 1024
V7X_LANES = 128
VMEM_LIMIT = V7X_VMEM_BYTES - 8 * 1024 * 1024

ROW_TILE = 512
FFN_TF = 1024
QKV_TN = 256
GATE_TC = 512
ATTN_T = 256
ATTN_TK = 256
F32 = jnp.float32
BF16 = jnp.bfloat16


def _params(semantics):
    return pltpu.CompilerParams(dimension_semantics=semantics, vmem_limit_bytes=VMEM_LIMIT)


def _layer_norm(y, g, b):
    mu = jnp.mean(y, axis=-1, keepdims=True)
    d = y - mu
    var = jnp.mean(d * d, axis=-1, keepdims=True)
    return d * lax.rsqrt(var + LN_EPS) * g + b


def _dot(a, b):
    return jnp.dot(a, b, preferred_element_type=F32)


def _dot_nt(a, b):
    return lax.dot_general(a, b, (((1,), (1,)), ((), ())), preferred_element_type=F32)


def _lane_groups(x):
    return [x[:, j * V7X_LANES:(j + 1) * V7X_LANES] for j in range(x.shape[1] // V7X_LANES)]


def _lane_fold(x, op):
    return functools.reduce(op, _lane_groups(x))


def _exp2_minus(x, col):
    parts = [jnp.exp2(p - col) for p in _lane_groups(x)]
    return parts[0] if len(parts) == 1 else jnp.concatenate(parts, axis=1)


def _lane_bcast(col):
    return jnp.broadcast_to(col, (col.shape[0], V7X_LANES))


def _ffn_body(idx_ref, x_ref, wgu_hbm, wd_hbm, g_ref, b_ref, o_ref, xb_ref, acc_ref, wg_buf, wu_buf, wd_buf, sem,
              *, tf, n_tiles):
    i = pl.program_id(0)
    layer = idx_ref[0]
    which = idx_ref[1]
    chunks = [(off, min(tf, D_FF - off)) for off in range(0, D_FF, tf)]
    n_c = len(chunks)

    def chunk_copies(c):
        off, w = chunks[c]
        slot = c % 2
        return (
            pltpu.make_async_copy(wgu_hbm.at[layer, which, :, pl.ds(off, w)], wg_buf.at[slot, :, pl.ds(0, w)], sem.at[0, slot]),
            pltpu.make_async_copy(wgu_hbm.at[layer, which, :, pl.ds(D_FF + off, w)], wu_buf.at[slot, :, pl.ds(0, w)], sem.at[1, slot]),
            pltpu.make_async_copy(wd_hbm.at[layer, which, pl.ds(off, w), :], wd_buf.at[slot, pl.ds(0, w), :], sem.at[2, slot]),
        )

    def start(c):
        for cp in chunk_copies(c):
            cp.start()

    def wait(c):
        for cp in chunk_copies(c):
            cp.wait()

    @pl.when(i == 0)
    def _():
        start(0)

    xb_ref[...] = x_ref[...].astype(BF16)
    acc_ref[...] = jnp.zeros_like(acc_ref)
    wait(0)
    xb = xb_ref[...]
    for c in range(n_c):
        w = chunks[c][1]
        slot = c % 2
        if c + 1 < n_c:
            start(c + 1)
        else:
            @pl.when(i + 1 < n_tiles)
            def _():
                start(0)
        h = _dot(xb, wg_buf[slot, :, 0:w])
        u = _dot(xb, wu_buf[slot, :, 0:w])
        a = (h * jax.nn.sigmoid(h) * u).astype(BF16)
        acc_ref[...] += _dot(a, wd_buf[slot, 0:w, :])
        if c + 1 < n_c:
            wait(c + 1)

    y = ALPHA * x_ref[...] + 0.5 * acc_ref[...]
    o_ref[...] = _layer_norm(y, g_ref[...], b_ref[...])


def _ffn(x, w_gu, w_d, ln_g, ln_b, layer, which, ln_idx, *, tm=ROW_TILE, tf=FFN_TF):
    m = x.shape[0]
    n_tiles = m // tm
    assert pl.cdiv(D_FF, tf) % 2 == 0, "the next tile's chunk 0 must reuse the slot the last chunk leaves free"
    idx = jnp.array([layer, which, ln_idx], jnp.int32)
    ln_spec = pl.BlockSpec((None, None, 1, D_MODEL), lambda i, idx: (idx[0], idx[2], 0, 0))
    grid_spec = pltpu.PrefetchScalarGridSpec(
        num_scalar_prefetch=1,
        grid=(n_tiles,),
        in_specs=[
            pl.BlockSpec((tm, D_MODEL), lambda i, idx: (i, 0)),
            pl.BlockSpec(memory_space=pl.ANY),
            pl.BlockSpec(memory_space=pl.ANY),
            ln_spec, ln_spec,
        ],
        out_specs=pl.BlockSpec((tm, D_MODEL), lambda i, idx: (i, 0)),
        scratch_shapes=[
            pltpu.VMEM((tm, D_MODEL), BF16), pltpu.VMEM((tm, D_MODEL), F32),
            pltpu.VMEM((2, D_MODEL, tf), BF16), pltpu.VMEM((2, D_MODEL, tf), BF16),
            pltpu.VMEM((2, tf, D_MODEL), BF16),
            pltpu.SemaphoreType.DMA((3, 2)),
        ],
    )
    return pl.pallas_call(
        functools.partial(_ffn_body, tf=tf, n_tiles=n_tiles),
        out_shape=jax.ShapeDtypeStruct((m, D_MODEL), F32),
        grid_spec=grid_spec,
        compiler_params=_params(("arbitrary",)),
        name="ffn",
    )(idx, x, w_gu, w_d, ln_g, ln_b)


def _qkv_body(x_ref, w_ref, gn_ref, cos_ref, sin_ref, o_ref, *, tn):
    xb = x_ref[...].astype(BF16)
    cos = cos_ref[...]
    sin = sin_ref[...]
    starts = [j * tn for j in range(QKV_WIDTH // tn)]
    for c0 in sorted(starts, key=lambda c: not (QC_OFF <= c < VC_OFF)):
        acc = _dot(xb, w_ref[:, c0:c0 + tn])
        is_q = (QA_OFF <= c0 < KA_OFF) or (QB_OFF <= c0 < KB_OFF) or (QC_OFF <= c0 < KC_OFF)
        if QC_OFF <= c0 < VC_OFF:
            g = gn_ref[0:1, :] if c0 < KC_OFF else gn_ref[1:2, :]
            for hh in range(tn // HEAD_DIM):
                a = acc[:, hh * HEAD_DIM:(hh + 1) * HEAD_DIM]
                r = lax.rsqrt(jnp.mean(a * a, axis=-1, keepdims=True) + RMS_EPS)
                y = a * r * g
                y = y * cos + pltpu.roll(y, HEAD_DIM // 2, 1) * sin
                if is_q:
                    y = y * QSCALE
                o_ref[:, c0 + hh * HEAD_DIM:c0 + (hh + 1) * HEAD_DIM] = y.astype(BF16)
        else:
            if is_q:
                acc = acc * QSCALE
            o_ref[:, c0:c0 + tn] = acc.astype(BF16)


def _qkv(x, w_qkv, qk_norm_g, cos2, sin2, layer, *, tm=ROW_TILE, tn=QKV_TN):
    m = x.shape[0]
    s = cos2.shape[0]
    n_pos = s // tm
    return pl.pallas_call(
        functools.partial(_qkv_body, tn=tn),
        out_shape=jax.ShapeDtypeStruct((m, QKV_WIDTH), BF16),
        grid=(m // tm,),
        in_specs=[
            pl.BlockSpec((tm, D_MODEL), lambda i: (i, 0)),
            pl.BlockSpec((None, D_MODEL, QKV_WIDTH), lambda i: (layer, 0, 0), pipeline_mode=pl.Buffered(1)),
            pl.BlockSpec((None, 2, HEAD_DIM), lambda i: (layer, 0, 0)),
            pl.BlockSpec((tm, HEAD_DIM), lambda i: (i % n_pos, 0)),
            pl.BlockSpec((tm, HEAD_DIM), lambda i: (i % n_pos, 0)),
        ],
        out_specs=pl.BlockSpec((tm, QKV_WIDTH), lambda i: (i, 0)),
        compiler_params=_params(("parallel",)),
        name="qkv",
    )(x, w_qkv, qk_norm_g, cos2, sin2)


def _diff_body(q_ref, k_ref, v_ref, bias_ref, lam_ref, sg_ref, o_ref, s1_ref, s2_ref, m_ref, *, t, tk, nc, nq, n_items, lambda_init):
    g = pl.program_id(0)
    i = jnp.minimum(g, n_items - 1) % nq

    @pl.when(g == 0)
    def _():
        s1_ref[1] = jnp.zeros(s1_ref.shape[1:], F32)
        s2_ref[1] = jnp.zeros(s2_ref.shape[1:], F32)
        m_ref[1] = jnp.zeros(m_ref.shape[1:], F32)

    def step(cur, prv):
        q = q_ref[...]
        q1 = q[:, :HEAD_DIM]
        q2 = q[:, HEAD_DIM:]
        m1p = m_ref[prv, 0]
        m2p = m_ref[prv, 1]
        m1 = m2 = acc = l1 = l2 = None
        for c in range(nc):
            kc = k_ref[c * tk:(c + 1) * tk, :]
            b = bias_ref[jnp.clip(c - i * (t // tk), -2, t // tk + 1) + 2]
            s1 = _dot_nt(q1, kc[:, :HEAD_DIM]) + b
            s2 = _dot_nt(q2, kc[:, HEAD_DIM:]) + b
            s1_ref[cur, c] = s1
            s2_ref[cur, c] = s2
            f1 = _lane_fold(s1, jnp.maximum)
            f2 = _lane_fold(s2, jnp.maximum)
            m1 = f1 if m1 is None else jnp.maximum(m1, f1)
            m2 = f2 if m2 is None else jnp.maximum(m2, f2)

            e1 = _exp2_minus(s1_ref[prv, c], m1p)
            e2 = _exp2_minus(s2_ref[prv, c], m2p)
            e = jnp.concatenate([e1.astype(BF16), e2.astype(BF16)], axis=0)
            d = _dot(e, v_ref[c * tk:(c + 1) * tk, :])
            f1 = _lane_fold(e1, jnp.add)
            f2 = _lane_fold(e2, jnp.add)
            acc = d if acc is None else acc + d
            l1 = f1 if l1 is None else l1 + f1
            l2 = f2 if l2 is None else l2 + f2
        m_ref[cur, 0] = _lane_bcast(jnp.max(m1, axis=-1, keepdims=True))
        m_ref[cur, 1] = _lane_bcast(jnp.max(m2, axis=-1, keepdims=True))

        l1 = jnp.sum(l1, axis=-1, keepdims=True)
        l2 = jnp.sum(l2, axis=-1, keepdims=True)
        lam = lam_ref[...]
        lam_full = (jnp.exp(jnp.sum(lam[0:1] * lam[1:2], axis=-1, keepdims=True))
                    - jnp.exp(jnp.sum(lam[2:3] * lam[3:4], axis=-1, keepdims=True)) + lambda_init)
        o = acc[0:t, :] / l1 - lam_full * (acc[t:2 * t, :] / l2)
        o = o * lax.rsqrt(jnp.mean(o * o, axis=-1, keepdims=True) + RMS_EPS) * sg_ref[...]
        o_ref[...] = (o * (1.0 - lambda_init)).astype(BF16)

    @pl.when(g % 2 == 0)
    def _():
        step(0, 1)

    @pl.when(g % 2 == 1)
    def _():
        step(1, 0)


def _pipeline_items(n_items, n_heads, nq):
    def scored(g):
        it = jnp.minimum(g, n_items - 1)
        return it // (n_heads * nq), (it // nq) % n_heads, it % nq

    def finished(g):
        it = jnp.maximum(g - 1, 0)
        return it // (n_heads * nq), (it // nq) % n_heads, it % nq

    return scored, finished


def _diff_rows(s):
    return 2 * ATTN_T if 32 * ATTN_T * s <= 16 * 1024 * 1024 else ATTN_T


def _diff_attention(qkv, bias_a, lambda_qk, subln_g, layer, n_seq, s, lambda_init, *, tk=ATTN_TK):
    t = _diff_rows(s)
    m = qkv.shape[0]
    nc = s // tk
    nq = s // t
    n_tiles = t // tk + 4
    n_items = n_seq * A_HEADS * nq
    scored, finished = _pipeline_items(n_items, A_HEADS, nq)

    def q_map(g):
        b, h, i = scored(g)
        return (b * nq + i, QA_OFF // A_V + h)

    def k_map(g):
        b, h, _ = scored(g)
        return (b, KA_OFF // A_V + h)

    def v_map(g):
        b, h, _ = finished(g)
        return (b, VA_OFF // A_V + h)

    def o_map(g):
        b, h, i = finished(g)
        return (b * nq + i, h)

    return pl.pallas_call(
        functools.partial(_diff_body, t=t, tk=tk, nc=nc, nq=nq, n_items=n_items, lambda_init=lambda_init),
        out_shape=jax.ShapeDtypeStruct((m, BRANCH_WIDTH), BF16),
        grid=(n_items + 1,),
        in_specs=[
            pl.BlockSpec((t, A_V), q_map),
            pl.BlockSpec((s, A_V), k_map),
            pl.BlockSpec((s, A_V), v_map),
            pl.BlockSpec((None, n_tiles, t, tk), lambda g: (scored(g)[1], 0, 0, 0)),
            pl.BlockSpec((None, 4, HEAD_DIM), lambda g: (layer, 0, 0)),
            pl.BlockSpec((None, 1, A_V), lambda g: (layer, 0, 0)),
        ],
        out_specs=pl.BlockSpec((t, A_V), o_map),
        scratch_shapes=[pltpu.VMEM((2, nc, t, tk), F32), pltpu.VMEM((2, nc, t, tk), F32), pltpu.VMEM((2, 2, t, V7X_LANES), F32)],
        compiler_params=_params(("arbitrary",)),
        name="diff_attn",
    )(qkv, qkv, qkv, bias_a, lambda_qk, subln_g)


def _axial_body(q_ref, k_ref, v_ref, o_ref, s_ref, m_ref, *, t, tk, nc):
    gq = C_HEADS // C_KV_HEADS
    g = pl.program_id(0)
    half = gq // 2

    @pl.when(g == 0)
    def _():
        s_ref[1] = jnp.zeros(s_ref.shape[1:], F32)
        m_ref[1] = jnp.zeros(m_ref.shape[1:], F32)

    def step(cur, prv):
        q = q_ref[...]
        qs = [jnp.concatenate([q[:, h * HEAD_DIM:(h + 1) * HEAD_DIM] for h in range(r * half, (r + 1) * half)], axis=0)
              for r in range(2)]
        mp = [m_ref[prv, r] for r in range(2)]
        m = [None, None]
        acc = [None, None]
        l = [None, None]
        for c in range(nc):
            kc = k_ref[c * tk:(c + 1) * tk, :]
            vc = v_ref[c * tk:(c + 1) * tk, :]
            for r in range(2):
                sc = _dot_nt(qs[r], kc)
                s_ref[cur, r, c] = sc
                f = _lane_fold(sc, jnp.maximum)
                m[r] = f if m[r] is None else jnp.maximum(m[r], f)

                e = _exp2_minus(s_ref[prv, r, c], mp[r])
                d = _dot(e.astype(BF16), vc)
                f = _lane_fold(e, jnp.add)
                acc[r] = d if acc[r] is None else acc[r] + d
                l[r] = f if l[r] is None else l[r] + f
        for r in range(2):
            m_ref[cur, r] = _lane_bcast(jnp.max(m[r], axis=-1, keepdims=True))
            o = acc[r] / jnp.sum(l[r], axis=-1, keepdims=True)
            for hh in range(half):
                h = r * half + hh
                o_ref[:, h * HEAD_DIM:(h + 1) * HEAD_DIM] = o[hh * t:(hh + 1) * t, :].astype(BF16)

    @pl.when(g % 2 == 0)
    def _():
        step(0, 1)

    @pl.when(g % 2 == 1)
    def _():
        step(1, 0)


def _axial_rows(s):
    return 2 * ATTN_T if 64 * ATTN_T * s <= 32 * 1024 * 1024 else ATTN_T


def _axial_attention(qkv, n_seq, s, *, tk=ATTN_TK):
    t = _axial_rows(s)
    m = qkv.shape[0]
    gq = C_HEADS // C_KV_HEADS
    gw = gq * HEAD_DIM
    nq = s // t
    nc = s // tk
    n_items = n_seq * C_KV_HEADS * nq
    scored, finished = _pipeline_items(n_items, C_KV_HEADS, nq)

    def q_map(g):
        b, kv, i = scored(g)
        return (b * nq + i, QC_OFF // gw + kv)

    def k_map(g):
        b, kv, _ = scored(g)
        return (b, KC_OFF // HEAD_DIM + kv)

    def v_map(g):
        b, kv, _ = finished(g)
        return (b, VC_OFF // HEAD_DIM + kv)

    def o_map(g):
        b, kv, i = finished(g)
        return (b * nq + i, kv)

    return pl.pallas_call(
        functools.partial(_axial_body, t=t, tk=tk, nc=nc),
        out_shape=jax.ShapeDtypeStruct((m, BRANCH_WIDTH), BF16),
        grid=(n_items + 1,),
        in_specs=[
            pl.BlockSpec((t, gw), q_map),
            pl.BlockSpec((s, HEAD_DIM), k_map),
            pl.BlockSpec((s, HEAD_DIM), v_map),
        ],
        out_specs=pl.BlockSpec((t, gw), o_map),
        scratch_shapes=[pltpu.VMEM((2, 2, nc, gq // 2 * t, tk), F32), pltpu.VMEM((2, 2, gq // 2 * t, V7X_LANES), F32)],
        compiler_params=_params(("arbitrary",)),
        name="axial_attn",
    )(qkv, qkv, qkv)


def _window_body(sink_ref, q_ref, k_ref, v_ref, bias_ref, o_ref, *, nb):
    g = B_HEADS // B_KV_HEADS
    t = WINDOW
    kv = pl.program_id(1)
    sink = jnp.concatenate([jnp.full((1, t), sink_ref[kv * g + h] * LOG2E, F32) for h in range(g)], axis=1)

    def block(i, carry):
        r0 = pl.multiple_of(i * t, t)
        w0 = pl.multiple_of(jnp.clip(i - 1, 0, nb - 3) * t, t)
        tile = jnp.where(i == 0, 0, jnp.where(i == nb - 1, 2, 1))
        q = q_ref[pl.ds(r0, t), :]
        qs = jnp.concatenate([q[:, h * HEAD_DIM:(h + 1) * HEAD_DIM] for h in range(g)], axis=0)
        st = _dot_nt(k_ref[pl.ds(w0, 3 * t), :], qs) + bias_ref[tile]
        mx = jnp.maximum(jnp.max(st, axis=0, keepdims=True), sink)
        e = jnp.exp2(st - mx)
        l = jnp.sum(e, axis=0, keepdims=True) + jnp.exp2(sink - mx)
        p = (e * (1.0 / l)).astype(BF16)
        ot = lax.dot_general(v_ref[pl.ds(w0, 3 * t), :], p, (((0,), (0,)), ((), ())),
                             preferred_element_type=F32)
        for h in range(g):
            o_ref[pl.ds(r0, t), h * HEAD_DIM:(h + 1) * HEAD_DIM] = ot[:, h * t:(h + 1) * t].T.astype(BF16)
        return carry

    lax.fori_loop(0, nb, block, 0, unroll=4)


def _window_attention(qkv, bias_b, sink_logits, layer, n_seq, s):
    m = qkv.shape[0]
    g = B_HEADS // B_KV_HEADS
    gw = g * HEAD_DIM
    t = WINDOW
    nb = s // t
    grid_spec = pltpu.PrefetchScalarGridSpec(
        num_scalar_prefetch=1,
        grid=(n_seq, B_KV_HEADS),
        in_specs=[
            pl.BlockSpec((s, gw), lambda b, kv, sink: (b, QB_OFF // gw + kv)),
            pl.BlockSpec((s, HEAD_DIM), lambda b, kv, sink: (b, KB_OFF // HEAD_DIM + kv)),
            pl.BlockSpec((s, HEAD_DIM), lambda b, kv, sink: (b, VB_OFF // HEAD_DIM + kv)),
            pl.BlockSpec((3, None, 3 * t, g * t), lambda b, kv, sink: (0, kv, 0, 0)),
        ],
        out_specs=pl.BlockSpec((s, gw), lambda b, kv, sink: (b, kv)),
    )
    return pl.pallas_call(
        functools.partial(_window_body, nb=nb),
        out_shape=jax.ShapeDtypeStruct((m, BRANCH_WIDTH), BF16),
        grid_spec=grid_spec,
        compiler_params=_params(("parallel", "parallel")),
        name="window_attn",
    )(sink_logits[layer], qkv, qkv, qkv, bias_b)


def _gate_body(x_ref, ya_ref, yb_ref, yc_ref, wga_ref, wgb_ref, wgc_ref, wb_ref, o_ref, xb_ref):
    @pl.when(pl.program_id(1) == 0)
    def _():
        xb_ref[...] = x_ref[...].astype(BF16)

    xb = xb_ref[...]
    merged = None
    for n, (y_ref, wg_ref) in enumerate(((ya_ref, wga_ref), (yb_ref, wgb_ref), (yc_ref, wgc_ref))):
        gate = jax.nn.sigmoid(_dot(xb, wg_ref[...]))
        term = gate * _dot(y_ref[...], wb_ref[n])
        merged = term if merged is None else merged + term
    o_ref[...] = merged.astype(BF16)


def _outproj_body(x_ref, m_ref, wo_ref, g_ref, b_ref, o_ref, *, n_split):
    rows = x_ref.shape[0] // n_split
    for r in range(n_split):
        sl = slice(r * rows, (r + 1) * rows)
        y = ALPHA * x_ref[sl, :] + _dot(m_ref[sl, :], wo_ref[...])
        o_ref[sl, :] = _layer_norm(y, g_ref[...], b_ref[...])


def _merge(x, ya, yb, yc, w_in, w_branch, w_out, ln_g, ln_b, layer, *, tm=ROW_TILE, tc=GATE_TC):
    m = x.shape[0]
    yspec = pl.BlockSpec((tm, BRANCH_WIDTH), lambda i, j: (i, 0))

    def gate_spec(n):
        return pl.BlockSpec((None, D_MODEL, tc), lambda i, j: (layer, 0, (QKV_WIDTH + n * D_MODEL) // tc + j))

    merged = pl.pallas_call(
        _gate_body,
        out_shape=jax.ShapeDtypeStruct((m, D_MODEL), BF16),
        grid=(m // tm, D_MODEL // tc),
        in_specs=[
            pl.BlockSpec((tm, D_MODEL), lambda i, j: (i, 0)),
            yspec, yspec, yspec,
            gate_spec(0), gate_spec(1), gate_spec(2),
            pl.BlockSpec((None, N_BRANCH, BRANCH_WIDTH, tc), lambda i, j: (layer, 0, 0, j)),
        ],
        out_specs=pl.BlockSpec((tm, tc), lambda i, j: (i, j)),
        scratch_shapes=[pltpu.VMEM((tm, D_MODEL), BF16)],
        compiler_params=_params(("parallel", "arbitrary")),
        name="gate_merge",
    )(x, ya, yb, yc, w_in, w_in, w_in, w_branch)
    return pl.pallas_call(
        functools.partial(_outproj_body, n_split=2),
        out_shape=jax.ShapeDtypeStruct((m, D_MODEL), F32),
        grid=(m // tm,),
        in_specs=[
            pl.BlockSpec((tm, D_MODEL), lambda i: (i, 0)),
            pl.BlockSpec((tm, D_MODEL), lambda i: (i, 0)),
            pl.BlockSpec((None, D_MODEL, D_MODEL), lambda i: (layer, 0, 0), pipeline_mode=pl.Buffered(1)),
            pl.BlockSpec((None, None, 1, D_MODEL), lambda i: (layer, 1, 0, 0)),
            pl.BlockSpec((None, None, 1, D_MODEL), lambda i: (layer, 1, 0, 0)),
        ],
        out_specs=pl.BlockSpec((tm, D_MODEL), lambda i: (i, 0)),
        compiler_params=_params(("parallel",)),
        name="out_proj",
    )(x, merged, w_out, ln_g, ln_b)


def _memkv_body(m_ref, w_ref, o_ref):
    o_ref[...] = _dot(m_ref[...].astype(BF16), w_ref[...]).astype(BF16)


def _memkv(mem, w_kv, layer):
    rows = mem.shape[0]
    return pl.pallas_call(
        _memkv_body,
        out_shape=jax.ShapeDtypeStruct((rows, 2 * MEM_INNER), BF16),
        grid=(rows // MEM_TOKENS,),
        in_specs=[
            pl.BlockSpec((MEM_TOKENS, D_MODEL), lambda i: (i, 0)),
            pl.BlockSpec((None, D_MODEL, 2 * MEM_INNER), lambda i: (layer, 0, 0)),
        ],
        out_specs=pl.BlockSpec((MEM_TOKENS, 2 * MEM_INNER), lambda i: (i, 0)),
        compiler_params=_params(("parallel",)),
        name="memkv",
    )(mem, w_kv)


def _mem_body(x_ref, kv_ref, wq_ref, wo_ref, g_ref, b_ref, o_ref, *, n_split):
    kv = kv_ref[...]
    rows = x_ref.shape[0] // n_split
    for r in range(n_split):
        sl = slice(r * rows, (r + 1) * rows)
        x = x_ref[sl, :]
        q = (_dot(x.astype(BF16), wq_ref[...]) * QSCALE).astype(BF16)
        heads = []
        for h in range(MEM_HEADS):
            sc = _dot_nt(q[:, h * HEAD_DIM:(h + 1) * HEAD_DIM], kv[:, h * HEAD_DIM:(h + 1) * HEAD_DIM])
            e = jnp.exp2(sc - jnp.max(sc, axis=-1, keepdims=True))
            l = jnp.sum(e, axis=-1, keepdims=True)
            v = kv[:, MEM_INNER + h * HEAD_DIM:MEM_INNER + (h + 1) * HEAD_DIM]
            heads.append((_dot(e.astype(BF16), v) / l).astype(BF16))
        o = jnp.concatenate(heads, axis=1)
        y = ALPHA * x + _dot(o, wo_ref[...])
        o_ref[sl, :] = _layer_norm(y, g_ref[...], b_ref[...])


def _mem_attention(x, kv, w_q, w_o, ln_g, ln_b, layer, s, *, tm=ROW_TILE):
    m = x.shape[0]
    per_seq = s // tm
    return pl.pallas_call(
        functools.partial(_mem_body, n_split=1),
        out_shape=jax.ShapeDtypeStruct((m, D_MODEL), F32),
        grid=(m // tm,),
        in_specs=[
            pl.BlockSpec((tm, D_MODEL), lambda i: (i, 0)),
            pl.BlockSpec((MEM_TOKENS, 2 * MEM_INNER), lambda i: (i // per_seq, 0)),
            pl.BlockSpec((None, D_MODEL, MEM_INNER), lambda i: (layer, 0, 0)),
            pl.BlockSpec((None, MEM_INNER, D_MODEL), lambda i: (layer, 0, 0)),
            pl.BlockSpec((None, None, 1, D_MODEL), lambda i: (layer, 2, 0, 0)),
            pl.BlockSpec((None, None, 1, D_MODEL), lambda i: (layer, 2, 0, 0)),
        ],
        out_specs=pl.BlockSpec((tm, D_MODEL), lambda i: (i, 0)),
        compiler_params=_params(("parallel",)),
        name="mem_attn",
    )(x, kv, w_q, w_o, ln_g, ln_b)


def _t5_bucket_np(rel):
    half = NUM_BUCKETS // 2
    max_exact = half // 2
    rel = np.asarray(rel, np.int32)
    ret = np.where(rel > 0, half, 0)
    n = np.abs(rel)
    ratio = np.log(np.maximum(n, 1).astype(np.float32) / np.float32(max_exact)) / np.float32(math.log(MAX_DISTANCE / max_exact))
    large = max_exact + (ratio * np.float32(half - max_exact)).astype(np.int32)
    large = np.minimum(large, half - 1)
    return (ret + np.where(n < max_exact, n, large)).astype(np.int32)


def _lookup_bias(table, buckets):
    n_heads = table.shape[1]
    b = jnp.asarray(buckets)[None]
    tab = table * LOG2E
    out = jnp.zeros((n_heads,) + buckets.shape, F32)
    for k in range(NUM_BUCKETS):
        out = jnp.where(b == k, tab[k].reshape((n_heads,) + (1,) * buckets.ndim), out)
    return out


def _diff_bias_tiles(table_a, t, tk):
    assert tk >= MAX_DISTANCE and t % tk == 0
    ii = np.arange(t)[:, None]
    jj = np.arange(tk)[None, :]
    buckets = np.stack([_t5_bucket_np(d * tk + jj - ii) for d in range(-2, t // tk + 2)])
    return _lookup_bias(table_a, buckets)


def _window_bias_tiles(table_b):
    g = B_HEADS // B_KV_HEADS
    rel = np.stack([(np.arange(3 * WINDOW) - p * WINDOW)[None, :] - np.arange(WINDOW)[:, None] for p in range(3)])
    bias = _lookup_bias(table_b, _t5_bucket_np(rel))
    bias = jnp.where(jnp.asarray(np.abs(rel) <= WINDOW)[None], bias, NEG_INF)
    bias = bias.reshape(B_KV_HEADS, g, 3, WINDOW, 3 * WINDOW)
    return jnp.transpose(bias, (2, 0, 4, 1, 3)).reshape(3, B_KV_HEADS, 3 * WINDOW, g * WINDOW)


def _rope_tables(s):
    n_rows = s // GRID_W
    row = np.repeat(np.arange(n_rows), GRID_W).astype(np.float32)
    col = np.tile(np.arange(GRID_W), n_rows).astype(np.float32)
    axis_dims = HEAD_DIM // 2
    inv = ROPE_THETA ** (-jnp.arange(0, axis_dims, 2, dtype=F32) / axis_dims)
    ang = jnp.concatenate([row[:, None] * inv[None, :], col[:, None] * inv[None, :]], axis=-1)
    cos, sin = jnp.cos(ang), jnp.sin(ang)
    return jnp.concatenate([cos, cos], axis=-1), jnp.concatenate([-sin, sin], axis=-1)


def _run_trunk(x, mem, wts, table_a, bias_b):
    n_seq, s, _ = x.shape
    x = x.reshape(n_seq * s, D_MODEL)
    mem = mem.reshape(n_seq * MEM_TOKENS, D_MODEL)
    cos2, sin2 = _rope_tables(s)
    bias_a = _diff_bias_tiles(table_a, _diff_rows(s), ATTN_TK)
    for i in range(DEPTH):
        lambda_init = 0.8 - 0.6 * math.exp(-0.3 * i)
        x = _ffn(x, wts["w_ffn_gu"], wts["w_ffn_d"], wts["ln_g"], wts["ln_b"], i, 0, 0)
        qkv = _qkv(x, wts["w_in"], wts["qk_norm_g"], cos2, sin2, i)
        ya = _diff_attention(qkv, bias_a, wts["lambda_qk"], wts["subln_g"], i, n_seq, s, lambda_init)
        yb = _window_attention(qkv, bias_b, wts["sink_logits"], i, n_seq, s)
        yc = _axial_attention(qkv, n_seq, s)
        x = _merge(x, ya, yb, yc, wts["w_in"], wts["w_branch"], wts["w_out"], wts["ln_g"], wts["ln_b"], i)
        kv = _memkv(mem, wts["w_mem_kv"], i)
        x = _mem_attention(x, kv, wts["w_mem_q"], wts["w_mem_o"], wts["ln_g"], wts["ln_b"], i, s)
        x = _ffn(x, wts["w_ffn_gu"], wts["w_ffn_d"], wts["ln_g"], wts["ln_b"], i, 1, 3)
    return x.reshape(n_seq, s, D_MODEL)


def kernel(x_prompt, x_sample, mem_prompt, mem_sample, rel_bias_table, w_in, w_branch, w_out, lambda_qk, subln_g, sink_logits, qk_norm_g, w_mem_q, w_mem_kv, w_mem_o, w_ffn_gu, w_ffn_d, ln_g, ln_b):
    wts = {
        "w_in": w_in.astype(BF16),
        "w_branch": w_branch.astype(BF16),
        "w_out": w_out.astype(BF16),
        "w_mem_q": w_mem_q.astype(BF16),
        "w_mem_kv": w_mem_kv.astype(BF16),
        "w_mem_o": w_mem_o.astype(BF16),
        "w_ffn_gu": w_ffn_gu.astype(BF16),
        "w_ffn_d": w_ffn_d.astype(BF16),
        "lambda_qk": lambda_qk,
        "subln_g": subln_g.reshape(DEPTH, 1, A_V),
        "sink_logits": sink_logits,
        "qk_norm_g": qk_norm_g,
        "ln_g": ln_g.reshape(DEPTH, 4, 1, D_MODEL),
        "ln_b": ln_b.reshape(DEPTH, 4, 1, D_MODEL),
    }
    table_a = rel_bias_table[:, :A_HEADS]
    bias_b = _window_bias_tiles(rel_bias_table[:, A_HEADS:])
    y_prompt = _run_trunk(x_prompt, mem_prompt, wts, table_a, bias_b)
    y_sample = _run_trunk(x_sample, mem_sample, wts, table_a, bias_b)
    return (y_prompt, y_sample)
```

```python
import functools
import math

import numpy as np
import jax
import jax.numpy as jnp
from jax import lax
from jax.experimental import pallas as pl
from jax.experimental.pallas import tpu as pltpu

D_MODEL = 2048
DEPTH = 4
HEAD_DIM = 128
A_HEADS = 4
A_V = 2 * HEAD_DIM
B_HEADS = 8
B_KV_HEADS = 2
WINDOW = 128
C_HEADS = 8
C_KV_HEADS = 2
GRID_W = 64
ROPE_THETA = 10000.0
NUM_BUCKETS = 32
MAX_DISTANCE = 128
MEM_TOKENS = 256
MEM_HEADS = 4
MEM_INNER = MEM_HEADS * HEAD_DIM
D_FF = 5632
N_BRANCH = 3
BRANCH_WIDTH = 1024
QKV_WIDTH = 6144
ALPHA = (2 * DEPTH) ** 0.25
LN_EPS = 1e-5
RMS_EPS = 1e-6
NEG_INF = -1e30
LOG2E = math.log2(math.e)
QSCALE = HEAD_DIM ** -0.5 * LOG2E

QA_OFF, KA_OFF, VA_OFF = 0, 1024, 2048
QB_OFF, KB_OFF, VB_OFF = 3072, 4096, 4352
QC_OFF, KC_OFF, VC_OFF = 4608, 5632, 5888

V7X_VMEM_BYTES = 64 * 1024 * 1024
V7X_LANES = 128
VMEM_LIMIT = V7X_VMEM_BYTES - 8 * 1024 * 1024

ROW_TILE = 512
FFN_TF = 1024
QKV_TN = 256
GATE_TC = 512
ATTN_T = 256
ATTN_TK = 256
F32 = jnp.float32
BF16 = jnp.bfloat16


def _params(semantics):
    return pltpu.CompilerParams(dimension_semantics=semantics, vmem_limit_bytes=VMEM_LIMIT)


def _layer_norm(y, g, b):
    mu = jnp.mean(y, axis=-1, keepdims=True)
    d = y - mu
    var = jnp.mean(d * d, axis=-1, keepdims=True)
    return d * lax.rsqrt(var + LN_EPS) * g + b


def _dot(a, b):
    return jnp.dot(a, b, preferred_element_type=F32)


def _dot_nt(a, b):
    return lax.dot_general(a, b, (((1,), (1,)), ((), ())), preferred_element_type=F32)


def _lane_groups(x):
    return [x[:, j * V7X_LANES:(j + 1) * V7X_LANES] for j in range(x.shape[1] // V7X_LANES)]


def _lane_fold(x, op):
    return functools.reduce(op, _lane_groups(x))


def _exp2_minus(x, col):
    parts = [jnp.exp2(p - col) for p in _lane_groups(x)]
    return parts[0] if len(parts) == 1 else jnp.concatenate(parts, axis=1)


def _lane_bcast(col):
    return jnp.broadcast_to(col, (col.shape[0], V7X_LANES))


def _ffn_body(idx_ref, x_ref, wgu_hbm, wd_hbm, g_ref, b_ref, o_ref, xb_ref, acc_ref, wg_buf, wu_buf, wd_buf, sem,
              *, tf, n_tiles):
    i = pl.program_id(0)
    layer = idx_ref[0]
    which = idx_ref[1]
    chunks = [(off, min(tf, D_FF - off)) for off in range(0, D_FF, tf)]
    n_c = len(chunks)

    def chunk_copies(c):
        off, w = chunks[c]
        slot = c % 2
        return (
            pltpu.make_async_copy(wgu_hbm.at[layer, which, :, pl.ds(off, w)], wg_buf.at[slot, :, pl.ds(0, w)], sem.at[0, slot]),
            pltpu.make_async_copy(wgu_hbm.at[layer, which, :, pl.ds(D_FF + off, w)], wu_buf.at[slot, :, pl.ds(0, w)], sem.at[1, slot]),
            pltpu.make_async_copy(wd_hbm.at[layer, which, pl.ds(off, w), :], wd_buf.at[slot, pl.ds(0, w), :], sem.at[2, slot]),
        )

    def start(c):
        for cp in chunk_copies(c):
            cp.start()

    def wait(c):
        for cp in chunk_copies(c):
            cp.wait()

    @pl.when(i == 0)
    def _():
        start(0)

    xb_ref[...] = x_ref[...].astype(BF16)
    acc_ref[...] = jnp.zeros_like(acc_ref)
    wait(0)
    xb = xb_ref[...]
    for c in range(n_c):
        w = chunks[c][1]
        slot = c % 2
        if c + 1 < n_c:
            start(c + 1)
        else:
            @pl.when(i + 1 < n_tiles)
            def _():
                start(0)
        h = _dot(xb, wg_buf[slot, :, 0:w])
        u = _dot(xb, wu_buf[slot, :, 0:w])
        a = (h * jax.nn.sigmoid(h) * u).astype(BF16)
        acc_ref[...] += _dot(a, wd_buf[slot, 0:w, :])
        if c + 1 < n_c:
            wait(c + 1)

    y = ALPHA * x_ref[...] + 0.5 * acc_ref[...]
    o_ref[...] = _layer_norm(y, g_ref[...], b_ref[...])


def _ffn(x, w_gu, w_d, ln_g, ln_b, layer, which, ln_idx, *, tm=ROW_TILE, tf=FFN_TF):
    m = x.shape[0]
    n_tiles = m // tm
    assert pl.cdiv(D_FF, tf) % 2 == 0, "the next tile's chunk 0 must reuse the slot the last chunk leaves free"
    idx = jnp.array([layer, which, ln_idx], jnp.int32)
    ln_spec = pl.BlockSpec((None, None, 1, D_MODEL), lambda i, idx: (idx[0], idx[2], 0, 0))
    grid_spec = pltpu.PrefetchScalarGridSpec(
        num_scalar_prefetch=1,
        grid=(n_tiles,),
        in_specs=[
            pl.BlockSpec((tm, D_MODEL), lambda i, idx: (i, 0)),
            pl.BlockSpec(memory_space=pl.ANY),
            pl.BlockSpec(memory_space=pl.ANY),
            ln_spec, ln_spec,
        ],
        out_specs=pl.BlockSpec((tm, D_MODEL), lambda i, idx: (i, 0)),
        scratch_shapes=[
            pltpu.VMEM((tm, D_MODEL), BF16), pltpu.VMEM((tm, D_MODEL), F32),
            pltpu.VMEM((2, D_MODEL, tf), BF16), pltpu.VMEM((2, D_MODEL, tf), BF16),
            pltpu.VMEM((2, tf, D_MODEL), BF16),
            pltpu.SemaphoreType.DMA((3, 2)),
        ],
    )
    return pl.pallas_call(
        functools.partial(_ffn_body, tf=tf, n_tiles=n_tiles),
        out_shape=jax.ShapeDtypeStruct((m, D_MODEL), F32),
        grid_spec=grid_spec,
        compiler_params=_params(("arbitrary",)),
        name="ffn",
    )(idx, x, w_gu, w_d, ln_g, ln_b)


def _qkv_body(x_ref, w_ref, gn_ref, cos_ref, sin_ref, o_ref, *, tn):
    xb = x_ref[...].astype(BF16)
    cos = cos_ref[...]
    sin = sin_ref[...]
    starts = [j * tn for j in range(QKV_WIDTH // tn)]
    for c0 in sorted(starts, key=lambda c: not (QC_OFF <= c < VC_OFF)):
        acc = _dot(xb, w_ref[:, c0:c0 + tn])
        is_q = (QA_OFF <= c0 < KA_OFF) or (QB_OFF <= c0 < KB_OFF) or (QC_OFF <= c0 < KC_OFF)
        if QC_OFF <= c0 < VC_OFF:
            g = gn_ref[0:1, :] if c0 < KC_OFF else gn_ref[1:2, :]
            for hh in range(tn // HEAD_DIM):
                a = acc[:, hh * HEAD_DIM:(hh + 1) * HEAD_DIM]
                r = lax.rsqrt(jnp.mean(a * a, axis=-1, keepdims=True) + RMS_EPS)
                y = a * r * g
                y = y * cos + pltpu.roll(y, HEAD_DIM // 2, 1) * sin
                if is_q:
                    y = y * QSCALE
                o_ref[:, c0 + hh * HEAD_DIM:c0 + (hh + 1) * HEAD_DIM] = y.astype(BF16)
        else:
            if is_q:
                acc = acc * QSCALE
            o_ref[:, c0:c0 + tn] = acc.astype(BF16)


def _qkv(x, w_qkv, qk_norm_g, cos2, sin2, layer, *, tm=ROW_TILE, tn=QKV_TN):
    m = x.shape[0]
    s = cos2.shape[0]
    n_pos = s // tm
    return pl.pallas_call(
        functools.partial(_qkv_body, tn=tn),
        out_shape=jax.ShapeDtypeStruct((m, QKV_WIDTH), BF16),
        grid=(m // tm,),
        in_specs=[
            pl.BlockSpec((tm, D_MODEL), lambda i: (i, 0)),
            pl.BlockSpec((None, D_MODEL, QKV_WIDTH), lambda i: (layer, 0, 0), pipeline_mode=pl.Buffered(1)),
            pl.BlockSpec((None, 2, HEAD_DIM), lambda i: (layer, 0, 0)),
            pl.BlockSpec((tm, HEAD_DIM), lambda i: (i % n_pos, 0)),
            pl.BlockSpec((tm, HEAD_DIM), lambda i: (i % n_pos, 0)),
        ],
        out_specs=pl.BlockSpec((tm, QKV_WIDTH), lambda i: (i, 0)),
        compiler_params=_params(("parallel",)),
        name="qkv",
    )(x, w_qkv, qk_norm_g, cos2, sin2)


def _diff_body(q_ref, k_ref, v_ref, bias_ref, lam_ref, sg_ref, o_ref, s1_ref, s2_ref, m_ref, *, t, tk, nc, nq, n_items, lambda_init):
    g = pl.program_id(0)
    i = jnp.minimum(g, n_items - 1) % nq

    @pl.when(g == 0)
    def _():
        s1_ref[1] = jnp.zeros(s1_ref.shape[1:], F32)
        s2_ref[1] = jnp.zeros(s2_ref.shape[1:], F32)
        m_ref[1] = jnp.zeros(m_ref.shape[1:], F32)

    def step(cur, prv):
        q = q_ref[...]
        q1 = q[:, :HEAD_DIM]
        q2 = q[:, HEAD_DIM:]
        m1p = m_ref[prv, 0]
        m2p = m_ref[prv, 1]
        m1 = m2 = acc = l1 = l2 = None
        for c in range(nc):
            kc = k_ref[c * tk:(c + 1) * tk, :]
            b = bias_ref[jnp.clip(c - i * (t // tk), -2, t // tk + 1) + 2]
            s1 = _dot_nt(q1, kc[:, :HEAD_DIM]) + b
            s2 = _dot_nt(q2, kc[:, HEAD_DIM:]) + b
            s1_ref[cur, c] = s1
            s2_ref[cur, c] = s2
            f1 = _lane_fold(s1, jnp.maximum)
            f2 = _lane_fold(s2, jnp.maximum)
            m1 = f1 if m1 is None else jnp.maximum(m1, f1)
            m2 = f2 if m2 is None else jnp.maximum(m2, f2)

            e1 = _exp2_minus(s1_ref[prv, c], m1p)
            e2 = _exp2_minus(s2_ref[prv, c], m2p)
            e = jnp.concatenate([e1.astype(BF16), e2.astype(BF16)], axis=0)
            d = _dot(e, v_ref[c * tk:(c + 1) * tk, :])
            f1 = _lane_fold(e1, jnp.add)
            f2 = _lane_fold(e2, jnp.add)
            acc = d if acc is None else acc + d
            l1 = f1 if l1 is None else l1 + f1
            l2 = f2 if l2 is None else l2 + f2
        m_ref[cur, 0] = _lane_bcast(jnp.max(m1, axis=-1, keepdims=True))
        m_ref[cur, 1] = _lane_bcast(jnp.max(m2, axis=-1, keepdims=True))

        l1 = jnp.sum(l1, axis=-1, keepdims=True)
        l2 = jnp.sum(l2, axis=-1, keepdims=True)
        lam = lam_ref[...]
        lam_full = (jnp.exp(jnp.sum(lam[0:1] * lam[1:2], axis=-1, keepdims=True))
                    - jnp.exp(jnp.sum(lam[2:3] * lam[3:4], axis=-1, keepdims=True)) + lambda_init)
        o = acc[0:t, :] / l1 - lam_full * (acc[t:2 * t, :] / l2)
        o = o * lax.rsqrt(jnp.mean(o * o, axis=-1, keepdims=True) + RMS_EPS) * sg_ref[...]
        o_ref[...] = (o * (1.0 - lambda_init)).astype(BF16)

    @pl.when(g % 2 == 0)
    def _():
        step(0, 1)

    @pl.when(g % 2 == 1)
    def _():
        step(1, 0)


def _pipeline_items(n_items, n_heads, nq):
    def scored(g):
        it = jnp.minimum(g, n_items - 1)
        return it // (n_heads * nq), (it // nq) % n_heads, it % nq

    def finished(g):
        it = jnp.maximum(g - 1, 0)
        return it // (n_heads * nq), (it // nq) % n_heads, it % nq

    return scored, finished


def _diff_rows(s):
    return 2 * ATTN_T if 32 * ATTN_T * s <= 16 * 1024 * 1024 else ATTN_T


def _diff_attention(qkv, bias_a, lambda_qk, subln_g, layer, n_seq, s, lambda_init, *, tk=ATTN_TK):
    t = _diff_rows(s)
    m = qkv.shape[0]
    nc = s // tk
    nq = s // t
    n_tiles = t // tk + 4
    n_items = n_seq * A_HEADS * nq
    scored, finished = _pipeline_items(n_items, A_HEADS, nq)

    def q_map(g):
        b, h, i = scored(g)
        return (b * nq + i, QA_OFF // A_V + h)

    def k_map(g):
        b, h, _ = scored(g)
        return (b, KA_OFF // A_V + h)

    def v_map(g):
        b, h, _ = finished(g)
        return (b, VA_OFF // A_V + h)

    def o_map(g):
        b, h, i = finished(g)
        return (b * nq + i, h)

    return pl.pallas_call(
        functools.partial(_diff_body, t=t, tk=tk, nc=nc, nq=nq, n_items=n_items, lambda_init=lambda_init),
        out_shape=jax.ShapeDtypeStruct((m, BRANCH_WIDTH), BF16),
        grid=(n_items + 1,),
        in_specs=[
            pl.BlockSpec((t, A_V), q_map),
            pl.BlockSpec((s, A_V), k_map),
            pl.BlockSpec((s, A_V), v_map),
            pl.BlockSpec((None, n_tiles, t, tk), lambda g: (scored(g)[1], 0, 0, 0)),
            pl.BlockSpec((None, 4, HEAD_DIM), lambda g: (layer, 0, 0)),
            pl.BlockSpec((None, 1, A_V), lambda g: (layer, 0, 0)),
        ],
        out_specs=pl.BlockSpec((t, A_V), o_map),
        scratch_shapes=[pltpu.VMEM((2, nc, t, tk), F32), pltpu.VMEM((2, nc, t, tk), F32), pltpu.VMEM((2, 2, t, V7X_LANES), F32)],
        compiler_params=_params(("arbitrary",)),
        name="diff_attn",
    )(qkv, qkv, qkv, bias_a, lambda_qk, subln_g)


def _axial_body(q_ref, k_ref, v_ref, o_ref, s_ref, m_ref, *, t, tk, nc):
    gq = C_HEADS // C_KV_HEADS
    g = pl.program_id(0)
    half = gq // 2

    @pl.when(g == 0)
    def _():
        s_ref[1] = jnp.zeros(s_ref.shape[1:], F32)
        m_ref[1] = jnp.zeros(m_ref.shape[1:], F32)

    def step(cur, prv):
        q = q_ref[...]
        qs = [jnp.concatenate([q[:, h * HEAD_DIM:(h + 1) * HEAD_DIM] for h in range(r * half, (r + 1) * half)], axis=0)
              for r in range(2)]
        mp = [m_ref[prv, r] for r in range(2)]
        m = [None, None]
        acc = [None, None]
        l = [None, None]
        for c in range(nc):
            kc = k_ref[c * tk:(c + 1) * tk, :]
            vc = v_ref[c * tk:(c + 1) * tk, :]
            for r in range(2):
                sc = _dot_nt(qs[r], kc)
                s_ref[cur, r, c] = sc
                f = _lane_fold(sc, jnp.maximum)
                m[r] = f if m[r] is None else jnp.maximum(m[r], f)

                e = _exp2_minus(s_ref[prv, r, c], mp[r])
                d = _dot(e.astype(BF16), vc)
                f = _lane_fold(e, jnp.add)
                acc[r] = d if acc[r] is None else acc[r] + d
                l[r] = f if l[r] is None else l[r] + f
        for r in range(2):
            m_ref[cur, r] = _lane_bcast(jnp.max(m[r], axis=-1, keepdims=True))
            o = acc[r] / jnp.sum(l[r], axis=-1, keepdims=True)
            for hh in range(half):
                h = r * half + hh
                o_ref[:, h * HEAD_DIM:(h + 1) * HEAD_DIM] = o[hh * t:(hh + 1) * t, :].astype(BF16)

    @pl.when(g % 2 == 0)
    def _():
        step(0, 1)

    @pl.when(g % 2 == 1)
    def _():
        step(1, 0)


def _axial_attention(qkv, n_seq, s, *, t=ATTN_T, tk=ATTN_TK):
    m = qkv.shape[0]
    gq = C_HEADS // C_KV_HEADS
    gw = gq * HEAD_DIM
    nq = s // t
    nc = s // tk
    n_items = n_seq * C_KV_HEADS * nq
    scored, finished = _pipeline_items(n_items, C_KV_HEADS, nq)

    def q_map(g):
        b, kv, i = scored(g)
        return (b * nq + i, QC_OFF // gw + kv)

    def k_map(g):
        b, kv, _ = scored(g)
        return (b, KC_OFF // HEAD_DIM + kv)

    def v_map(g):
        b, kv, _ = finished(g)
        return (b, VC_OFF // HEAD_DIM + kv)

    def o_map(g):
        b, kv, i = finished(g)
        return (b * nq + i, kv)

    return pl.pallas_call(
        functools.partial(_axial_body, t=t, tk=tk, nc=nc),
        out_shape=jax.ShapeDtypeStruct((m, BRANCH_WIDTH), BF16),
        grid=(n_items + 1,),
        in_specs=[
            pl.BlockSpec((t, gw), q_map),
            pl.BlockSpec((s, HEAD_DIM), k_map),
            pl.BlockSpec((s, HEAD_DIM), v_map),
        ],
        out_specs=pl.BlockSpec((t, gw), o_map),
        scratch_shapes=[pltpu.VMEM((2, 2, nc, gq // 2 * t, tk), F32), pltpu.VMEM((2, 2, gq // 2 * t, V7X_LANES), F32)],
        compiler_params=_params(("arbitrary",)),
        name="axial_attn",
    )(qkv, qkv, qkv)


def _window_body(sink_ref, q_ref, k_ref, v_ref, bias_ref, o_ref, *, nb):
    g = B_HEADS // B_KV_HEADS
    t = WINDOW
    kv = pl.program_id(1)
    sink = jnp.concatenate([jnp.full((1, t), sink_ref[kv * g + h] * LOG2E, F32) for h in range(g)], axis=1)

    def block(i, carry):
        r0 = pl.multiple_of(i * t, t)
        w0 = pl.multiple_of(jnp.clip(i - 1, 0, nb - 3) * t, t)
        tile = jnp.where(i == 0, 0, jnp.where(i == nb - 1, 2, 1))
        q = q_ref[pl.ds(r0, t), :]
        qs = jnp.concatenate([q[:, h * HEAD_DIM:(h + 1) * HEAD_DIM] for h in range(g)], axis=0)
        st = _dot_nt(k_ref[pl.ds(w0, 3 * t), :], qs) + bias_ref[tile]
        mx = jnp.maximum(jnp.max(st, axis=0, keepdims=True), sink)
        e = jnp.exp2(st - mx)
        l = jnp.sum(e, axis=0, keepdims=True) + jnp.exp2(sink - mx)
        p = (e * (1.0 / l)).astype(BF16)
        ot = lax.dot_general(v_ref[pl.ds(w0, 3 * t), :], p, (((0,), (0,)), ((), ())),
                             preferred_element_type=F32)
        for h in range(g):
            o_ref[pl.ds(r0, t), h * HEAD_DIM:(h + 1) * HEAD_DIM] = ot[:, h * t:(h + 1) * t].T.astype(BF16)
        return carry

    lax.fori_loop(0, nb, block, 0, unroll=4)


def _window_attention(qkv, bias_b, sink_logits, layer, n_seq, s):
    m = qkv.shape[0]
    g = B_HEADS // B_KV_HEADS
    gw = g * HEAD_DIM
    t = WINDOW
    nb = s // t
    grid_spec = pltpu.PrefetchScalarGridSpec(
        num_scalar_prefetch=1,
        grid=(n_seq, B_KV_HEADS),
        in_specs=[
            pl.BlockSpec((s, gw), lambda b, kv, sink: (b, QB_OFF // gw + kv)),
            pl.BlockSpec((s, HEAD_DIM), lambda b, kv, sink: (b, KB_OFF // HEAD_DIM + kv)),
            pl.BlockSpec((s, HEAD_DIM), lambda b, kv, sink: (b, VB_OFF // HEAD_DIM + kv)),
            pl.BlockSpec((3, None, 3 * t, g * t), lambda b, kv, sink: (0, kv, 0, 0)),
        ],
        out_specs=pl.BlockSpec((s, gw), lambda b, kv, sink: (b, kv)),
    )
    return pl.pallas_call(
        functools.partial(_window_body, nb=nb),
        out_shape=jax.ShapeDtypeStruct((m, BRANCH_WIDTH), BF16),
        grid_spec=grid_spec,
        compiler_params=_params(("parallel", "parallel")),
        name="window_attn",
    )(sink_logits[layer], qkv, qkv, qkv, bias_b)


def _gate_body(x_ref, ya_ref, yb_ref, yc_ref, wga_ref, wgb_ref, wgc_ref, wb_ref, o_ref, xb_ref):
    @pl.when(pl.program_id(1) == 0)
    def _():
        xb_ref[...] = x_ref[...].astype(BF16)

    xb = xb_ref[...]
    merged = None
    for n, (y_ref, wg_ref) in enumerate(((ya_ref, wga_ref), (yb_ref, wgb_ref), (yc_ref, wgc_ref))):
        gate = jax.nn.sigmoid(_dot(xb, wg_ref[...]))
        term = gate * _dot(y_ref[...], wb_ref[n])
        merged = term if merged is None else merged + term
    o_ref[...] = merged.astype(BF16)


def _outproj_body(x_ref, m_ref, wo_ref, g_ref, b_ref, o_ref, *, n_split):
    rows = x_ref.shape[0] // n_split
    for r in range(n_split):
        sl = slice(r * rows, (r + 1) * rows)
        y = ALPHA * x_ref[sl, :] + _dot(m_ref[sl, :], wo_ref[...])
        o_ref[sl, :] = _layer_norm(y, g_ref[...], b_ref[...])


def _merge(x, ya, yb, yc, w_in, w_branch, w_out, ln_g, ln_b, layer, *, tm=ROW_TILE, tc=GATE_TC):
    m = x.shape[0]
    yspec = pl.BlockSpec((tm, BRANCH_WIDTH), lambda i, j: (i, 0))

    def gate_spec(n):
        return pl.BlockSpec((None, D_MODEL, tc), lambda i, j: (layer, 0, (QKV_WIDTH + n * D_MODEL) // tc + j))

    merged = pl.pallas_call(
        _gate_body,
        out_shape=jax.ShapeDtypeStruct((m, D_MODEL), BF16),
        grid=(m // tm, D_MODEL // tc),
        in_specs=[
            pl.BlockSpec((tm, D_MODEL), lambda i, j: (i, 0)),
            yspec, yspec, yspec,
            gate_spec(0), gate_spec(1), gate_spec(2),
            pl.BlockSpec((None, N_BRANCH, BRANCH_WIDTH, tc), lambda i, j: (layer, 0, 0, j)),
        ],
        out_specs=pl.BlockSpec((tm, tc), lambda i, j: (i, j)),
        scratch_shapes=[pltpu.VMEM((tm, D_MODEL), BF16)],
        compiler_params=_params(("parallel", "arbitrary")),
        name="gate_merge",
    )(x, ya, yb, yc, w_in, w_in, w_in, w_branch)
    return pl.pallas_call(
        functools.partial(_outproj_body, n_split=2),
        out_shape=jax.ShapeDtypeStruct((m, D_MODEL), F32),
        grid=(m // tm,),
        in_specs=[
            pl.BlockSpec((tm, D_MODEL), lambda i: (i, 0)),
            pl.BlockSpec((tm, D_MODEL), lambda i: (i, 0)),
            pl.BlockSpec((None, D_MODEL, D_MODEL), lambda i: (layer, 0, 0), pipeline_mode=pl.Buffered(1)),
            pl.BlockSpec((None, None, 1, D_MODEL), lambda i: (layer, 1, 0, 0)),
            pl.BlockSpec((None, None, 1, D_MODEL), lambda i: (layer, 1, 0, 0)),
        ],
        out_specs=pl.BlockSpec((tm, D_MODEL), lambda i: (i, 0)),
        compiler_params=_params(("parallel",)),
        name="out_proj",
    )(x, merged, w_out, ln_g, ln_b)


def _memkv_body(m_ref, w_ref, o_ref):
    o_ref[...] = _dot(m_ref[...].astype(BF16), w_ref[...]).astype(BF16)


def _memkv(mem, w_kv, layer):
    rows = mem.shape[0]
    return pl.pallas_call(
        _memkv_body,
        out_shape=jax.ShapeDtypeStruct((rows, 2 * MEM_INNER), BF16),
        grid=(rows // MEM_TOKENS,),
        in_specs=[
            pl.BlockSpec((MEM_TOKENS, D_MODEL), lambda i: (i, 0)),
            pl.BlockSpec((None, D_MODEL, 2 * MEM_INNER), lambda i: (layer, 0, 0)),
        ],
        out_specs=pl.BlockSpec((MEM_TOKENS, 2 * MEM_INNER), lambda i: (i, 0)),
        compiler_params=_params(("parallel",)),
        name="memkv",
    )(mem, w_kv)


def _mem_body(x_ref, mem_ref, wkv_ref, wq_ref, wo_ref, g_ref, b_ref, o_ref, kv_ref, *, n_split, per_seq):
    @pl.when(pl.program_id(0) % per_seq == 0)
    def _():
        kv_ref[...] = _dot(mem_ref[...].astype(BF16), wkv_ref[...]).astype(BF16)

    kv = kv_ref[...]
    rows = x_ref.shape[0] // n_split
    for r in range(n_split):
        sl = slice(r * rows, (r + 1) * rows)
        x = x_ref[sl, :]
        q = (_dot(x.astype(BF16), wq_ref[...]) * QSCALE).astype(BF16)
        heads = []
        for h in range(MEM_HEADS):
            sc = _dot_nt(q[:, h * HEAD_DIM:(h + 1) * HEAD_DIM], kv[:, h * HEAD_DIM:(h + 1) * HEAD_DIM])
            e = jnp.exp2(sc - jnp.max(sc, axis=-1, keepdims=True))
            l = jnp.sum(e, axis=-1, keepdims=True)
            v = kv[:, MEM_INNER + h * HEAD_DIM:MEM_INNER + (h + 1) * HEAD_DIM]
            heads.append((_dot(e.astype(BF16), v) / l).astype(BF16))
        o = jnp.concatenate(heads, axis=1)
        y = ALPHA * x + _dot(o, wo_ref[...])
        o_ref[sl, :] = _layer_norm(y, g_ref[...], b_ref[...])


def _mem_attention(x, mem, w_kv, w_q, w_o, ln_g, ln_b, layer, s, *, tm=ROW_TILE):
    m = x.shape[0]
    per_seq = s // tm
    return pl.pallas_call(
        functools.partial(_mem_body, n_split=1, per_seq=per_seq),
        out_shape=jax.ShapeDtypeStruct((m, D_MODEL), F32),
        grid=(m // tm,),
        in_specs=[
            pl.BlockSpec((tm, D_MODEL), lambda i: (i, 0)),
            pl.BlockSpec((MEM_TOKENS, D_MODEL), lambda i: (i // per_seq, 0)),
            pl.BlockSpec((None, D_MODEL, 2 * MEM_INNER), lambda i: (layer, 0, 0), pipeline_mode=pl.Buffered(1)),
            pl.BlockSpec((None, D_MODEL, MEM_INNER), lambda i: (layer, 0, 0)),
            pl.BlockSpec((None, MEM_INNER, D_MODEL), lambda i: (layer, 0, 0)),
            pl.BlockSpec((None, None, 1, D_MODEL), lambda i: (layer, 2, 0, 0)),
            pl.BlockSpec((None, None, 1, D_MODEL), lambda i: (layer, 2, 0, 0)),
        ],
        out_specs=pl.BlockSpec((tm, D_MODEL), lambda i: (i, 0)),
        scratch_shapes=[pltpu.VMEM((MEM_TOKENS, 2 * MEM_INNER), BF16)],
        compiler_params=_params(("arbitrary",)),
        name="mem_attn",
    )(x, mem, w_kv, w_q, w_o, ln_g, ln_b)


def _t5_bucket_np(rel):
    half = NUM_BUCKETS // 2
    max_exact = half // 2
    rel = np.asarray(rel, np.int32)
    ret = np.where(rel > 0, half, 0)
    n = np.abs(rel)
    ratio = np.log(np.maximum(n, 1).astype(np.float32) / np.float32(max_exact)) / np.float32(math.log(MAX_DISTANCE / max_exact))
    large = max_exact + (ratio * np.float32(half - max_exact)).astype(np.int32)
    large = np.minimum(large, half - 1)
    return (ret + np.where(n < max_exact, n, large)).astype(np.int32)


def _lookup_bias(table, buckets):
    n_heads = table.shape[1]
    b = jnp.asarray(buckets)[None]
    tab = table * LOG2E
    out = jnp.zeros((n_heads,) + buckets.shape, F32)
    for k in range(NUM_BUCKETS):
        out = jnp.where(b == k, tab[k].reshape((n_heads,) + (1,) * buckets.ndim), out)
    return out


def _diff_bias_tiles(table_a, t, tk):
    assert tk >= MAX_DISTANCE and t % tk == 0
    ii = np.arange(t)[:, None]
    jj = np.arange(tk)[None, :]
    buckets = np.stack([_t5_bucket_np(d * tk + jj - ii) for d in range(-2, t // tk + 2)])
    return _lookup_bias(table_a, buckets)


def _window_bias_tiles(table_b):
    g = B_HEADS // B_KV_HEADS
    rel = np.stack([(np.arange(3 * WINDOW) - p * WINDOW)[None, :] - np.arange(WINDOW)[:, None] for p in range(3)])
    bias = _lookup_bias(table_b, _t5_bucket_np(rel))
    bias = jnp.where(jnp.asarray(np.abs(rel) <= WINDOW)[None], bias, NEG_INF)
    bias = bias.reshape(B_KV_HEADS, g, 3, WINDOW, 3 * WINDOW)
    return jnp.transpose(bias, (2, 0, 4, 1, 3)).reshape(3, B_KV_HEADS, 3 * WINDOW, g * WINDOW)


def _rope_tables(s):
    n_rows = s // GRID_W
    row = np.repeat(np.arange(n_rows), GRID_W).astype(np.float32)
    col = np.tile(np.arange(GRID_W), n_rows).astype(np.float32)
    axis_dims = HEAD_DIM // 2
    inv = ROPE_THETA ** (-jnp.arange(0, axis_dims, 2, dtype=F32) / axis_dims)
    ang = jnp.concatenate([row[:, None] * inv[None, :], col[:, None] * inv[None, :]], axis=-1)
    cos, sin = jnp.cos(ang), jnp.sin(ang)
    return jnp.concatenate([cos, cos], axis=-1), jnp.concatenate([-sin, sin], axis=-1)


def _run_trunk(x, mem, wts, table_a, bias_b):
    n_seq, s, _ = x.shape
    x = x.reshape(n_seq * s, D_MODEL)
    mem = mem.reshape(n_seq * MEM_TOKENS, D_MODEL)
    cos2, sin2 = _rope_tables(s)
    bias_a = _diff_bias_tiles(table_a, _diff_rows(s), ATTN_TK)
    for i in range(DEPTH):
        lambda_init = 0.8 - 0.6 * math.exp(-0.3 * i)
        x = _ffn(x, wts["w_ffn_gu"], wts["w_ffn_d"], wts["ln_g"], wts["ln_b"], i, 0, 0)
        qkv = _qkv(x, wts["w_in"], wts["qk_norm_g"], cos2, sin2, i)
        ya = _diff_attention(qkv, bias_a, wts["lambda_qk"], wts["subln_g"], i, n_seq, s, lambda_init)
        yb = _window_attention(qkv, bias_b, wts["sink_logits"], i, n_seq, s)
        yc = _axial_attention(qkv, n_seq, s)
        x = _merge(x, ya, yb, yc, wts["w_in"], wts["w_branch"], wts["w_out"], wts["ln_g"], wts["ln_b"], i)
        x = _mem_attention(x, mem, wts["w_mem_kv"], wts["w_mem_q"], wts["w_mem_o"], wts["ln_g"], wts["ln_b"], i, s)
        x = _ffn(x, wts["w_ffn_gu"], wts["w_ffn_d"], wts["ln_g"], wts["ln_b"], i, 1, 3)
    return x.reshape(n_seq, s, D_MODEL)


def kernel(x_prompt, x_sample, mem_prompt, mem_sample, rel_bias_table, w_in, w_branch, w_out, lambda_qk, subln_g, sink_logits, qk_norm_g, w_mem_q, w_mem_kv, w_mem_o, w_ffn_gu, w_ffn_d, ln_g, ln_b):
    wts = {
        "w_in": w_in.astype(BF16),
        "w_branch": w_branch.astype(BF16),
        "w_out": w_out.astype(BF16),
        "w_mem_q": w_mem_q.astype(BF16),
        "w_mem_kv": w_mem_kv.astype(BF16),
        "w_mem_o": w_mem_o.astype(BF16),
        "w_ffn_gu": w_ffn_gu.astype(BF16),
        "w_ffn_d": w_ffn_d.astype(BF16),
        "lambda_qk": lambda_qk,
        "subln_g": subln_g.reshape(DEPTH, 1, A_V),
        "sink_logits": sink_logits,
        "qk_norm_g": qk_norm_g,
        "ln_g": ln_g.reshape(DEPTH, 4, 1, D_MODEL),
        "ln_b": ln_b.reshape(DEPTH, 4, 1, D_MODEL),
    }
    table_a = rel_bias_table[:, :A_HEADS]
    bias_b = _window_bias_tiles(rel_bias_table[:, A_HEADS:])
    y_prompt = _run_trunk(x_prompt, mem_prompt, wts, table_a, bias_b)
    y_sample = _run_trunk(x_sample, mem_sample, wts, table_a, bias_b)
    return (y_prompt, y_sample)
```
